```python
import math, functools
import jax, jax.numpy as jnp
from jax import lax
import numpy as np

D_MODEL = 1024
BATCH = 4
SEQ = 4096
DEPTH = 4

GRID_W = 64
CTX_LEN = 256
N_MIXERS = 4
EPS = 1e-6
CONV_W = 5
NEG_BIG = -1e30

N_SSD = (DEPTH + 3) // 4
N_HGRN = (DEPTH + 2) // 4
N_ATTN = (DEPTH + 1) // 4
N_MLSTM = DEPTH // 4
N_DENSE = (DEPTH + 1) // 2
N_MOE = DEPTH // 2

SSD_INNER = 2 * D_MODEL
SSD_HEAD_DIM = 64
SSD_HEADS = SSD_INNER // SSD_HEAD_DIM
SSD_STATE = 128
SSD_GROUPS = 8
SSD_R = SSD_HEADS // SSD_GROUPS
SSD_CHUNK = 128
SSD_CONV_DIM = SSD_INNER + 2 * SSD_GROUPS * SSD_STATE
SSD_IN_DIM = SSD_INNER + SSD_CONV_DIM + 2 * SSD_HEADS

HGRN_DK = 128
HGRN_HEADS = D_MODEL // HGRN_DK
HGRN_DV = D_MODEL // HGRN_HEADS
HGRN_CHUNK = 64

HEAD_DIM = 64
N_HEADS = D_MODEL // HEAD_DIM
N_KV = N_HEADS // 4
GQA_R = N_HEADS // N_KV
ATTN_SCALE = HEAD_DIM ** -0.5
ROPE_THETA = 10000.0
ROPE_FREQS = HEAD_DIM // 4
Q_BLOCK = 128

MLSTM_INNER = 2 * D_MODEL
MLSTM_HEADS = 4
MLSTM_DH = MLSTM_INNER // MLSTM_HEADS
MLSTM_CHUNK = 128

D_FF = 7 * D_MODEL // 2
N_EXPERTS = 8
TOP_K = 2

kernel_name = 'hybrid_diffusion_trunk'


def rms_norm(x, g):
    xf = x.astype(jnp.float32)
    y = xf * lax.rsqrt(jnp.mean(xf * xf, axis=-1, keepdims=True) + EPS)
    return (y * g.astype(jnp.float32)).astype(x.dtype)


def modulate(h, shift, scale):
    return h * (1.0 + scale) + shift


def dw_conv_centred(x, w, b):
    pad = w.shape[0] // 2
    y = lax.conv_general_dilated(x, w[:, None, :], window_strides=(1,), padding=[(pad, pad)],
                                 dimension_numbers=('NWC', 'WIO', 'NWC'), feature_group_count=x.shape[-1])
    return y + b


def axial_rope(seq_len):
    rows = seq_len // GRID_W
    row = jnp.repeat(jnp.arange(rows, dtype=jnp.float32), GRID_W)
    col = jnp.tile(jnp.arange(GRID_W, dtype=jnp.float32), rows)
    inv = ROPE_THETA ** (-jnp.arange(ROPE_FREQS, dtype=jnp.float32) / ROPE_FREQS)
    ang_r = row[:, None] * inv
    ang_c = col[:, None] * inv
    ang = jnp.concatenate([ang_r, ang_r, ang_c, ang_c], axis=-1)
    return jnp.cos(ang)[:, None, :], jnp.sin(ang)[:, None, :]


def apply_rope(x, cos, sin):
    xf = x.astype(jnp.float32)
    xr = xf.reshape(*x.shape[:-1], 2, 2, ROPE_FREQS)
    rot = jnp.stack([-xr[..., 1, :], xr[..., 0, :]], axis=-2).reshape(x.shape)
    return (xf * cos + rot * sin).astype(x.dtype)


def bidir_scan(scan_fn, ctx_f, ctx_b, lat_f, lat_b, par_f, par_b, init, want_ctx):
    rev = lambda seqs: tuple(jnp.flip(a, axis=1) for a in seqs)
    yc_f, s_f = scan_fn(ctx_f, par_f, init)
    yc_b, s_b = scan_fn(rev(ctx_b), par_b, init)
    yx_f, _ = scan_fn(lat_f, par_f, s_f)
    yx_b, _ = scan_fn(rev(lat_b), par_b, s_b)
    y_lat = yx_f + jnp.flip(yx_b, axis=1)
    y_ctx = (yc_f + jnp.flip(yc_b, axis=1)) if want_ctx else None
    return y_lat, y_ctx


def _chunks(a, length):
    bsz, t = a.shape[:2]
    return jnp.moveaxis(a.reshape(bsz, t // length, length, *a.shape[2:]), 1, 0)


def _unchunk(a):
    a = jnp.moveaxis(a, 0, 1)
    return a.reshape(a.shape[0], a.shape[1] * a.shape[2], *a.shape[3:])


def ssd_scan(seqs, par, s0):
    xs, dt, bm, cm = seqs
    (a_gr,) = par
    causal = jnp.tril(jnp.ones((SSD_CHUNK, SSD_CHUNK), dtype=bool))[None, :, :, None, None]

    def step(s, inp):
        xc, dtc, bc, cc = inp
        a = jnp.cumsum(dtc * a_gr, axis=1)
        rel = jnp.where(causal, a[:, :, None] - a[:, None], -jnp.inf)
        cb = jnp.einsum('blgn,bsgn->blsg', cc, bc)
        y = jnp.einsum('blsgr,bsgrp->blgrp', cb[..., None] * jnp.exp(rel) * dtc[:, None], xc)
        y = y + jnp.einsum('blgn,bgrpn->blgrp', cc, s) * jnp.exp(a)[..., None]
        a_end = a[:, -1]
        s_new = jnp.exp(a_end)[..., None, None] * s + jnp.einsum(
            'blgr,blgn,blgrp->bgrpn', jnp.exp(a_end[:, None] - a) * dtc, bc, xc)
        return s_new, y

    s_fin, y = lax.scan(step, s0, tuple(_chunks(t, SSD_CHUNK) for t in (xs, dt, bm, cm)))
    return _unchunk(y), s_fin


def ssd_mixer(hx, hc, w_in, conv_w, conv_b, dt_bias, a_log, d_skip, norm_g, w_out, want_ctx):
    a_neg = -jnp.exp(a_log.astype(jnp.float32)).reshape(2, SSD_GROUPS, SSD_R)

    def project(h):
        bsz, t = h.shape[:2]
        z, xbc, dt = jnp.split(h @ w_in, [SSD_INNER, SSD_INNER + SSD_CONV_DIM], axis=-1)
        xbc = jax.nn.silu(dw_conv_centred(xbc, conv_w, conv_b)).astype(jnp.float32)
        xs, bm, cm = jnp.split(xbc, [SSD_INNER, SSD_INNER + SSD_GROUPS * SSD_STATE], axis=-1)
        xs = xs.reshape(bsz, t, SSD_GROUPS, SSD_R, SSD_HEAD_DIM)
        bm = bm.reshape(bsz, t, SSD_GROUPS, SSD_STATE)
        cm = cm.reshape(bsz, t, SSD_GROUPS, SSD_STATE)
        dt = jax.nn.softplus(dt.astype(jnp.float32).reshape(bsz, t, 2, SSD_GROUPS, SSD_R)
                             + dt_bias.astype(jnp.float32).reshape(2, SSD_GROUPS, SSD_R))
        return z, xs, bm, cm, dt

    zx, xx, bx, cx, dtx = project(hx)
    zc, xc, bc, cc, dtc = project(hc)
    s0 = jnp.zeros((hx.shape[0], SSD_GROUPS, SSD_R, SSD_HEAD_DIM, SSD_STATE), jnp.float32)
    y_x, y_c = bidir_scan(ssd_scan,
                          (xc, dtc[:, :, 0], bc, cc), (xc, dtc[:, :, 1], bc, cc),
                          (xx, dtx[:, :, 0], bx, cx), (xx, dtx[:, :, 1], bx, cx),
                          (a_neg[0],), (a_neg[1],), s0, want_ctx)
    d = d_skip.astype(jnp.float32).reshape(SSD_GROUPS, SSD_R, 1)

    def finish(y, xs, z):
        bsz, t = y.shape[:2]
        y = (y + d * xs).reshape(bsz, t, SSD_INNER).astype(z.dtype) * jax.nn.silu(z)
        y = rms_norm(y.reshape(bsz, t, SSD_GROUPS, SSD_INNER // SSD_GROUPS), norm_g.reshape(SSD_GROUPS, -1))
        return y.reshape(bsz, t, SSD_INNER) @ w_out

    return finish(y_x, xx, zx), (finish(y_c, xc, zc) if want_ctx else None)


def gla_scan(seqs, par, s0):
    q, k, lf, v = seqs
    causal = jnp.tril(jnp.ones((HGRN_CHUNK, HGRN_CHUNK), dtype=bool))[None, :, :, None, None]

    def step(s, inp):
        qc, kc, lfc, vc = inp
        cum = jnp.cumsum(lfc, axis=1)
        rel = jnp.where(causal, cum[:, :, None] - cum[:, None], -jnp.inf)
        att = jnp.einsum('blhd,bshd,blshd->bhls', qc, kc, jnp.exp(rel))
        o = jnp.einsum('bhls,bshv->blhv', att, vc) + jnp.einsum('blhd,bhdv->blhv', qc * jnp.exp(cum), s)
        c_end = cum[:, -1]
        s_new = jnp.exp(c_end)[..., None] * s + jnp.einsum(
            'blhd,blhv->bhdv', kc * jnp.exp(c_end[:, None] - cum), vc)
        return s_new, o

    s_fin, o = lax.scan(step, s0, tuple(_chunks(t, HGRN_CHUNK) for t in (q, k, lf, v)))
    return _unchunk(o), s_fin


def hgrn2_mixer(hx, hc, w_in, lower_bound, norm_g, w_out, want_ctx):
    lb = lower_bound.astype(jnp.float32).reshape(2, HGRN_HEADS, HGRN_DK)

    def project(h):
        bsz, t = h.shape[:2]
        q, f_fw, f_bw, i, g = jnp.split(h @ w_in, 5, axis=-1)
        heads = lambda a, d: a.astype(jnp.float32).reshape(bsz, t, HGRN_HEADS, d)
        q = jax.nn.silu(heads(q, HGRN_DK)) * HGRN_DK ** -0.5

        def gates(f, lb_d):
            f = heads(f, HGRN_DK)
            log_f = jnp.logaddexp(jnp.log(lb_d), jnp.log1p(-lb_d) + jax.nn.log_sigmoid(f))
            return (1.0 - lb_d) * jax.nn.sigmoid(-f), log_f

        k_fw, lf_fw = gates(f_fw, lb[0])
        k_bw, lf_bw = gates(f_bw, lb[1])
        return q, k_fw, lf_fw, k_bw, lf_bw, heads(i, HGRN_DV), g

    qx, kxf, lxf, kxb, lxb, vx, gx = project(hx)
    qc, kcf, lcf, kcb, lcb, vc, gc = project(hc)
    s0 = jnp.zeros((hx.shape[0], HGRN_HEADS, HGRN_DK, HGRN_DV), jnp.float32)
    o_x, o_c = bidir_scan(gla_scan, (qc, kcf, lcf, vc), (qc, kcb, lcb, vc),
                          (qx, kxf, lxf, vx), (qx, kxb, lxb, vx), (), (), s0, want_ctx)

    def finish(o, g):
        bsz, t = o.shape[:2]
        o = rms_norm(o, norm_g.reshape(HGRN_HEADS, HGRN_DV)).reshape(bsz, t, D_MODEL).astype(g.dtype)
        return (o * jax.nn.silu(g)) @ w_out

    return finish(o_x, gx), (finish(o_c, gc) if want_ctx else None)


def gqa_attend(q, k, v):
    s = jnp.einsum('bqgrd,bsgd->bgrqs', q, k).astype(jnp.float32) * ATTN_SCALE
    p = jax.nn.softmax(s, axis=-1).astype(v.dtype)
    return jnp.einsum('bgrqs,bsgd->bqgrd', p, v)


def attn_mixer(hx, hc, w_qkv, q_g, k_g, w_o, cos, sin, want_ctx):
    def project(h):
        bsz, t = h.shape[:2]
        q, k, v = jnp.split(h @ w_qkv, [N_HEADS * HEAD_DIM, (N_HEADS + N_KV) * HEAD_DIM], axis=-1)
        q = rms_norm(q.reshape(bsz, t, N_HEADS, HEAD_DIM), q_g)
        k = rms_norm(k.reshape(bsz, t, N_KV, HEAD_DIM), k_g)
        return q, k, v.reshape(bsz, t, N_KV, HEAD_DIM)

    qx, kx, vx = project(hx)
    qc, kc, vc = project(hc)
    qx = apply_rope(qx, cos, sin)
    kx = apply_rope(kx, cos, sin)
    k_all = jnp.concatenate([kx, kc], axis=1)
    v_all = jnp.concatenate([vx, vc], axis=1)
    bsz, t = hx.shape[:2]
    qb = jnp.moveaxis(qx.reshape(bsz, t // Q_BLOCK, Q_BLOCK, N_KV, GQA_R, HEAD_DIM), 1, 0)
    ox = lax.map(lambda qblk: gqa_attend(qblk, k_all, v_all), qb)
    ox = jnp.moveaxis(ox, 0, 1).reshape(bsz, t, N_HEADS * HEAD_DIM) @ w_o
    oc = None
    if want_ctx:
        n_ctx = hc.shape[1]
        oc = gqa_attend(qc.reshape(bsz, n_ctx, N_KV, GQA_R, HEAD_DIM), kc, vc)
        oc = oc.reshape(bsz, n_ctx, N_HEADS * HEAD_DIM) @ w_o
    return ox, oc


def mlstm_scan(seqs, par, s0):
    q, k, v, li, lf = seqs
    causal = jnp.tril(jnp.ones((MLSTM_CHUNK, MLSTM_CHUNK), dtype=bool))[None, :, :, None]

    def step(carry, inp):
        c, n, m = carry
        qc, kc, vc, lic, lfc = inp
        cum = jnp.cumsum(lfc, axis=1)
        dmat = jnp.where(causal, cum[:, :, None] - cum[:, None] + lic[:, None], -jnp.inf)
        inter = cum + m[:, None]
        m_t = jnp.maximum(inter, jnp.max(dmat, axis=2))
        w = jnp.exp(dmat - m_t[:, :, None])
        w_c = jnp.exp(inter - m_t)
        qk = jnp.einsum('blhd,bshd->blsh', qc, kc) * w
        num = jnp.einsum('blsh,bshe->blhe', qk, vc) + w_c[..., None] * jnp.einsum('blhd,bhde->blhe', qc, c)
        den = jnp.sum(qk, axis=2) + w_c * jnp.einsum('blhd,bhd->blh', qc, n)
        h = num / jnp.maximum(jnp.abs(den), jnp.exp(-m_t))[..., None]
        w_end = cum[:, -1:] - cum + lic
        m_new = jnp.maximum(cum[:, -1] + m, jnp.max(w_end, axis=1))
        e_end = jnp.exp(w_end - m_new[:, None])
        a_old = jnp.exp(cum[:, -1] + m - m_new)
        c_new = a_old[..., None, None] * c + jnp.einsum('blh,blhd,blhe->bhde', e_end, kc, vc)
        n_new = a_old[..., None] * n + jnp.einsum('blh,blhd->bhd', e_end, kc)
        return (c_new, n_new, m_new), h

    s_fin, h = lax.scan(step, s0, tuple(_chunks(t, MLSTM_CHUNK) for t in (q, k, v, li, lf)))
    return _unchunk(h), s_fin


def mlstm_mixer(hx, hc, w_up, conv_w, conv_b, w_q, w_k, w_v, w_gate, b_gate, skip, norm_g, w_down, want_ctx):
    def project(h):
        bsz, t = h.shape[:2]
        xm, z = jnp.split(h @ w_up, 2, axis=-1)
        xc = jax.nn.silu(dw_conv_centred(xm, conv_w, conv_b))
        heads = lambda a: a.reshape(bsz, t, MLSTM_HEADS, MLSTM_DH)
        q = jnp.einsum('bthd,hde->bthe', heads(xc), w_q)
        k = jnp.einsum('bthd,hde->bthe', heads(xc), w_k)
        v = jnp.einsum('bthd,hde->bthe', heads(xm), w_v)
        qkv = jnp.concatenate([q, k, v], axis=2).reshape(bsz, t, 3 * MLSTM_INNER)
        gt = (qkv @ w_gate + b_gate).astype(jnp.float32).reshape(bsz, t, 4, MLSTM_HEADS)
        q = q.astype(jnp.float32)
        k = k.astype(jnp.float32) * MLSTM_DH ** -0.5
        v = v.astype(jnp.float32)
        fw = (q, k, v, gt[:, :, 0], jax.nn.log_sigmoid(gt[:, :, 1]))
        bw = (q, k, v, gt[:, :, 2], jax.nn.log_sigmoid(gt[:, :, 3]))
        return fw, bw, xc, z

    fx, bx, xcx, zx = project(hx)
    fc, bc, xcc, zc = project(hc)
    bsz = hx.shape[0]
    s0 = (jnp.zeros((bsz, MLSTM_HEADS, MLSTM_DH, MLSTM_DH), jnp.float32),
          jnp.zeros((bsz, MLSTM_HEADS, MLSTM_DH), jnp.float32),
          jnp.full((bsz, MLSTM_HEADS), NEG_BIG, jnp.float32))
    h_x, h_c = bidir_scan(mlstm_scan, fc, bc, fx, bx, (), (), s0, want_ctx)

    def finish(h, xc, z):
        bsz_, t = h.shape[:2]
        hn = rms_norm(h, norm_g.reshape(MLSTM_HEADS, MLSTM_DH)).reshape(bsz_, t, MLSTM_INNER).astype(xc.dtype)
        return ((hn + skip * xc) * jax.nn.silu(z)) @ w_down

    return finish(h_x, xcx, zx), (finish(h_c, xcc, zc) if want_ctx else None)


def swiglu(h, w1, w3, w2):
    return (jax.nn.silu(h @ w1) * (h @ w3)) @ w2


def moe_swiglu(h, w_router, w1, w3, w2):
    logits = (h @ w_router).astype(jnp.float32)
    top_val, top_idx = lax.top_k(logits, TOP_K)
    top_w = jax.nn.softmax(top_val, axis=-1)
    combine = jnp.einsum('btk,btke->bte', top_w, jax.nn.one_hot(top_idx, N_EXPERTS, dtype=jnp.float32)).astype(h.dtype)
    out = jnp.zeros_like(h)
    for e in range(N_EXPERTS):
        out = out + combine[..., e:e + 1] * swiglu(h, w1[e], w3[e], w2[e])
    return out


def setup_inputs(seed: int = 0) -> dict:
    key = jax.random.key(seed)
    ks = iter(list(jax.random.split(key, 64)))
    D = D_MODEL

    def nrm(shape, scale):
        return jax.random.normal(next(ks), shape, jnp.float32) * scale

    def gain(shape):
        return 1.0 + nrm(shape, 0.02)

    inp = {}
    inp['x'] = nrm((BATCH, SEQ, D), 1.0)
    inp['c'] = nrm((BATCH, D), 1.0)
    inp['ctx'] = nrm((BATCH, CTX_LEN, D), 1.0)
    inp['c_ctx'] = nrm((D,), 1.0)
    inp['ada_w'] = nrm((DEPTH, D, 6 * D), 0.5 * D ** -0.5)
    inp['ada_b'] = nrm((DEPTH, 6 * D), 0.02)
    inp['norm_g'] = gain((DEPTH, 2, D))
    inp['ssd_w_in'] = nrm((N_SSD, D, SSD_IN_DIM), D ** -0.5)
    inp['ssd_conv_w'] = nrm((N_SSD, CONV_W, SSD_CONV_DIM), CONV_W ** -0.5)
    inp['ssd_conv_b'] = nrm((N_SSD, SSD_CONV_DIM), 0.02)
    dt0 = jnp.exp(jax.random.uniform(next(ks), (N_SSD, 2, SSD_HEADS), jnp.float32,
                                     minval=math.log(1e-3), maxval=math.log(1e-1)))
    inp['ssd_dt_bias'] = dt0 + jnp.log(-jnp.expm1(-dt0))
    inp['ssd_a_log'] = jnp.log(jax.random.uniform(next(ks), (N_SSD, 2, SSD_HEADS), jnp.float32, minval=1.0, maxval=16.0))
    inp['ssd_d'] = gain((N_SSD, SSD_HEADS))
    inp['ssd_norm_g'] = gain((N_SSD, SSD_INNER))
    inp['ssd_w_out'] = nrm((N_SSD, SSD_INNER, D), SSD_INNER ** -0.5)
    inp['hgrn_w_in'] = nrm((N_HGRN, D, 5 * D), D ** -0.5)
    inp['hgrn_lb'] = nrm((2, DEPTH, D), 0.5)
    inp['hgrn_norm_g'] = gain((N_HGRN, D))
    inp['hgrn_w_out'] = nrm((N_HGRN, D, D), D ** -0.5)
    inp['attn_w_qkv'] = nrm((N_ATTN, D, (N_HEADS + 2 * N_KV) * HEAD_DIM), D ** -0.5)
    inp['attn_q_g'] = gain((N_ATTN, HEAD_DIM))
    inp['attn_k_g'] = gain((N_ATTN, HEAD_DIM))
    inp['attn_w_o'] = nrm((N_ATTN, N_HEADS * HEAD_DIM, D), D ** -0.5)
    inp['mlstm_w_up'] = nrm((N_MLSTM, D, 2 * MLSTM_INNER), D ** -0.5)
    inp['mlstm_conv_w'] = nrm((N_MLSTM, CONV_W, MLSTM_INNER), CONV_W ** -0.5)
    inp['mlstm_conv_b'] = nrm((N_MLSTM, MLSTM_INNER), 0.02)
    inp['mlstm_w_q'] = nrm((N_MLSTM, MLSTM_HEADS, MLSTM_DH, MLSTM_DH), MLSTM_DH ** -0.5)
    inp['mlstm_w_k'] = nrm((N_MLSTM, MLSTM_HEADS, MLSTM_DH, MLSTM_DH), MLSTM_DH ** -0.5)
    inp['mlstm_w_v'] = nrm((N_MLSTM, MLSTM_HEADS, MLSTM_DH, MLSTM_DH), MLSTM_DH ** -0.5)
    inp['mlstm_w_gate'] = nrm((N_MLSTM, 3 * MLSTM_INNER, 4 * MLSTM_HEADS), 0.1 * (3 * MLSTM_INNER) ** -0.5)
    f_bias = jnp.linspace(3.0, 6.0, MLSTM_HEADS, dtype=jnp.float32)
    inp['mlstm_b_gate'] = jnp.concatenate([nrm((N_MLSTM, MLSTM_HEADS), 0.1), f_bias + nrm((N_MLSTM, MLSTM_HEADS), 0.1),
                                           nrm((N_MLSTM, MLSTM_HEADS), 0.1), f_bias + nrm((N_MLSTM, MLSTM_HEADS), 0.1)], axis=-1)
    inp['mlstm_skip'] = gain((N_MLSTM, MLSTM_INNER))
    inp['mlstm_norm_g'] = gain((N_MLSTM, MLSTM_INNER))
    inp['mlstm_w_down'] = nrm((N_MLSTM, MLSTM_INNER, D), MLSTM_INNER ** -0.5)
    inp['ffn_w1'] = nrm((N_DENSE, D, D_FF), D ** -0.5)
    inp['ffn_w3'] = nrm((N_DENSE, D, D_FF), D ** -0.5)
    inp['ffn_w2'] = nrm((N_DENSE, D_FF, D), D_FF ** -0.5)
    inp['moe_router'] = nrm((N_MOE, D, N_EXPERTS), D ** -0.5)
    inp['moe_w1'] = nrm((N_MOE, N_EXPERTS, D, D_FF), D ** -0.5)
    inp['moe_w3'] = nrm((N_MOE, N_EXPERTS, D, D_FF), D ** -0.5)
    inp['moe_w2'] = nrm((N_MOE, N_EXPERTS, D_FF, D), D_FF ** -0.5)
    return inp


def reference(x, c, ctx, c_ctx, ada_w, ada_b, norm_g,
              ssd_w_in, ssd_conv_w, ssd_conv_b, ssd_dt_bias, ssd_a_log, ssd_d, ssd_norm_g, ssd_w_out,
              hgrn_w_in, hgrn_lb, hgrn_norm_g, hgrn_w_out,
              attn_w_qkv, attn_q_g, attn_k_g, attn_w_o,
              mlstm_w_up, mlstm_conv_w, mlstm_conv_b, mlstm_w_q, mlstm_w_k, mlstm_w_v, mlstm_w_gate, mlstm_b_gate,
              mlstm_skip, mlstm_norm_g, mlstm_w_down,
              ffn_w1, ffn_w3, ffn_w2,
              moe_router, moe_w1, moe_w3, moe_w2):
    bsz, seq_len = x.shape[0], x.shape[1]
    cos, sin = axial_rope(seq_len)
    lb_all = jnp.cumsum(jax.nn.softmax(hgrn_lb.astype(jnp.float32), axis=1), axis=1)
    lb_all = lb_all - lb_all[:, :1]
    for i in range(DEPTH):
        want_ctx = i < DEPTH - 1
        mod_x = (jax.nn.silu(c) @ ada_w[i] + ada_b[i]).reshape(bsz, 6, 1, D_MODEL)
        mod_c = (jax.nn.silu(c_ctx) @ ada_w[i] + ada_b[i]).reshape(6, 1, D_MODEL)
        hx = modulate(rms_norm(x, norm_g[i, 0]), mod_x[:, 0], mod_x[:, 1])
        hc = modulate(rms_norm(ctx, norm_g[i, 0]), mod_c[0], mod_c[1])
        kind, j = i % N_MIXERS, i // N_MIXERS
        if kind == 0:
            yx, yc = ssd_mixer(hx, hc, ssd_w_in[j], ssd_conv_w[j], ssd_conv_b[j], ssd_dt_bias[j], ssd_a_log[j],
                               ssd_d[j], ssd_norm_g[j], ssd_w_out[j], want_ctx)
        elif kind == 1:
            yx, yc = hgrn2_mixer(hx, hc, hgrn_w_in[j], lb_all[:, i], hgrn_norm_g[j], hgrn_w_out[j], want_ctx)
        elif kind == 2:
            yx, yc = attn_mixer(hx, hc, attn_w_qkv[j], attn_q_g[j], attn_k_g[j], attn_w_o[j], cos, sin, want_ctx)
        else:
            yx, yc = mlstm_mixer(hx, hc, mlstm_w_up[j], mlstm_conv_w[j], mlstm_conv_b[j], mlstm_w_q[j], mlstm_w_k[j],
                                 mlstm_w_v[j], mlstm_w_gate[j], mlstm_b_gate[j], mlstm_skip[j], mlstm_norm_g[j],
                                 mlstm_w_down[j], want_ctx)
        x = x + mod_x[:, 2] * yx
        hx = modulate(rms_norm(x, norm_g[i, 1]), mod_x[:, 3], mod_x[:, 4])
        if want_ctx:
            ctx = ctx + mod_c[2] * yc
            hc = modulate(rms_norm(ctx, norm_g[i, 1]), mod_c[3], mod_c[4])
        if i % 2 == 0:
            ffn = functools.partial(swiglu, w1=ffn_w1[i // 2], w3=ffn_w3[i // 2], w2=ffn_w2[i // 2])
        else:
            ffn = functools.partial(moe_swiglu, w_router=moe_router[i // 2], w1=moe_w1[i // 2],
                                    w3=moe_w3[i // 2], w2=moe_w2[i // 2])
        x = x + mod_x[:, 5] * ffn(hx)
        if want_ctx:
            ctx = ctx + mod_c[5] * ffn(hc)
    return x
```

```python
import functools
import math

import jax
import jax.numpy as jnp
import numpy as np
from jax import lax
from jax.experimental import pallas as pl
from jax.experimental.pallas import tpu as pltpu

F32 = jnp.float32
BF16 = jnp.bfloat16
HI = lax.Precision.HIGHEST

D_MODEL = 1024
EPS = 1e-6
ROW_GROUP = 256
CONV_W = 5
NEG_BIG = -1e30
VMEM_LIMIT = 56 << 20

SSD_INNER = 2 * D_MODEL
SSD_P = 64
SSD_HEADS = SSD_INNER // SSD_P
SSD_N = 128
SSD_GROUPS = 8
SSD_GW = SSD_INNER // SSD_GROUPS
SCAN_L = 128

HG_HEADS = 8
HG_DK = 128
HG_SUB = 16

ATT_HEADS = 16
ATT_KV = 4
ATT_HD = 64
ROPE_THETA = 10000.0
ROPE_FREQS = ATT_HD // 4
ATT_TQ = 128

ML_INNER = 2 * D_MODEL
ML_HEADS = 4
ML_DH = ML_INNER // ML_HEADS

D_FF = 7 * D_MODEL // 2
N_EXPERTS = 8
MOE_TM = 512


def _cparams(sem):
    return pltpu.CompilerParams(dimension_semantics=sem, vmem_limit_bytes=VMEM_LIMIT)


def _dot(a, b):
    return jnp.dot(a, b, preferred_element_type=F32)


def _dot_hi(a, b):
    return jnp.dot(a, b, preferred_element_type=F32, precision=HI)


def _dot_nt(a, b):
    return lax.dot_general(a, b, (((1,), (1,)), ((), ())), preferred_element_type=F32)


def _dot_tn(a, b):
    return lax.dot_general(a, b, (((0,), (0,)), ((), ())), preferred_element_type=F32)


def _silu(x):
    return x * jax.nn.sigmoid(x)


def _log_sigmoid(x):
    return jnp.minimum(x, 0.0) - jnp.log1p(jnp.exp(-jnp.abs(x)))


def _softplus(x):
    return jnp.maximum(x, 0.0) + jnp.log1p(jnp.exp(-jnp.abs(x)))


def _pick_tile(m, pref):
    t = pref
    while m % t:
        t //= 2
    return t


def _rows_scale_shift(y, mod_ref, scale_idx, shift_idx):
    parts = []
    for gi in range(y.shape[0] // ROW_GROUP):
        sl = y[gi * ROW_GROUP:(gi + 1) * ROW_GROUP]
        parts.append(sl * (1.0 + mod_ref[gi, scale_idx:scale_idx + 1, :]) + mod_ref[gi, shift_idx:shift_idx + 1, :])
    return parts[0] if len(parts) == 1 else jnp.concatenate(parts, axis=0)


def _rows_gate_residual(x, acc, mod_ref, gate_idx):
    parts = []
    for gi in range(x.shape[0] // ROW_GROUP):
        sl = slice(gi * ROW_GROUP, (gi + 1) * ROW_GROUP)
        parts.append(x[sl] + mod_ref[gi, gate_idx:gate_idx + 1, :] * acc[sl])
    return parts[0] if len(parts) == 1 else jnp.concatenate(parts, axis=0)


def _norm_mod(x, g, mod_ref, shift_idx, scale_idx):
    y = x * lax.rsqrt(jnp.mean(x * x, axis=-1, keepdims=True) + EPS) * g
    return _rows_scale_shift(y, mod_ref, scale_idx, shift_idx)


def _flip_matrix(n, flip):
    ii = lax.broadcasted_iota(jnp.int32, (n, n), 0)
    jj = lax.broadcasted_iota(jnp.int32, (n, n), 1)
    tgt = jnp.where(flip == 1, n - 1 - ii, ii)
    return jnp.where(jj == tgt, 1.0, 0.0).astype(F32)


def _ada_kernel(c_ref, w_ref, b_ref, o_ref):
    c = c_ref[...]
    o_ref[...] = _dot_hi(_silu(c), w_ref[...]) + b_ref[...]


def ada_modulation(c_pad, w, b):
    n = w.shape[1]
    tn = 1024
    return pl.pallas_call(
        _ada_kernel,
        grid=(n // tn,),
        in_specs=[pl.BlockSpec(c_pad.shape, lambda j: (0, 0)),
                  pl.BlockSpec((D_MODEL, tn), lambda j: (0, j)),
                  pl.BlockSpec((1, tn), lambda j: (0, j))],
        out_specs=pl.BlockSpec((c_pad.shape[0], tn), lambda j: (0, j)),
        out_shape=jax.ShapeDtypeStruct((c_pad.shape[0], n), F32),
        compiler_params=_cparams(("arbitrary",)),
        name="ada_modulation",
    )(c_pad, w, b.reshape(1, n))


def _inproj_kernel(x_ref, g_ref, mod_ref, *refs, n_w, shift_idx, scale_idx, tn):
    h = _norm_mod(x_ref[...], g_ref[...], mod_ref, shift_idx, scale_idx).astype(BF16)
    for w_ref, o_ref in zip(refs[:n_w], refs[n_w:]):
        n = o_ref.shape[1]
        step = min(tn, n)
        for j in range(n // step):
            o_ref[:, j * step:(j + 1) * step] = _dot(h, w_ref[:, j * step:(j + 1) * step]).astype(o_ref.dtype)


def in_projection(x, g, rowmod, ws, out_dtypes, *, shift_idx, scale_idx, tm=256, tn=512):
    m = x.shape[0]
    tm = _pick_tile(m, tm)
    gm = tm // ROW_GROUP
    in_specs = [pl.BlockSpec((tm, D_MODEL), lambda i: (i, 0)),
                pl.BlockSpec((1, D_MODEL), lambda i: (0, 0)),
                pl.BlockSpec((gm, 6, D_MODEL), lambda i: (i, 0, 0))]
    in_specs += [pl.BlockSpec(w.shape, lambda i: (0, 0)) for w in ws]
    out_specs = [pl.BlockSpec((tm, w.shape[1]), lambda i: (i, 0)) for w in ws]
    out_shape = [jax.ShapeDtypeStruct((m, w.shape[1]), dt) for w, dt in zip(ws, out_dtypes)]
    return pl.pallas_call(
        functools.partial(_inproj_kernel, n_w=len(ws), shift_idx=shift_idx, scale_idx=scale_idx, tn=tn),
        grid=(m // tm,),
        in_specs=in_specs, out_specs=out_specs, out_shape=out_shape,
        compiler_params=_cparams(("parallel",)),
        name="in_projection",
    )(x, g.reshape(1, D_MODEL), rowmod, *ws)


def _outproj_kernel(*refs, n_pro, pro_fn, gate_idx):
    pro_refs = refs[:n_pro]
    w_ref, x_ref, mod_ref, o_ref = refs[n_pro:]
    a = pro_fn(*pro_refs).astype(BF16)
    acc = _dot(a, w_ref[...])
    o_ref[...] = _rows_gate_residual(x_ref[...], acc, mod_ref, gate_idx)


def out_projection(pro_fn, pro_args, pro_specs, w, x, rowmod, *, gate_idx, tm):
    m = x.shape[0]
    gm = tm // ROW_GROUP
    in_specs = list(pro_specs) + [pl.BlockSpec(w.shape, lambda i: (0, 0)),
                                  pl.BlockSpec((tm, D_MODEL), lambda i: (i, 0)),
                                  pl.BlockSpec((gm, 6, D_MODEL), lambda i: (i, 0, 0))]
    return pl.pallas_call(
        functools.partial(_outproj_kernel, n_pro=len(pro_args), pro_fn=pro_fn, gate_idx=gate_idx),
        grid=(m // tm,),
        in_specs=in_specs,
        out_specs=pl.BlockSpec((tm, D_MODEL), lambda i: (i, 0)),
        out_shape=jax.ShapeDtypeStruct((m, D_MODEL), F32),
        compiler_params=_cparams(("parallel",)),
        name="out_projection",
    )(*pro_args, w, x, rowmod)


def _conv_kernel(u_ref, w_ref, b_ref, o_ref, *, ctx_len):
    x = u_ref[...].astype(F32)
    t_all = x.shape[0]
    t = lax.broadcasted_iota(jnp.int32, (t_all, 1), 0)
    in_ctx = t < ctx_len
    pad = CONV_W // 2
    acc = b_ref[...] + w_ref[pad:pad + 1, :] * x
    for off in range(-pad, pad + 1):
        if off == 0:
            continue
        xs = pltpu.roll(x, (-off) % t_all, 0)
        tt = t + off
        valid = (tt >= 0) & (tt < t_all) & ((tt < ctx_len) == in_ctx)
        acc = acc + w_ref[pad + off:pad + off + 1, :] * jnp.where(valid, xs, 0.0)
    o_ref[...] = _silu(acc).astype(o_ref.dtype)


def dwconv_silu(u, col_off, width, w, b, *, bsz, ctx_len, tc=256):
    m = u.shape[0]
    t_all = m // bsz
    cb = col_off // tc
    return pl.pallas_call(
        functools.partial(_conv_kernel, ctx_len=ctx_len),
        grid=(bsz, width // tc),
        in_specs=[pl.BlockSpec((t_all, tc), lambda bi, j: (bi, j + cb)),
                  pl.BlockSpec((CONV_W, tc), lambda bi, j: (0, j)),
                  pl.BlockSpec((1, tc), lambda bi, j: (0, j))],
        out_specs=pl.BlockSpec((t_all, tc), lambda bi, j: (bi, j)),
        out_shape=jax.ShapeDtypeStruct((m, width), BF16),
        compiler_params=_cparams(("parallel", "parallel")),
        name="dwconv_silu",
    )(u, w, b.reshape(1, width))


def _scan_chunk_index(d, c, n_ctx_chunks, n_chunks):
    bwd = jnp.where(c < n_ctx_chunks, n_ctx_chunks - 1 - c, n_chunks - 1 + n_ctx_chunks - c)
    return jnp.where(d == 0, c, bwd)


def _ssd_scan_kernel(x_ref, b_ref, c_ref, dt_ref, dtt_ref, bias_ref, biast_ref, alog_ref, alogt_ref, y_ref, state):
    d = pl.program_id(0)
    ci = pl.program_id(2)
    L = SCAN_L
    H = SSD_HEADS

    @pl.when(ci == 0)
    def _():
        state[...] = jnp.zeros_like(state)

    pm = _flip_matrix(L, d)
    pmb = pm.astype(BF16)
    dt = _softplus(_dot_hi(pm, dt_ref[...]) + bias_ref[...])
    dtt = _softplus(_dot_hi(dtt_ref[...], pm) + biast_ref[...])
    a_neg = -jnp.exp(alog_ref[...])
    a_negt = -jnp.exp(alogt_ref[...])
    ii = lax.broadcasted_iota(jnp.int32, (L, L), 0)
    jj = lax.broadcasted_iota(jnp.int32, (L, L), 1)
    causal = ii >= jj
    tri = jnp.where(causal, 1.0, 0.0).astype(F32)
    trit = jnp.where(ii <= jj, 1.0, 0.0).astype(F32)
    dta = dt * a_neg
    a_col = _dot_hi(tri, dta)
    a_row = _dot_hi(dtt * a_negt, trit)
    a_end = jnp.sum(dta, axis=0, keepdims=True)
    hh = lax.broadcasted_iota(jnp.int32, (H, SSD_INNER), 0)
    cc = lax.broadcasted_iota(jnp.int32, (H, SSD_INNER), 1)
    expand = jnp.where(cc // SSD_P == hh, 1.0, 0.0).astype(F32)
    ea_x = _dot_hi(jnp.exp(a_col), expand)
    w_x = _dot_hi(jnp.exp(a_end - a_col) * dt, expand)
    eend_x = _dot_hi(jnp.exp(a_end), expand)
    lane = lax.broadcasted_iota(jnp.int32, (L, 2 * SSD_P), 1)
    for g in range(SSD_GROUPS):
        gs = slice(g * SSD_GW, (g + 1) * SSD_GW)
        ns = slice(g * SSD_N, (g + 1) * SSD_N)
        bg = _dot(pmb, b_ref[:, ns]).astype(BF16)
        cg = _dot(pmb, c_ref[:, ns]).astype(BF16)
        xg = _dot(pmb, x_ref[:, gs])
        xgb = xg.astype(BF16)
        cb = _dot_nt(cg, bg)
        sg = state[:, gs]
        y_inter = _dot(cg, sg.astype(BF16)) * ea_x[:, gs]
        pieces = []
        for pr in range(SSD_GW // (2 * SSD_P)):
            ys = []
            for q in range(2):
                h = g * (SSD_GW // SSD_P) + 2 * pr + q
                rel = a_col[:, h:h + 1] - a_row[h:h + 1, :]
                gm = cb * jnp.exp(jnp.where(causal, rel, NEG_BIG)) * dtt[h:h + 1, :]
                ys.append(_dot(gm.astype(BF16), xgb[:, pr * 2 * SSD_P:(pr + 1) * 2 * SSD_P]))
            pieces.append(jnp.where(lane < SSD_P, ys[0], ys[1]))
        y = jnp.concatenate(pieces, axis=1) + y_inter
        y_ref[:, gs] = _dot(pmb, y.astype(BF16)).astype(y_ref.dtype)
        xw = (xg * w_x[:, gs]).astype(BF16)
        state[:, gs] = sg * eend_x[:, gs] + _dot_tn(bg, xw)


def ssd_scan(u, dt_dir, dtt_dir, dt_bias, a_log, *, bsz, ctx_len):
    m = u.shape[0]
    t_all = m // bsz
    nc = t_all // SCAN_L
    ncc = ctx_len // SCAN_L
    L, H = SCAN_L, SSD_HEADS

    def row(d, bi, c):
        return bi * nc + _scan_chunk_index(d, c, ncc, nc)

    bn = SSD_INNER // (SSD_GROUPS * SSD_N)
    return pl.pallas_call(
        _ssd_scan_kernel,
        grid=(2, bsz, nc),
        in_specs=[pl.BlockSpec((L, SSD_INNER), lambda d, bi, c: (row(d, bi, c), 0)),
                  pl.BlockSpec((L, SSD_GROUPS * SSD_N), lambda d, bi, c: (row(d, bi, c), bn)),
                  pl.BlockSpec((L, SSD_GROUPS * SSD_N), lambda d, bi, c: (row(d, bi, c), bn + 1)),
                  pl.BlockSpec((None, L, H), lambda d, bi, c: (d, row(d, bi, c), 0)),
                  pl.BlockSpec((None, H, L), lambda d, bi, c: (d, 0, row(d, bi, c))),
                  pl.BlockSpec((None, 1, H), lambda d, bi, c: (d, 0, 0)),
                  pl.BlockSpec((None, H, 1), lambda d, bi, c: (d, 0, 0)),
                  pl.BlockSpec((None, 1, H), lambda d, bi, c: (d, 0, 0)),
                  pl.BlockSpec((None, H, 1), lambda d, bi, c: (d, 0, 0))],
        out_specs=pl.BlockSpec((None, L, SSD_INNER), lambda d, bi, c: (d, row(d, bi, c), 0)),
        out_shape=jax.ShapeDtypeStruct((2, m, SSD_INNER), BF16),
        scratch_shapes=[pltpu.VMEM((SSD_N, SSD_INNER), F32)],
        compiler_params=_cparams(("arbitrary", "arbitrary", "arbitrary")),
        name="ssd_scan",
    )(u, u, u, dt_dir, dtt_dir, dt_bias.reshape(2, 1, H), dt_bias.reshape(2, H, 1),
      a_log.reshape(2, 1, H), a_log.reshape(2, H, 1))


def _ssd_finish(y_ref, xs_ref, z_ref, d_ref, g_ref):
    y = y_ref[0].astype(F32) + y_ref[1].astype(F32) + d_ref[...] * xs_ref[...].astype(F32)
    y = y * _silu(z_ref[...].astype(F32))
    parts = []
    for g in range(SSD_GROUPS):
        sl = y[:, g * SSD_GW:(g + 1) * SSD_GW]
        parts.append(sl * lax.rsqrt(jnp.mean(sl * sl, axis=-1, keepdims=True) + EPS))
    return jnp.concatenate(parts, axis=1) * g_ref[...]


def _gla_scan_kernel(q_ref, f_ref, v_ref, llb_ref, l1m_ref, omlb_ref, o_ref, state):
    d = pl.program_id(0)
    ci = pl.program_id(2)
    L, C = SCAN_L, HG_SUB
    dk = HG_DK

    @pl.when(ci == 0)
    def _():
        state[...] = jnp.zeros_like(state)

    pmb = _flip_matrix(L, d).astype(BF16)
    ii = lax.broadcasted_iota(jnp.int32, (L, L), 0)
    jj = lax.broadcasted_iota(jnp.int32, (L, L), 1)
    tri = jnp.where(ii >= jj, 1.0, 0.0).astype(F32)
    same_sub = (ii // C) == (jj // C)
    later_sub = (ii // C) > (jj // C)
    col_sub = jj // C
    ones_b = jnp.ones((dk, L), BF16)

    def head(h, carry):
        ls = pl.ds(pl.multiple_of(h * dk, dk), dk)
        q = _dot(pmb, q_ref[:, ls])
        f = _dot(pmb, f_ref[:, ls])
        v = _dot(pmb, v_ref[:, ls]).astype(BF16)
        q = _silu(q) * (dk ** -0.5)
        a = llb_ref[:, ls]
        b = l1m_ref[:, ls] + _log_sigmoid(f)
        lf = jnp.maximum(a, b) + jnp.log1p(jnp.exp(-jnp.abs(a - b)))
        k = omlb_ref[:, ls] * jax.nn.sigmoid(-f)
        cum = _dot_hi(tri, lf)
        c_end = cum[L - 1:L, :]
        att = jnp.zeros((L, L), F32)
        for dl in range(C):
            if dl == 0:
                e = q * k
            else:
                ks = pltpu.roll(k, dl, 0)
                cs = pltpu.roll(cum, dl, 0)
                e = q * ks * jnp.exp(jnp.minimum(cum - cs, 0.0))
            rep = _dot(e.astype(BF16), ones_b)
            att = att + jnp.where(same_sub & (jj == ii - dl), rep, 0.0)
        kt_parts = []
        for j in range(L // C):
            rs = slice(j * C, (j + 1) * C)
            e_j = cum[(j + 1) * C - 1:(j + 1) * C, :]
            kt_parts.append(k[rs] * jnp.exp(e_j - cum[rs]))
        kt = jnp.concatenate(kt_parts, axis=0).astype(BF16)
        for j in range(L // C - 1):
            e_j = cum[(j + 1) * C - 1:(j + 1) * C, :]
            qt = (q * jnp.exp(jnp.minimum(cum - e_j, 0.0))).astype(BF16)
            att = att + jnp.where(later_sub & (col_sub == j), _dot_nt(qt, kt), 0.0)
        st = state[h]
        o = _dot(att.astype(BF16), v) + _dot_nt((q * jnp.exp(cum)).astype(BF16), st.astype(BF16))
        o_ref[:, ls] = _dot(pmb, o.astype(BF16)).astype(o_ref.dtype)
        kw = (k * jnp.exp(c_end - cum)).astype(BF16)
        state[h] = st * jnp.exp(c_end) + _dot_tn(v, kw)
        return carry

    lax.fori_loop(0, HG_HEADS, head, 0)


def gla_scan(proj, llb, l1m, omlb, *, bsz, ctx_len):
    m = proj.shape[0]
    t_all = m // bsz
    nc = t_all // SCAN_L
    ncc = ctx_len // SCAN_L
    L = SCAN_L

    def row(d, bi, c):
        return bi * nc + _scan_chunk_index(d, c, ncc, nc)

    vec = pl.BlockSpec((None, 1, D_MODEL), lambda d, bi, c: (d, 0, 0))
    return pl.pallas_call(
        _gla_scan_kernel,
        grid=(2, bsz, nc),
        in_specs=[pl.BlockSpec((L, D_MODEL), lambda d, bi, c: (row(d, bi, c), 0)),
                  pl.BlockSpec((L, D_MODEL), lambda d, bi, c: (row(d, bi, c), 1 + d)),
                  pl.BlockSpec((L, D_MODEL), lambda d, bi, c: (row(d, bi, c), 3)),
                  vec, vec, vec],
        out_specs=pl.BlockSpec((None, L, D_MODEL), lambda d, bi, c: (d, row(d, bi, c), 0)),
        out_shape=jax.ShapeDtypeStruct((2, m, D_MODEL), BF16),
        scratch_shapes=[pltpu.VMEM((HG_HEADS, HG_DK, HG_DK), F32)],
        compiler_params=_cparams(("arbitrary", "arbitrary", "arbitrary")),
        name="gla_scan",
    )(proj, proj, proj, llb, l1m, omlb)


def _hgrn_finish(o_ref, gate_ref, g_ref):
    o = o_ref[0].astype(F32) + o_ref[1].astype(F32)
    parts = []
    for h in range(HG_HEADS):
        sl = o[:, h * HG_DK:(h + 1) * HG_DK]
        parts.append(sl * lax.rsqrt(jnp.mean(sl * sl, axis=-1, keepdims=True) + EPS))
    return jnp.concatenate(parts, axis=1) * g_ref[...] * _silu(gate_ref[...].astype(F32))


def _attn_qkv_kernel(x_ref, g_ref, mod_ref, w_ref, cos_ref, sin_ref, qg_ref, kg_ref, bd_ref, rot_ref,
                     q_ref, k_ref, v_ref, *, shift_idx, scale_idx):
    h = _norm_mod(x_ref[...], g_ref[...], mod_ref, shift_idx, scale_idx).astype(BF16)
    nq = ATT_HEADS * ATT_HD
    nk = ATT_KV * ATT_HD

    def norm_rope(a, gain, width, out_scale):
        ss = _dot((a * a).astype(BF16), bd_ref[:width, :width])
        an = a * lax.rsqrt(ss * (1.0 / ATT_HD) + EPS) * gain
        rot = _dot(an.astype(BF16), rot_ref[:width, :width])
        cos = jnp.concatenate([cos_ref[...]] * (width // 128), axis=1)
        sin = jnp.concatenate([sin_ref[...]] * (width // 128), axis=1)
        return (an * cos + rot * sin) * out_scale

    q = _dot(h, w_ref[:, :nq])
    q_ref[...] = norm_rope(q, qg_ref[...], nq, ATT_HD ** -0.5).astype(q_ref.dtype)
    k = _dot(h, w_ref[:, nq:nq + nk])
    k_ref[...] = norm_rope(k, kg_ref[...], nk, 1.0).astype(k_ref.dtype)
    v_ref[...] = _dot(h, w_ref[:, nq + nk:]).astype(v_ref.dtype)


def attn_qkv(x, g, rowmod, w, cos_t, sin_t, q_g, k_g, bd, rot, *, bsz, shift_idx, scale_idx):
    m = x.shape[0]
    tm = ROW_GROUP
    tpb = (m // bsz) // tm
    nq = ATT_HEADS * ATT_HD
    nk = ATT_KV * ATT_HD
    const = lambda shape: pl.BlockSpec(shape, lambda i: (0, 0))
    return pl.pallas_call(
        functools.partial(_attn_qkv_kernel, shift_idx=shift_idx, scale_idx=scale_idx),
        grid=(m // tm,),
        in_specs=[pl.BlockSpec((tm, D_MODEL), lambda i: (i, 0)),
                  const((1, D_MODEL)),
                  pl.BlockSpec((1, 6, D_MODEL), lambda i: (i, 0, 0)),
                  const(w.shape),
                  pl.BlockSpec((tm, 128), lambda i: (i % tpb, 0)),
                  pl.BlockSpec((tm, 128), lambda i: (i % tpb, 0)),
                  const((1, nq)), const((1, nk)), const(bd.shape), const(rot.shape)],
        out_specs=[pl.BlockSpec((tm, nq), lambda i: (i, 0)),
                   pl.BlockSpec((tm, nk), lambda i: (i, 0)),
                   pl.BlockSpec((tm, nk), lambda i: (i, 0))],
        out_shape=[jax.ShapeDtypeStruct((m, nq), BF16),
                   jax.ShapeDtypeStruct((m, nk), BF16),
                   jax.ShapeDtypeStruct((m, nk), BF16)],
        compiler_params=_cparams(("parallel",)),
        name="attn_qkv",
    )(x, g.reshape(1, D_MODEL), rowmod, w, cos_t, sin_t, q_g, k_g, bd, rot)


def _attn_kernel(q_ref, k_ref, v_ref, o_ref, *, ctx_len):
    qt = pl.program_id(2)
    tq = q_ref.shape[0]
    lane = lax.broadcasted_iota(jnp.int32, (1, 2 * ATT_HD), 1)
    n_r = q_ref.shape[1] // (2 * ATT_HD)

    def attend(k, v):
        outs = []
        for j in range(2):
            sel = (lane // ATT_HD) == j
            qs = jnp.concatenate(
                [jnp.where(sel, q_ref[:, r * 128:(r + 1) * 128], jnp.zeros((), BF16)) for r in range(n_r)], axis=0)
            s = _dot_nt(qs, k)
            mx = jnp.max(s, axis=-1, keepdims=True)
            p = jnp.exp(s - mx)
            den = jnp.sum(p, axis=-1, keepdims=True)
            outs.append(_dot(p.astype(BF16), v) / den)
        o = jnp.where(lane < ATT_HD, outs[0], outs[1])
        for r in range(n_r):
            o_ref[:, r * 128:(r + 1) * 128] = o[r * tq:(r + 1) * tq].astype(o_ref.dtype)

    n_ctx_tiles = ctx_len // tq

    @pl.when(qt < n_ctx_tiles)
    def _():
        attend(k_ref[:ctx_len, :], v_ref[:ctx_len, :])

    @pl.when(qt >= n_ctx_tiles)
    def _():
        attend(k_ref[...], v_ref[...])


def attention(q, k, v, *, bsz, ctx_len):
    m = q.shape[0]
    t_all = m // bsz
    tq = ATT_TQ
    nqt = t_all // tq
    n_pairs = ATT_KV // 2
    qw = q.shape[1] // n_pairs
    return pl.pallas_call(
        functools.partial(_attn_kernel, ctx_len=ctx_len),
        grid=(bsz, n_pairs, nqt),
        in_specs=[pl.BlockSpec((tq, qw), lambda bi, p, t: (bi * nqt + t, p)),
                  pl.BlockSpec((t_all, 2 * ATT_HD), lambda bi, p, t: (bi, p)),
                  pl.BlockSpec((t_all, 2 * ATT_HD), lambda bi, p, t: (bi, p))],
        out_specs=pl.BlockSpec((tq, qw), lambda bi, p, t: (bi * nqt + t, p)),
        out_shape=jax.ShapeDtypeStruct(q.shape, BF16),
        compiler_params=_cparams(("parallel", "parallel", "arbitrary")),
        name="attention",
    )(q, k, v)


def _identity_pro(a_ref):
    return a_ref[...]


def _mlstm_qkv_kernel(xc_ref, xm_ref, wq_ref, wk_ref, wv_ref, wg_ref, bg_ref, q_ref, k_ref, v_ref, gate_ref):
    gates = jnp.zeros(gate_ref.shape, F32) + bg_ref[...]
    for which, (src, w_ref, o_ref) in enumerate(((xc_ref, wq_ref, q_ref), (xc_ref, wk_ref, k_ref),
                                                 (xm_ref, wv_ref, v_ref))):
        for h in range(ML_HEADS):
            hs = slice(h * ML_DH, (h + 1) * ML_DH)
            r = _dot(src[:, hs], w_ref[h]).astype(BF16)
            o_ref[:, hs] = r
            gates = gates + _dot(r, wg_ref[which * ML_INNER + h * ML_DH:which * ML_INNER + (h + 1) * ML_DH, :])
    gate_ref[...] = gates


def mlstm_qkv(xc, up, wq, wk, wv, wg, bg, *, tm=256):
    m = xc.shape[0]
    ng = wg.shape[1]
    const = lambda shape: pl.BlockSpec(shape, lambda i: (0,) * len(shape))
    row = pl.BlockSpec((tm, ML_INNER), lambda i: (i, 0))
    return pl.pallas_call(
        _mlstm_qkv_kernel,
        grid=(m // tm,),
        in_specs=[row, row, const(wq.shape), const(wk.shape), const(wv.shape), const(wg.shape), const((1, ng))],
        out_specs=[row, row, row, pl.BlockSpec((tm, ng), lambda i: (i, 0))],
        out_shape=[jax.ShapeDtypeStruct((m, ML_INNER), BF16)] * 3 + [jax.ShapeDtypeStruct((m, ng), F32)],
        compiler_params=_cparams(("parallel",)),
        name="mlstm_qkv",
    )(xc, up, wq, wk, wv, wg, bg.reshape(1, ng))


def _mlstm_scan_kernel(q_ref, k_ref, v_ref, gt_ref, gtt_ref, h_ref, c_st, n_st, m_st):
    d = pl.program_id(0)
    ci = pl.program_id(2)
    L = SCAN_L
    nh = ML_HEADS
    scale = ML_DH ** -0.5

    @pl.when(ci == 0)
    def _():
        c_st[...] = jnp.zeros_like(c_st)
        n_st[...] = jnp.zeros_like(n_st)
        m_st[...] = jnp.full(m_st.shape, NEG_BIG, F32)

    pm = _flip_matrix(L, d)
    pmb = pm.astype(BF16)
    ii = lax.broadcasted_iota(jnp.int32, (L, L), 0)
    jj = lax.broadcasted_iota(jnp.int32, (L, L), 1)
    causal = ii >= jj
    tri = jnp.where(causal, 1.0, 0.0).astype(F32)
    trit = jnp.where(ii <= jj, 1.0, 0.0).astype(F32)
    gt = _dot_hi(pm, gt_ref[...])
    gtt = _dot_hi(gtt_ref[...], pm)
    li_c, li_r = gt[:, :nh], gtt[:nh, :]
    lf_c, lf_r = _log_sigmoid(gt[:, nh:]), _log_sigmoid(gtt[nh:, :])
    cum_c = _dot_hi(tri, lf_c)
    cum_r = _dot_hi(lf_r, trit)
    end_c = jnp.sum(lf_c, axis=0, keepdims=True)
    for h in range(nh):
        hs = slice(h * ML_DH, (h + 1) * ML_DH)
        q = _dot(pmb, q_ref[:, hs]).astype(BF16)
        k = _dot(pmb, k_ref[:, hs])
        kb = k.astype(BF16)
        v = _dot(pmb, v_ref[:, hs]).astype(BF16)
        m_prev = m_st[h:h + 1, 0:1]
        cum_end = end_c[:, h:h + 1]
        dmat = jnp.where(causal, cum_c[:, h:h + 1] - cum_r[h:h + 1, :] + li_r[h:h + 1, :], -jnp.inf)
        inter = cum_c[:, h:h + 1] + m_prev
        m_t = jnp.maximum(inter, jnp.max(dmat, axis=1, keepdims=True))
        w = jnp.exp(dmat - m_t)
        w_c = jnp.exp(inter - m_t)
        qk = _dot_nt(q, kb) * scale * w
        cmat = c_st[h]
        num = _dot(qk.astype(BF16), v) + w_c * _dot(q, cmat.astype(BF16))
        qn = jnp.sum(q.astype(F32) * n_st[h:h + 1, :], axis=1, keepdims=True)
        den = jnp.sum(qk, axis=1, keepdims=True) + w_c * qn
        hv = num / jnp.maximum(jnp.abs(den), jnp.exp(-m_t))
        h_ref[:, hs] = _dot(pmb, hv.astype(BF16)).astype(h_ref.dtype)
        wend_c = cum_end - cum_c[:, h:h + 1] + li_c[:, h:h + 1]
        wend_r = cum_end - cum_r[h:h + 1, :] + li_r[h:h + 1, :]
        m_new = jnp.maximum(cum_end + m_prev, jnp.max(wend_r, axis=1, keepdims=True))
        a_old = jnp.exp(cum_end + m_prev - m_new)
        e_c = jnp.exp(wend_c - m_new) * scale
        e_r = jnp.exp(wend_r - m_new) * scale
        c_st[h] = a_old * cmat + _dot_tn((k * e_c).astype(BF16), v)
        e_r8 = jnp.broadcast_to(e_r, (8, L)).astype(BF16)
        n_st[h:h + 1, :] = a_old * n_st[h:h + 1, :] + _dot(e_r8, kb)[0:1, :]
        m_st[h:h + 1, :] = jnp.broadcast_to(m_new, (1, m_st.shape[1]))


def mlstm_scan(q, k, v, gt_dir, gtt_dir, *, bsz, ctx_len):
    m = q.shape[0]
    t_all = m // bsz
    nc = t_all // SCAN_L
    ncc = ctx_len // SCAN_L
    L, nh = SCAN_L, ML_HEADS

    def row(d, bi, c):
        return bi * nc + _scan_chunk_index(d, c, ncc, nc)

    blk = pl.BlockSpec((L, ML_INNER), lambda d, bi, c: (row(d, bi, c), 0))
    return pl.pallas_call(
        _mlstm_scan_kernel,
        grid=(2, bsz, nc),
        in_specs=[blk, blk, blk,
                  pl.BlockSpec((None, L, 2 * nh), lambda d, bi, c: (d, row(d, bi, c), 0)),
                  pl.BlockSpec((None, 2 * nh, L), lambda d, bi, c: (d, 0, row(d, bi, c)))],
        out_specs=pl.BlockSpec((None, L, ML_INNER), lambda d, bi, c: (d, row(d, bi, c), 0)),
        out_shape=jax.ShapeDtypeStruct((2, m, ML_INNER), BF16),
        scratch_shapes=[pltpu.VMEM((nh, ML_DH, ML_DH), F32),
                        pltpu.VMEM((8, ML_DH), F32),
                        pltpu.VMEM((8, 128), F32)],
        compiler_params=_cparams(("arbitrary", "arbitrary", "arbitrary")),
        name="mlstm_scan",
    )(q, k, v, gt_dir, gtt_dir)


def _mlstm_finish(h_ref, xc_ref, z_ref, skip_ref, g_ref):
    hsum = h_ref[0].astype(F32) + h_ref[1].astype(F32)
    parts = []
    for h in range(ML_HEADS):
        sl = hsum[:, h * ML_DH:(h + 1) * ML_DH]
        parts.append(sl * lax.rsqrt(jnp.mean(sl * sl, axis=-1, keepdims=True) + EPS))
    hn = jnp.concatenate(parts, axis=1) * g_ref[...]
    return (hn + skip_ref[...] * xc_ref[...].astype(F32)) * _silu(z_ref[...].astype(F32))


def _ffn_kernel(x_ref, g_ref, mod_ref, w1_ref, w3_ref, w2_ref, o_ref, h_sc, acc_sc, *, shift_idx, scale_idx, gate_idx):
    f = pl.program_id(1)

    @pl.when(f == 0)
    def _():
        h_sc[...] = _norm_mod(x_ref[...], g_ref[...], mod_ref, shift_idx, scale_idx).astype(BF16)
        acc_sc[...] = jnp.zeros_like(acc_sc)

    h = h_sc[...]
    a = _silu(_dot(h, w1_ref[...])) * _dot(h, w3_ref[...])
    acc_sc[...] += _dot(a.astype(BF16), w2_ref[...])

    @pl.when(f == pl.num_programs(1) - 1)
    def _():
        o_ref[...] = _rows_gate_residual(x_ref[...], acc_sc[...], mod_ref, gate_idx)


def dense_ffn(x, g, rowmod, w1, w3, w2, *, tm=1024, tf=512):
    m = x.shape[0]
    tm = _pick_tile(m, tm)
    gm = tm // ROW_GROUP
    nf = D_FF // tf
    return pl.pallas_call(
        functools.partial(_ffn_kernel, shift_idx=3, scale_idx=4, gate_idx=5),
        grid=(m // tm, nf),
        in_specs=[pl.BlockSpec((tm, D_MODEL), lambda i, f: (i, 0)),
                  pl.BlockSpec((1, D_MODEL), lambda i, f: (0, 0)),
                  pl.BlockSpec((gm, 6, D_MODEL), lambda i, f: (i, 0, 0)),
                  pl.BlockSpec((D_MODEL, tf), lambda i, f: (0, f)),
                  pl.BlockSpec((D_MODEL, tf), lambda i, f: (0, f)),
                  pl.BlockSpec((tf, D_MODEL), lambda i, f: (f, 0))],
        out_specs=pl.BlockSpec((tm, D_MODEL), lambda i, f: (i, 0)),
        out_shape=jax.ShapeDtypeStruct((m, D_MODEL), F32),
        scratch_shapes=[pltpu.VMEM((tm, D_MODEL), BF16), pltpu.VMEM((tm, D_MODEL), F32)],
        compiler_params=_cparams(("parallel", "arbitrary")),
        name="dense_ffn",
    )(x, g.reshape(1, D_MODEL), rowmod, w1, w3, w2)


def _router_kernel(x_ref, g_ref, mod_ref, wr_ref, h_ref, route_ref, *, shift_idx, scale_idx):
    h = _norm_mod(x_ref[...], g_ref[...], mod_ref, shift_idx, scale_idx)
    h_ref[...] = h.astype(h_ref.dtype)
    logits = _dot_hi(h, wr_ref[...])
    lane = lax.broadcasted_iota(jnp.int32, logits.shape, 1)
    logits = jnp.where(lane < N_EXPERTS, logits, -jnp.inf)
    m1 = jnp.max(logits, axis=-1, keepdims=True)
    i1 = jnp.min(jnp.where(logits == m1, lane, 128), axis=-1, keepdims=True)
    rest = jnp.where(lane == i1, -jnp.inf, logits)
    m2 = jnp.max(rest, axis=-1, keepdims=True)
    i2 = jnp.min(jnp.where(rest == m2, lane, 128), axis=-1, keepdims=True)
    e2 = jnp.exp(m2 - m1)
    w1 = 1.0 / (1.0 + e2)
    w2 = e2 / (1.0 + e2)
    route = jnp.where(lane == 0, i1.astype(F32),
                      jnp.where(lane == 1, i2.astype(F32), jnp.where(lane == 2, w1, jnp.where(lane == 3, w2, 0.0))))
    route_ref[...] = route


def moe_router(x, g, rowmod, wr_pad, *, tm=256):
    m = x.shape[0]
    return pl.pallas_call(
        functools.partial(_router_kernel, shift_idx=3, scale_idx=4),
        grid=(m // tm,),
        in_specs=[pl.BlockSpec((tm, D_MODEL), lambda i: (i, 0)),
                  pl.BlockSpec((1, D_MODEL), lambda i: (0, 0)),
                  pl.BlockSpec((tm // ROW_GROUP, 6, D_MODEL), lambda i: (i, 0, 0)),
                  pl.BlockSpec(wr_pad.shape, lambda i: (0, 0))],
        out_specs=[pl.BlockSpec((tm, D_MODEL), lambda i: (i, 0)), pl.BlockSpec((tm, 128), lambda i: (i, 0))],
        out_shape=[jax.ShapeDtypeStruct((m, D_MODEL), BF16), jax.ShapeDtypeStruct((m, 128), F32)],
        compiler_params=_cparams(("parallel",)),
        name="moe_router",
    )(x, g.reshape(1, D_MODEL), rowmod, wr_pad)


def _moe_ffn_kernel(te_ref, nt_ref, h_ref, w1_ref, w3_ref, w2_ref, o_ref, acc_sc):
    i = pl.program_id(0)
    f = pl.program_id(1)
    live = i < nt_ref[0]

    @pl.when(f == 0)
    def _():
        acc_sc[...] = jnp.zeros_like(acc_sc)

    @pl.when(live)
    def _():
        h = h_ref[...]
        a = _silu(_dot(h, w1_ref[...])) * _dot(h, w3_ref[...])
        acc_sc[...] += _dot(a.astype(BF16), w2_ref[...])

    @pl.when(f == pl.num_programs(1) - 1)
    def _():
        o_ref[...] = acc_sc[...].astype(o_ref.dtype)


def moe_grouped_ffn(h_sorted, tile_expert, n_tiles, w1, w3, w2, *, tm=MOE_TM, tf=512):
    p = h_sorted.shape[0]
    nf = D_FF // tf
    grid_spec = pltpu.PrefetchScalarGridSpec(
        num_scalar_prefetch=2,
        grid=(p // tm, nf),
        in_specs=[pl.BlockSpec((tm, D_MODEL), lambda i, f, te, nt: (i, 0)),
                  pl.BlockSpec((None, D_MODEL, tf), lambda i, f, te, nt: (te[i], 0, f)),
                  pl.BlockSpec((None, D_MODEL, tf), lambda i, f, te, nt: (te[i], 0, f)),
                  pl.BlockSpec((None, tf, D_MODEL), lambda i, f, te, nt: (te[i], f, 0))],
        out_specs=pl.BlockSpec((tm, D_MODEL), lambda i, f, te, nt: (i, 0)),
        scratch_shapes=[pltpu.VMEM((tm, D_MODEL), F32)])
    return pl.pallas_call(
        _moe_ffn_kernel,
        grid_spec=grid_spec,
        out_shape=jax.ShapeDtypeStruct((p, D_MODEL), BF16),
        compiler_params=_cparams(("arbitrary", "arbitrary")),
        name="moe_grouped_ffn",
    )(tile_expert, n_tiles, h_sorted, w1, w3, w2)


def _moe_combine_kernel(x_ref, ya_ref, yb_ref, route_ref, mod_ref, o_ref, *, gate_idx):
    r = route_ref[...]
    y = r[:, 2:3] * ya_ref[...].astype(F32) + r[:, 3:4] * yb_ref[...].astype(F32)
    o_ref[...] = _rows_gate_residual(x_ref[...], y, mod_ref, gate_idx)


def moe_combine(x, ya, yb, route, rowmod, *, tm=256):
    m = x.shape[0]
    row = pl.BlockSpec((tm, D_MODEL), lambda i: (i, 0))
    return pl.pallas_call(
        functools.partial(_moe_combine_kernel, gate_idx=5),
        grid=(m // tm,),
        in_specs=[row, row, row, pl.BlockSpec((tm, 128), lambda i: (i, 0)),
                  pl.BlockSpec((tm // ROW_GROUP, 6, D_MODEL), lambda i: (i, 0, 0))],
        out_specs=row,
        out_shape=jax.ShapeDtypeStruct((m, D_MODEL), F32),
        compiler_params=_cparams(("parallel",)),
        name="moe_combine",
    )(x, ya, yb, route, rowmod)


def moe_ffn(x, g, rowmod, w_router, w1, w3, w2):
    m = x.shape[0]
    wr_pad = jnp.zeros((D_MODEL, 128), F32).at[:, :N_EXPERTS].set(w_router)
    h, route = moe_router(x, g, rowmod, wr_pad)
    tm = MOE_TM
    n_tiles_max = (2 * m) // tm + N_EXPERTS
    p = n_tiles_max * tm
    expert = route[:, :2].astype(jnp.int32).reshape(-1)
    order = jnp.argsort(expert, stable=True).astype(jnp.int32)
    counts = jnp.sum(expert[:, None] == jnp.arange(N_EXPERTS, dtype=jnp.int32)[None, :], axis=0).astype(jnp.int32)
    tiles_per = (counts + tm - 1) // tm
    tile_end = jnp.cumsum(tiles_per)
    grp_start = (tile_end - tiles_per) * tm
    cnt_start = jnp.cumsum(counts) - counts
    sorted_expert = expert[order]
    pos_sorted = grp_start[sorted_expert] + jnp.arange(2 * m, dtype=jnp.int32) - cnt_start[sorted_expert]
    slot_pos = jnp.zeros((2 * m,), jnp.int32).at[order].set(pos_sorted)
    src_token = jnp.zeros((p,), jnp.int32).at[pos_sorted].set(order // 2)
    tile_expert = jnp.minimum(
        jnp.sum(jnp.arange(n_tiles_max, dtype=jnp.int32)[:, None] >= tile_end[None, :], axis=1), N_EXPERTS - 1
    ).astype(jnp.int32)
    n_tiles = tile_end[-1:].astype(jnp.int32)
    h_sorted = jnp.take(h, src_token, axis=0)
    y_sorted = moe_grouped_ffn(h_sorted, tile_expert, n_tiles, w1, w3, w2)
    pos2 = slot_pos.reshape(m, 2)
    ya = jnp.take(y_sorted, pos2[:, 0], axis=0)
    yb = jnp.take(y_sorted, pos2[:, 1], axis=0)
    return moe_combine(x, ya, yb, route, rowmod)


def _dir_split(a, n):
    m = a.shape[0]
    a3 = a.reshape(m, 2, n)
    return jnp.transpose(a3, (1, 0, 2)), jnp.transpose(a3, (1, 2, 0))


def ssd_layer(x, g, rowmod, w_in, conv_w, conv_b, dt_bias, a_log, d_skip, norm_g, w_out, *, bsz, ctx_len):
    m = x.shape[0]
    n_main = SSD_INNER + SSD_INNER + 2 * SSD_GROUPS * SSD_N
    zx, dt_raw = in_projection(x, g, rowmod, [w_in[:, :n_main].astype(BF16), w_in[:, n_main:].astype(BF16)],
                               [BF16, F32], shift_idx=0, scale_idx=1)
    u = dwconv_silu(zx, SSD_INNER, n_main - SSD_INNER, conv_w, conv_b, bsz=bsz, ctx_len=ctx_len)
    dt_dir, dtt_dir = _dir_split(dt_raw, SSD_HEADS)
    y = ssd_scan(u, dt_dir, dtt_dir, dt_bias, a_log, bsz=bsz, ctx_len=ctx_len)
    tm = 256
    d_x = jnp.repeat(d_skip.astype(F32), SSD_P).reshape(1, SSD_INNER)
    specs = [pl.BlockSpec((2, tm, SSD_INNER), lambda i: (0, i, 0)),
             pl.BlockSpec((tm, SSD_INNER), lambda i: (i, 0)),
             pl.BlockSpec((tm, SSD_INNER), lambda i: (i, 0)),
             pl.BlockSpec((1, SSD_INNER), lambda i: (0, 0)),
             pl.BlockSpec((1, SSD_INNER), lambda i: (0, 0))]
    return out_projection(_ssd_finish, [y, u, zx, d_x, norm_g.reshape(1, SSD_INNER)], specs,
                          w_out.astype(BF16), x, rowmod, gate_idx=2, tm=tm)


def hgrn_layer(x, g, rowmod, w_in, lb, norm_g, w_out, *, bsz, ctx_len):
    proj, = in_projection(x, g, rowmod, [w_in.astype(BF16)], [BF16], shift_idx=0, scale_idx=1)
    lb = lb.astype(F32).reshape(2, 1, D_MODEL)
    o = gla_scan(proj, jnp.log(lb), jnp.log1p(-lb), 1.0 - lb, bsz=bsz, ctx_len=ctx_len)
    tm = 256
    specs = [pl.BlockSpec((2, tm, D_MODEL), lambda i: (0, i, 0)),
             pl.BlockSpec((tm, D_MODEL), lambda i: (i, 4)),
             pl.BlockSpec((1, D_MODEL), lambda i: (0, 0))]
    return out_projection(_hgrn_finish, [o, proj, norm_g.reshape(1, D_MODEL)], specs,
                          w_out.astype(BF16), x, rowmod, gate_idx=2, tm=tm)


def _attn_head_perm():
    r_per = ATT_HEADS // ATT_KV
    heads = [(2 * p + j) * r_per + r for p in range(ATT_KV // 2) for r in range(r_per) for j in range(2)]
    return np.concatenate([np.arange(h * ATT_HD, (h + 1) * ATT_HD) for h in heads])


def _rope_tables(seq_len, ctx_len, grid_w):
    rows = seq_len // grid_w
    row = jnp.repeat(jnp.arange(rows, dtype=F32), grid_w)
    col = jnp.tile(jnp.arange(grid_w, dtype=F32), rows)
    inv = ROPE_THETA ** (-jnp.arange(ROPE_FREQS, dtype=F32) / ROPE_FREQS)
    ang_r = row[:, None] * inv
    ang_c = col[:, None] * inv
    ang = jnp.concatenate([ang_r, ang_r, ang_c, ang_c], axis=-1)
    cos = jnp.concatenate([jnp.ones((ctx_len, ATT_HD), F32), jnp.cos(ang)], axis=0)
    sin = jnp.concatenate([jnp.zeros((ctx_len, ATT_HD), F32), jnp.sin(ang)], axis=0)
    return jnp.tile(cos, (1, 2)), jnp.tile(sin, (1, 2))


def _rope_matrices():
    r64 = np.zeros((ATT_HD, ATT_HD), np.float32)
    fq = ROPE_FREQS
    for ax in range(2):
        o = ax * 2 * fq
        for i in range(fq):
            r64[o + fq + i, o + i] = -1.0
            r64[o + i, o + fq + i] = 1.0
    n = ATT_HEADS
    bd = np.kron(np.eye(n, dtype=np.float32), np.ones((ATT_HD, ATT_HD), np.float32))
    rot = np.kron(np.eye(n, dtype=np.float32), r64)
    return jnp.asarray(bd, BF16), jnp.asarray(rot, BF16)


def attn_layer(x, g, rowmod, w_qkv, q_g, k_g, w_o, *, bsz, ctx_len, grid_w):
    m = x.shape[0]
    seq_len = m // bsz - ctx_len
    perm = _attn_head_perm()
    nq = ATT_HEADS * ATT_HD
    w = jnp.concatenate([w_qkv[:, :nq][:, perm], w_qkv[:, nq:]], axis=1).astype(BF16)
    cos_t, sin_t = _rope_tables(seq_len, ctx_len, grid_w)
    bd, rot = _rope_matrices()
    qg = jnp.tile(q_g.astype(F32), ATT_HEADS).reshape(1, nq)
    kg = jnp.tile(k_g.astype(F32), ATT_KV).reshape(1, ATT_KV * ATT_HD)
    q, k, v = attn_qkv(x, g, rowmod, w, cos_t, sin_t, qg, kg, bd, rot, bsz=bsz, shift_idx=0, scale_idx=1)
    o = attention(q, k, v, bsz=bsz, ctx_len=ctx_len)
    tm = 256
    specs = [pl.BlockSpec((tm, nq), lambda i: (i, 0))]
    return out_projection(_identity_pro, [o], specs, w_o[perm, :].astype(BF16), x, rowmod, gate_idx=2, tm=tm)


def mlstm_layer(x, g, rowmod, w_up, conv_w, conv_b, w_q, w_k, w_v, w_gate, b_gate, skip, norm_g, w_down,
                *, bsz, ctx_len):
    up, = in_projection(x, g, rowmod, [w_up.astype(BF16)], [BF16], shift_idx=0, scale_idx=1)
    xc = dwconv_silu(up, 0, ML_INNER, conv_w, conv_b, bsz=bsz, ctx_len=ctx_len)
    q, k, v, gates = mlstm_qkv(xc, up, w_q.astype(BF16), w_k.astype(BF16), w_v.astype(BF16),
                               w_gate.astype(BF16), b_gate)
    gt_dir, gtt_dir = _dir_split(gates, 2 * ML_HEADS)
    hsc = mlstm_scan(q, k, v, gt_dir, gtt_dir, bsz=bsz, ctx_len=ctx_len)
    tm = 256
    specs = [pl.BlockSpec((2, tm, ML_INNER), lambda i: (0, i, 0)),
             pl.BlockSpec((tm, ML_INNER), lambda i: (i, 0)),
             pl.BlockSpec((tm, ML_INNER), lambda i: (i, 1)),
             pl.BlockSpec((1, ML_INNER), lambda i: (0, 0)),
             pl.BlockSpec((1, ML_INNER), lambda i: (0, 0))]
    return out_projection(_mlstm_finish, [hsc, xc, up, skip.reshape(1, ML_INNER), norm_g.reshape(1, ML_INNER)],
                          specs, w_down.astype(BF16), x, rowmod, gate_idx=2, tm=tm)


def kernel(x, c, ctx, c_ctx, ada_w, ada_b, norm_g, ssd_w_in, ssd_conv_w, ssd_conv_b, ssd_dt_bias, ssd_a_log, ssd_d, ssd_norm_g, ssd_w_out, hgrn_w_in, hgrn_lb, hgrn_norm_g, hgrn_w_out, attn_w_qkv, attn_q_g, attn_k_g, attn_w_o, mlstm_w_up, mlstm_conv_w, mlstm_conv_b, mlstm_w_q, mlstm_w_k, mlstm_w_v, mlstm_w_gate, mlstm_b_gate, mlstm_skip, mlstm_norm_g, mlstm_w_down, ffn_w1, ffn_w3, ffn_w2, moe_router, moe_w1, moe_w3, moe_w2):
    bsz, seq_len, _ = x.shape
    ctx_len = ctx.shape[1]
    depth = ada_w.shape[0]
    grid_w = 64
    t_all = ctx_len + seq_len
    m = bsz * t_all
    xa = jnp.concatenate([ctx, x], axis=1).reshape(m, D_MODEL)
    c_pad = jnp.zeros((8, D_MODEL), F32).at[:bsz].set(c).at[bsz].set(c_ctx)
    groups_per_batch = t_all // ROW_GROUP
    ctx_groups = ctx_len // ROW_GROUP
    gidx = np.array([bsz if (gi % groups_per_batch) < ctx_groups else gi // groups_per_batch
                     for gi in range(m // ROW_GROUP)], np.int32)
    lb_all = jnp.cumsum(jax.nn.softmax(hgrn_lb.astype(F32), axis=1), axis=1)
    lb_all = lb_all - lb_all[:, :1]
    kw = dict(bsz=bsz, ctx_len=ctx_len)
    for i in range(depth):
        mod = ada_modulation(c_pad, ada_w[i], ada_b[i]).reshape(8, 6, D_MODEL)
        rowmod = mod[gidx]
        kind, j = i % 4, i // 4
        if kind == 0:
            xa = ssd_layer(xa, norm_g[i, 0], rowmod, ssd_w_in[j], ssd_conv_w[j], ssd_conv_b[j], ssd_dt_bias[j],
                           ssd_a_log[j], ssd_d[j], ssd_norm_g[j], ssd_w_out[j], **kw)
        elif kind == 1:
            xa = hgrn_layer(xa, norm_g[i, 0], rowmod, hgrn_w_in[j], lb_all[:, i], hgrn_norm_g[j], hgrn_w_out[j], **kw)
        elif kind == 2:
            xa = attn_layer(xa, norm_g[i, 0], rowmod, attn_w_qkv[j], attn_q_g[j], attn_k_g[j], attn_w_o[j],
                            grid_w=grid_w, **kw)
        else:
            xa = mlstm_layer(xa, norm_g[i, 0], rowmod, mlstm_w_up[j], mlstm_conv_w[j], mlstm_conv_b[j], mlstm_w_q[j],
                             mlstm_w_k[j], mlstm_w_v[j], mlstm_w_gate[j], mlstm_b_gate[j], mlstm_skip[j],
                             mlstm_norm_g[j], mlstm_w_down[j], **kw)
        if i % 2 == 0:
            xa = dense_ffn(xa, norm_g[i, 1], rowmod, ffn_w1[i // 2].astype(BF16), ffn_w3[i // 2].astype(BF16),
                           ffn_w2[i // 2].astype(BF16))
        else:
            xa = moe_ffn(xa, norm_g[i, 1], rowmod, moe_router[i // 2], moe_w1[i // 2].astype(BF16),
                         moe_w3[i // 2].astype(BF16), moe_w2[i // 2].astype(BF16))
    return xa.reshape(bsz, t_all, D_MODEL)[:, ctx_len:]
```

```python
import functools
import math

import jax
import jax.numpy as jnp
import numpy as np
from jax import lax
from jax.experimental import pallas as pl
from jax.experimental.pallas import tpu as pltpu

F32 = jnp.float32
BF16 = jnp.bfloat16
HI = lax.Precision.HIGHEST

D_MODEL = 1024
EPS = 1e-6
ROW_GROUP = 256
CONV_W = 5
NEG_BIG = -1e30
VMEM_LIMIT = 56 << 20

SSD_INNER = 2 * D_MODEL
SSD_P = 64
SSD_HEADS = SSD_INNER // SSD_P
SSD_N = 128
SSD_GROUPS = 8
SSD_GW = SSD_INNER // SSD_GROUPS
SCAN_L = 128

HG_HEADS = 8
HG_DK = 128
HG_SUB = 8

ATT_HEADS = 16
ATT_KV = 4
ATT_HD = 64
ROPE_THETA = 10000.0
ROPE_FREQS = ATT_HD // 4
ATT_TQ = 128

ML_INNER = 2 * D_MODEL
ML_HEADS = 4
ML_DH = ML_INNER // ML_HEADS

D_FF = 7 * D_MODEL // 2
N_EXPERTS = 8
MOE_TM = 512


def _cparams(sem):
    return pltpu.CompilerParams(dimension_semantics=sem, vmem_limit_bytes=VMEM_LIMIT)


def _dot(a, b):
    return jnp.dot(a, b, preferred_element_type=F32)


def _dot_hi(a, b):
    return jnp.dot(a, b, preferred_element_type=F32, precision=HI)


def _dot_nt(a, b):
    return lax.dot_general(a, b, (((1,), (1,)), ((), ())), preferred_element_type=F32)


def _dot_tn(a, b):
    return lax.dot_general(a, b, (((0,), (0,)), ((), ())), preferred_element_type=F32)


def _silu(x):
    return x * jax.nn.sigmoid(x)


def _log_sigmoid(x):
    return jnp.minimum(x, 0.0) - jnp.log1p(jnp.exp(-jnp.abs(x)))


def _softplus(x):
    return jnp.maximum(x, 0.0) + jnp.log1p(jnp.exp(-jnp.abs(x)))


def _pick_tile(m, pref):
    t = pref
    while m % t:
        t //= 2
    return t


def _rows_scale_shift(y, mod_ref, scale_idx, shift_idx):
    parts = []
    for gi in range(y.shape[0] // ROW_GROUP):
        sl = y[gi * ROW_GROUP:(gi + 1) * ROW_GROUP]
        parts.append(sl * (1.0 + mod_ref[gi, scale_idx:scale_idx + 1, :]) + mod_ref[gi, shift_idx:shift_idx + 1, :])
    return parts[0] if len(parts) == 1 else jnp.concatenate(parts, axis=0)


def _rows_gate_residual(x, acc, mod_ref, gate_idx):
    parts = []
    for gi in range(x.shape[0] // ROW_GROUP):
        sl = slice(gi * ROW_GROUP, (gi + 1) * ROW_GROUP)
        parts.append(x[sl] + mod_ref[gi, gate_idx:gate_idx + 1, :] * acc[sl])
    return parts[0] if len(parts) == 1 else jnp.concatenate(parts, axis=0)


def _norm_mod(x, g, mod_ref, shift_idx, scale_idx):
    y = x * lax.rsqrt(jnp.mean(x * x, axis=-1, keepdims=True) + EPS) * g
    return _rows_scale_shift(y, mod_ref, scale_idx, shift_idx)


def _split2(a):
    hi = a.astype(BF16)
    return hi, (a - hi.astype(F32)).astype(BF16)


def _dot_sel_l(sel, a):
    hi, lo = _split2(a)
    return _dot(sel, hi) + _dot(sel, lo)


def _dot_sel_r(a, sel):
    hi, lo = _split2(a)
    return _dot(hi, sel) + _dot(lo, sel)


def _scan_masks(n, bwd):
    ii = lax.broadcasted_iota(jnp.int32, (n, n), 0)
    jj = lax.broadcasted_iota(jnp.int32, (n, n), 1)
    visible = (ii <= jj) if bwd else (ii >= jj)
    visible_t = (ii >= jj) if bwd else (ii <= jj)
    tri = jnp.where(visible, 1.0, 0.0).astype(BF16)
    trit = jnp.where(visible_t, 1.0, 0.0).astype(BF16)
    return ii, jj, visible, tri, trit


def _ada_kernel(c_ref, w_ref, b_ref, o_ref):
    c = c_ref[...]
    o_ref[...] = _dot_hi(_silu(c), w_ref[...]) + b_ref[...]


def ada_modulation(c_pad, w, b):
    n = w.shape[1]
    tn = 1024
    return pl.pallas_call(
        _ada_kernel,
        grid=(n // tn,),
        in_specs=[pl.BlockSpec(c_pad.shape, lambda j: (0, 0)),
                  pl.BlockSpec((D_MODEL, tn), lambda j: (0, j)),
                  pl.BlockSpec((1, tn), lambda j: (0, j))],
        out_specs=pl.BlockSpec((c_pad.shape[0], tn), lambda j: (0, j)),
        out_shape=jax.ShapeDtypeStruct((c_pad.shape[0], n), F32),
        compiler_params=_cparams(("arbitrary",)),
        name="ada_modulation",
    )(c_pad, w, b.reshape(1, n))


def _inproj_kernel(x_ref, g_ref, mod_ref, *refs, n_w, shift_idx, scale_idx, tn):
    h = _norm_mod(x_ref[...], g_ref[...], mod_ref, shift_idx, scale_idx).astype(BF16)
    for w_ref, o_ref in zip(refs[:n_w], refs[n_w:]):
        n = o_ref.shape[1]
        step = min(tn, n)
        for j in range(n // step):
            o_ref[:, j * step:(j + 1) * step] = _dot(h, w_ref[:, j * step:(j + 1) * step]).astype(o_ref.dtype)


def in_projection(x, g, rowmod, ws, out_dtypes, *, shift_idx, scale_idx, tm=256, tn=512):
    m = x.shape[0]
    tm = _pick_tile(m, tm)
    gm = tm // ROW_GROUP
    in_specs = [pl.BlockSpec((tm, D_MODEL), lambda i: (i, 0)),
                pl.BlockSpec((1, D_MODEL), lambda i: (0, 0)),
                pl.BlockSpec((gm, 6, D_MODEL), lambda i: (i, 0, 0))]
    in_specs += [pl.BlockSpec(w.shape, lambda i: (0, 0)) for w in ws]
    out_specs = [pl.BlockSpec((tm, w.shape[1]), lambda i: (i, 0)) for w in ws]
    out_shape = [jax.ShapeDtypeStruct((m, w.shape[1]), dt) for w, dt in zip(ws, out_dtypes)]
    return pl.pallas_call(
        functools.partial(_inproj_kernel, n_w=len(ws), shift_idx=shift_idx, scale_idx=scale_idx, tn=tn),
        grid=(m // tm,),
        in_specs=in_specs, out_specs=out_specs, out_shape=out_shape,
        compiler_params=_cparams(("parallel",)),
        name="in_projection",
    )(x, g.reshape(1, D_MODEL), rowmod, *ws)


def _outproj_kernel(*refs, n_pro, pro_fn, gate_idx):
    pro_refs = refs[:n_pro]
    w_ref, x_ref, mod_ref, o_ref = refs[n_pro:]
    a = pro_fn(*pro_refs).astype(BF16)
    acc = _dot(a, w_ref[...])
    o_ref[...] = _rows_gate_residual(x_ref[...], acc, mod_ref, gate_idx)


def out_projection(pro_fn, pro_args, pro_specs, w, x, rowmod, *, gate_idx, tm):
    m = x.shape[0]
    gm = tm // ROW_GROUP
    in_specs = list(pro_specs) + [pl.BlockSpec(w.shape, lambda i: (0, 0)),
                                  pl.BlockSpec((tm, D_MODEL), lambda i: (i, 0)),
                                  pl.BlockSpec((gm, 6, D_MODEL), lambda i: (i, 0, 0))]
    return pl.pallas_call(
        functools.partial(_outproj_kernel, n_pro=len(pro_args), pro_fn=pro_fn, gate_idx=gate_idx),
        grid=(m // tm,),
        in_specs=in_specs,
        out_specs=pl.BlockSpec((tm, D_MODEL), lambda i: (i, 0)),
        out_shape=jax.ShapeDtypeStruct((m, D_MODEL), F32),
        compiler_params=_cparams(("parallel",)),
        name="out_projection",
    )(*pro_args, w, x, rowmod)


def _conv_kernel(u_ref, w_ref, b_ref, o_ref, *, ctx_len):
    x = u_ref[...].astype(F32)
    t_all = x.shape[0]
    t = lax.broadcasted_iota(jnp.int32, (t_all, 1), 0)
    in_ctx = t < ctx_len
    pad = CONV_W // 2
    acc = b_ref[...] + w_ref[pad:pad + 1, :] * x
    for off in range(-pad, pad + 1):
        if off == 0:
            continue
        xs = pltpu.roll(x, (-off) % t_all, 0)
        tt = t + off
        valid = (tt >= 0) & (tt < t_all) & ((tt < ctx_len) == in_ctx)
        acc = acc + w_ref[pad + off:pad + off + 1, :] * jnp.where(valid, xs, 0.0)
    o_ref[...] = _silu(acc).astype(o_ref.dtype)


def dwconv_silu(u, col_off, width, w, b, *, bsz, ctx_len, tc=256):
    m = u.shape[0]
    t_all = m // bsz
    cb = col_off // tc
    return pl.pallas_call(
        functools.partial(_conv_kernel, ctx_len=ctx_len),
        grid=(bsz, width // tc),
        in_specs=[pl.BlockSpec((t_all, tc), lambda bi, j: (bi, j + cb)),
                  pl.BlockSpec((CONV_W, tc), lambda bi, j: (0, j)),
                  pl.BlockSpec((1, tc), lambda bi, j: (0, j))],
        out_specs=pl.BlockSpec((t_all, tc), lambda bi, j: (bi, j)),
        out_shape=jax.ShapeDtypeStruct((m, width), BF16),
        compiler_params=_cparams(("parallel", "parallel")),
        name="dwconv_silu",
    )(u, w, b.reshape(1, width))


def _scan_chunk_index(bwd, c, n_ctx_chunks, n_chunks):
    if not bwd:
        return c
    return jnp.where(c < n_ctx_chunks, n_ctx_chunks - 1 - c, n_chunks - 1 + n_ctx_chunks - c)


def _ssd_scan_kernel(x_ref, b_ref, c_ref, dt_ref, dtt_ref, bias_ref, biast_ref, alog_ref, alogt_ref, exp_ref,
                     y_ref, state, *, bwd):
    ci = pl.program_id(1)
    L = SCAN_L
    H = SSD_HEADS

    @pl.when(ci == 0)
    def _():
        state[...] = jnp.zeros_like(state)

    _, _, visible, tri, trit = _scan_masks(L, bwd)
    dt = _softplus(dt_ref[...] + bias_ref[...])
    dtt = _softplus(dtt_ref[...] + biast_ref[...])
    a_neg = -jnp.exp(alog_ref[...])
    a_negt = -jnp.exp(alogt_ref[...])
    dta = dt * a_neg
    a_col = _dot_sel_l(tri, dta)
    a_row = _dot_sel_r(dtt * a_negt, trit)
    a_end = jnp.sum(dta, axis=0, keepdims=True)
    per_head = jnp.concatenate([jnp.exp(a_col), jnp.exp(a_end - a_col) * dt,
                                jnp.broadcast_to(jnp.exp(a_end), (16, H))], axis=0)
    per_chan = _dot_sel_r(per_head, exp_ref[...])
    ea_x, w_x, eend_x = per_chan[:L], per_chan[L:2 * L], per_chan[2 * L:2 * L + 1]
    lane = lax.broadcasted_iota(jnp.int32, (L, 2 * SSD_P), 1)
    for g in range(SSD_GROUPS):
        gs = slice(g * SSD_GW, (g + 1) * SSD_GW)
        ns = slice(g * SSD_N, (g + 1) * SSD_N)
        bg = b_ref[:, ns]
        cg = c_ref[:, ns]
        xgb = x_ref[:, gs]
        cb = _dot_nt(cg, bg)
        sg = state[:, gs]
        y_inter = _dot(cg, sg.astype(BF16)) * ea_x[:, gs]
        pieces = []
        for pr in range(SSD_GW // (2 * SSD_P)):
            ys = []
            for q in range(2):
                h = g * (SSD_GW // SSD_P) + 2 * pr + q
                rel = a_col[:, h:h + 1] - a_row[h:h + 1, :]
                gm = cb * jnp.exp(jnp.where(visible, rel, NEG_BIG)) * dtt[h:h + 1, :]
                ys.append(_dot(gm.astype(BF16), xgb[:, pr * 2 * SSD_P:(pr + 1) * 2 * SSD_P]))
            pieces.append(jnp.where(lane < SSD_P, ys[0], ys[1]))
        y = jnp.concatenate(pieces, axis=1) + y_inter
        y_ref[:, gs] = y.astype(y_ref.dtype)
        xw = (xgb.astype(F32) * w_x[:, gs]).astype(BF16)
        state[:, gs] = sg * eend_x[:, gs] + _dot_tn(bg, xw)


def ssd_scan(u, dt_dir, dtt_dir, dt_bias, a_log, *, bsz, ctx_len, bwd):
    m = u.shape[0]
    t_all = m // bsz
    nc = t_all // SCAN_L
    ncc = ctx_len // SCAN_L
    L, H = SCAN_L, SSD_HEADS
    d = int(bwd)

    def row(bi, c):
        return bi * nc + _scan_chunk_index(bwd, c, ncc, nc)

    bn = SSD_INNER // (SSD_GROUPS * SSD_N)
    expand = jnp.asarray(np.kron(np.eye(H, dtype=np.float32), np.ones((1, SSD_P), np.float32)), BF16)
    return pl.pallas_call(
        functools.partial(_ssd_scan_kernel, bwd=bwd),
        grid=(bsz, nc),
        in_specs=[pl.BlockSpec((L, SSD_INNER), lambda bi, c: (row(bi, c), 0)),
                  pl.BlockSpec((L, SSD_GROUPS * SSD_N), lambda bi, c: (row(bi, c), bn)),
                  pl.BlockSpec((L, SSD_GROUPS * SSD_N), lambda bi, c: (row(bi, c), bn + 1)),
                  pl.BlockSpec((None, L, H), lambda bi, c: (d, row(bi, c), 0)),
                  pl.BlockSpec((None, H, L), lambda bi, c: (d, 0, row(bi, c))),
                  pl.BlockSpec((None, 1, H), lambda bi, c: (d, 0, 0)),
                  pl.BlockSpec((None, H, 1), lambda bi, c: (d, 0, 0)),
                  pl.BlockSpec((None, 1, H), lambda bi, c: (d, 0, 0)),
                  pl.BlockSpec((None, H, 1), lambda bi, c: (d, 0, 0)),
                  pl.BlockSpec((H, SSD_INNER), lambda bi, c: (0, 0))],
        out_specs=pl.BlockSpec((L, SSD_INNER), lambda bi, c: (row(bi, c), 0)),
        out_shape=jax.ShapeDtypeStruct((m, SSD_INNER), BF16),
        scratch_shapes=[pltpu.VMEM((SSD_N, SSD_INNER), F32)],
        compiler_params=_cparams(("arbitrary", "arbitrary")),
        name="ssd_scan_bwd" if bwd else "ssd_scan_fwd",
    )(u, u, u, dt_dir, dtt_dir, dt_bias.reshape(2, 1, H), dt_bias.reshape(2, H, 1),
      a_log.reshape(2, 1, H), a_log.reshape(2, H, 1), expand)


def _ssd_finish(yf_ref, yb_ref, xs_ref, z_ref, d_ref, g_ref):
    y = yf_ref[...].astype(F32) + yb_ref[...].astype(F32) + d_ref[...] * xs_ref[...].astype(F32)
    y = y * _silu(z_ref[...].astype(F32))
    parts = []
    for g in range(SSD_GROUPS):
        sl = y[:, g * SSD_GW:(g + 1) * SSD_GW]
        parts.append(sl * lax.rsqrt(jnp.mean(sl * sl, axis=-1, keepdims=True) + EPS))
    return jnp.concatenate(parts, axis=1) * g_ref[...]


def _gla_scan_kernel(q_ref, f_ref, v_ref, llb_ref, l1m_ref, omlb_ref, place_ref, o_ref, state, *, bwd):
    ci = pl.program_id(1)
    L, C = SCAN_L, HG_SUB
    nb = L // C
    dk = HG_DK

    @pl.when(ci == 0)
    def _():
        state[...] = jnp.zeros_like(state)

    ii, jj, visible, tri, _ = _scan_masks(L, bwd)
    band_mask = ((ii // C) == (jj // C)) & visible
    off_mask = ((ii // C) < (jj // C)) if bwd else ((ii // C) > (jj // C))
    row_sub = lax.broadcasted_iota(jnp.int32, (L, 1), 0) // C
    last = 0 if bwd else L - 1
    blocks = range(1, nb) if bwd else range(nb - 1)

    def block_end(j):
        return j * C if bwd else (j + 1) * C - 1

    def head(h, carry):
        ls = pl.ds(pl.multiple_of(h * dk, dk), dk)
        q = _silu(q_ref[:, ls].astype(F32)) * (dk ** -0.5)
        f = f_ref[:, ls].astype(F32)
        v = v_ref[:, ls]
        a = llb_ref[:, ls]
        b = l1m_ref[:, ls] + _log_sigmoid(f)
        lf = jnp.maximum(a, b) + jnp.log1p(jnp.exp(-jnp.abs(a - b)))
        k = omlb_ref[:, ls] * jax.nn.sigmoid(-f)
        cum = _dot_sel_l(tri, lf)
        c_end = cum[last:last + 1, :]
        es = []
        for dl in range(C):
            if dl == 0:
                e = q * k
            else:
                sh = (L - dl) if bwd else dl
                ks = pltpu.roll(k, sh, 0)
                cs = pltpu.roll(cum, sh, 0)
                e = q * ks * jnp.exp(jnp.minimum(cum - cs, 0.0))
            es.append(e.astype(BF16))
        band = _dot(jnp.concatenate(es, axis=1), place_ref[...])
        band = pltpu.roll(band, 0, 1, stride=1, stride_axis=0)
        kparts = []
        for j in range(nb):
            rs = slice(j * C, (j + 1) * C)
            e_j = cum[block_end(j):block_end(j) + 1, :]
            kparts.append(k[rs] * jnp.exp(e_j - cum[rs]))
        kt = jnp.concatenate(kparts, axis=0)
        kcat = jnp.concatenate([jnp.where(row_sub == j, kt, 0.0).astype(BF16) for j in blocks], axis=1)
        qparts = []
        for j in blocks:
            e_j = cum[block_end(j):block_end(j) + 1, :]
            rows = slice(0, j * C) if bwd else slice((j + 1) * C, L)
            qt = q[rows] * jnp.exp(cum[rows] - e_j)
            pad = jnp.zeros((L - qt.shape[0], dk), F32)
            qparts.append(jnp.concatenate([qt, pad] if bwd else [pad, qt], axis=0).astype(BF16))
        off = _dot_nt(jnp.concatenate(qparts, axis=1), kcat)
        att = jnp.where(band_mask, band, jnp.where(off_mask, off, 0.0))
        st = state[h]
        o = _dot(att.astype(BF16), v) + _dot_nt((q * jnp.exp(cum)).astype(BF16), st.astype(BF16))
        o_ref[:, ls] = o.astype(o_ref.dtype)
        kw = (k * jnp.exp(c_end - cum)).astype(BF16)
        state[h] = st * jnp.exp(c_end) + _dot_tn(v, kw)
        return carry

    lax.fori_loop(0, HG_HEADS, head, 0)


def _gla_place_matrix(bwd):
    pm = np.zeros((HG_SUB * HG_DK, SCAN_L), np.float32)
    for dl in range(HG_SUB):
        pm[dl * HG_DK:(dl + 1) * HG_DK, dl if bwd else (SCAN_L - dl) % SCAN_L] = 1.0
    return jnp.asarray(pm, BF16)


def gla_scan(proj, llb, l1m, omlb, *, bsz, ctx_len, bwd):
    m = proj.shape[0]
    t_all = m // bsz
    nc = t_all // SCAN_L
    ncc = ctx_len // SCAN_L
    L = SCAN_L
    d = int(bwd)

    def row(bi, c):
        return bi * nc + _scan_chunk_index(bwd, c, ncc, nc)

    vec = pl.BlockSpec((None, 1, D_MODEL), lambda bi, c: (d, 0, 0))
    place = _gla_place_matrix(bwd)
    return pl.pallas_call(
        functools.partial(_gla_scan_kernel, bwd=bwd),
        grid=(bsz, nc),
        in_specs=[pl.BlockSpec((L, D_MODEL), lambda bi, c: (row(bi, c), 0)),
                  pl.BlockSpec((L, D_MODEL), lambda bi, c: (row(bi, c), 1 + d)),
                  pl.BlockSpec((L, D_MODEL), lambda bi, c: (row(bi, c), 3)),
                  vec, vec, vec,
                  pl.BlockSpec(place.shape, lambda bi, c: (0, 0))],
        out_specs=pl.BlockSpec((L, D_MODEL), lambda bi, c: (row(bi, c), 0)),
        out_shape=jax.ShapeDtypeStruct((m, D_MODEL), BF16),
        scratch_shapes=[pltpu.VMEM((HG_HEADS, HG_DK, HG_DK), F32)],
        compiler_params=_cparams(("arbitrary", "arbitrary")),
        name="gla_scan_bwd" if bwd else "gla_scan_fwd",
    )(proj, proj, proj, llb, l1m, omlb, place)


def _hgrn_finish(of_ref, ob_ref, gate_ref, g_ref):
    o = of_ref[...].astype(F32) + ob_ref[...].astype(F32)
    parts = []
    for h in range(HG_HEADS):
        sl = o[:, h * HG_DK:(h + 1) * HG_DK]
        parts.append(sl * lax.rsqrt(jnp.mean(sl * sl, axis=-1, keepdims=True) + EPS))
    return jnp.concatenate(parts, axis=1) * g_ref[...] * _silu(gate_ref[...].astype(F32))


def _attn_qkv_kernel(x_ref, g_ref, mod_ref, w_ref, cos_ref, sin_ref, qg_ref, kg_ref, bd_ref, rot_ref,
                     q_ref, k_ref, v_ref, *, shift_idx, scale_idx):
    h = _norm_mod(x_ref[...], g_ref[...], mod_ref, shift_idx, scale_idx).astype(BF16)
    nq = ATT_HEADS * ATT_HD
    nk = ATT_KV * ATT_HD

    def norm_rope(a, gain, width, out_scale):
        ss = _dot((a * a).astype(BF16), bd_ref[:width, :width])
        an = a * lax.rsqrt(ss * (1.0 / ATT_HD) + EPS) * gain
        rot = _dot(an.astype(BF16), rot_ref[:width, :width])
        cos = jnp.concatenate([cos_ref[...]] * (width // 128), axis=1)
        sin = jnp.concatenate([sin_ref[...]] * (width // 128), axis=1)
        return (an * cos + rot * sin) * out_scale

    q = _dot(h, w_ref[:, :nq])
    q_ref[...] = norm_rope(q, qg_ref[...], nq, ATT_HD ** -0.5 * math.log2(math.e)).astype(q_ref.dtype)
    k = _dot(h, w_ref[:, nq:nq + nk])
    k_ref[...] = norm_rope(k, kg_ref[...], nk, 1.0).astype(k_ref.dtype)
    v_ref[...] = _dot(h, w_ref[:, nq + nk:]).astype(v_ref.dtype)


def attn_qkv(x, g, rowmod, w, cos_t, sin_t, q_g, k_g, bd, rot, *, bsz, shift_idx, scale_idx):
    m = x.shape[0]
    tm = ROW_GROUP
    tpb = (m // bsz) // tm
    nq = ATT_HEADS * ATT_HD
    nk = ATT_KV * ATT_HD
    const = lambda shape: pl.BlockSpec(shape, lambda i: (0, 0))
    return pl.pallas_call(
        functools.partial(_attn_qkv_kernel, shift_idx=shift_idx, scale_idx=scale_idx),
        grid=(m // tm,),
        in_specs=[pl.BlockSpec((tm, D_MODEL), lambda i: (i, 0)),
                  const((1, D_MODEL)),
                  pl.BlockSpec((1, 6, D_MODEL), lambda i: (i, 0, 0)),
                  const(w.shape),
                  pl.BlockSpec((tm, 128), lambda i: (i % tpb, 0)),
                  pl.BlockSpec((tm, 128), lambda i: (i % tpb, 0)),
                  const((1, nq)), const((1, nk)), const(bd.shape), const(rot.shape)],
        out_specs=[pl.BlockSpec((tm, nq), lambda i: (i, 0)),
                   pl.BlockSpec((tm, nk), lambda i: (i, 0)),
                   pl.BlockSpec((tm, nk), lambda i: (i, 0))],
        out_shape=[jax.ShapeDtypeStruct((m, nq), BF16),
                   jax.ShapeDtypeStruct((m, nk), BF16),
                   jax.ShapeDtypeStruct((m, nk), BF16)],
        compiler_params=_cparams(("parallel",)),
        name="attn_qkv",
    )(x, g.reshape(1, D_MODEL), rowmod, w, cos_t, sin_t, q_g, k_g, bd, rot)


def _attn_kernel(q_ref, k_ref, v_ref, o_ref, vaug, *, ctx_len):
    qt = pl.program_id(2)
    tq = q_ref.shape[0]
    lane = lax.broadcasted_iota(jnp.int32, (1, 2 * ATT_HD), 1)
    n_r = q_ref.shape[1] // (2 * ATT_HD)

    @pl.when(qt == 0)
    def _():
        v = v_ref[...]
        one = jnp.ones((), BF16)
        vaug[0] = jnp.where(lane < ATT_HD, v, one)
        vaug[1] = jnp.where(lane < ATT_HD, one, v)

    def attend(n_keys):
        outs = []
        for j in range(2):
            sel = (lane // ATT_HD) == j
            qs = jnp.concatenate(
                [jnp.where(sel, q_ref[:, r * 128:(r + 1) * 128], jnp.zeros((), BF16)) for r in range(n_r)], axis=0)
            s = _dot_nt(qs, k_ref[:n_keys, :])
            mx = jnp.max(s, axis=-1, keepdims=True)
            p = jnp.exp2((s - mx).astype(BF16))
            ov = _dot(p, vaug[j, :n_keys, :])
            outs.append(ov / pltpu.roll(ov, ATT_HD, 1))
        o = jnp.where(lane < ATT_HD, outs[0], outs[1])
        for r in range(n_r):
            o_ref[:, r * 128:(r + 1) * 128] = o[r * tq:(r + 1) * tq].astype(o_ref.dtype)

    n_ctx_tiles = ctx_len // tq

    @pl.when(qt < n_ctx_tiles)
    def _():
        attend(ctx_len)

    @pl.when(qt >= n_ctx_tiles)
    def _():
        attend(k_ref.shape[0])


def attention(q, k, v, *, bsz, ctx_len):
    m = q.shape[0]
    t_all = m // bsz
    tq = ATT_TQ
    nqt = t_all // tq
    n_pairs = ATT_KV // 2
    qw = q.shape[1] // n_pairs
    return pl.pallas_call(
        functools.partial(_attn_kernel, ctx_len=ctx_len),
        grid=(bsz, n_pairs, nqt),
        in_specs=[pl.BlockSpec((tq, qw), lambda bi, p, t: (bi * nqt + t, p)),
                  pl.BlockSpec((t_all, 2 * ATT_HD), lambda bi, p, t: (bi, p)),
                  pl.BlockSpec((t_all, 2 * ATT_HD), lambda bi, p, t: (bi, p))],
        out_specs=pl.BlockSpec((tq, qw), lambda bi, p, t: (bi * nqt + t, p)),
        out_shape=jax.ShapeDtypeStruct(q.shape, BF16),
        scratch_shapes=[pltpu.VMEM((2, t_all, 2 * ATT_HD), BF16)],
        compiler_params=_cparams(("arbitrary", "arbitrary", "arbitrary")),
        name="attention",
    )(q, k, v)


def _identity_pro(a_ref):
    return a_ref[...]


def _mlstm_qkv_kernel(xc_ref, xm_ref, wq_ref, wk_ref, wv_ref, wg_ref, bg_ref, q_ref, k_ref, v_ref, gate_ref):
    gates = jnp.zeros(gate_ref.shape, F32) + bg_ref[...]
    for which, (src, w_ref, o_ref) in enumerate(((xc_ref, wq_ref, q_ref), (xc_ref, wk_ref, k_ref),
                                                 (xm_ref, wv_ref, v_ref))):
        for h in range(ML_HEADS):
            hs = slice(h * ML_DH, (h + 1) * ML_DH)
            r = _dot(src[:, hs], w_ref[h]).astype(BF16)
            o_ref[:, hs] = r
            gates = gates + _dot(r, wg_ref[which * ML_INNER + h * ML_DH:which * ML_INNER + (h + 1) * ML_DH, :])
    gate_ref[...] = gates


def mlstm_qkv(xc, up, wq, wk, wv, wg, bg, *, tm=256):
    m = xc.shape[0]
    ng = wg.shape[1]
    const = lambda shape: pl.BlockSpec(shape, lambda i: (0,) * len(shape))
    row = pl.BlockSpec((tm, ML_INNER), lambda i: (i, 0))
    return pl.pallas_call(
        _mlstm_qkv_kernel,
        grid=(m // tm,),
        in_specs=[row, row, const(wq.shape), const(wk.shape), const(wv.shape), const(wg.shape), const((1, ng))],
        out_specs=[row, row, row, pl.BlockSpec((tm, ng), lambda i: (i, 0))],
        out_shape=[jax.ShapeDtypeStruct((m, ML_INNER), BF16)] * 3 + [jax.ShapeDtypeStruct((m, ng), F32)],
        compiler_params=_cparams(("parallel",)),
        name="mlstm_qkv",
    )(xc, up, wq, wk, wv, wg, bg.reshape(1, ng))


def _mlstm_scan_kernel(q_ref, k_ref, v_ref, gt_ref, gtt_ref, h_ref, c_st, n_st, m_st, *, bwd):
    ci = pl.program_id(1)
    L = SCAN_L
    nh = ML_HEADS
    scale = ML_DH ** -0.5

    @pl.when(ci == 0)
    def _():
        c_st[...] = jnp.zeros_like(c_st)
        n_st[...] = jnp.zeros_like(n_st)
        m_st[...] = jnp.full(m_st.shape, NEG_BIG, F32)

    _, _, visible, tri, trit = _scan_masks(L, bwd)
    gt = gt_ref[...]
    gtt = gtt_ref[...]
    li_c, li_r = gt[:, :nh], gtt[:nh, :]
    lf_c, lf_r = _log_sigmoid(gt[:, nh:]), _log_sigmoid(gtt[nh:, :])
    cum_c = _dot_sel_l(tri, lf_c)
    cum_r = _dot_sel_r(lf_r, trit)
    end_c = jnp.sum(lf_c, axis=0, keepdims=True)
    for h in range(nh):
        hs = slice(h * ML_DH, (h + 1) * ML_DH)
        q = q_ref[:, hs]
        kb = k_ref[:, hs]
        k = kb.astype(F32)
        v = v_ref[:, hs]
        m_prev = m_st[h:h + 1, 0:1]
        cum_end = end_c[:, h:h + 1]
        dmat = jnp.where(visible, cum_c[:, h:h + 1] - cum_r[h:h + 1, :] + li_r[h:h + 1, :], -jnp.inf)
        inter = cum_c[:, h:h + 1] + m_prev
        m_t = jnp.maximum(inter, jnp.max(dmat, axis=1, keepdims=True))
        w = jnp.exp(dmat - m_t)
        w_c = jnp.exp(inter - m_t)
        qk = _dot_nt(q, kb) * scale * w
        cmat = c_st[h]
        num = _dot(qk.astype(BF16), v) + w_c * _dot(q, cmat.astype(BF16))
        qn = jnp.sum(q.astype(F32) * n_st[h:h + 1, :], axis=1, keepdims=True)
        den = jnp.sum(qk, axis=1, keepdims=True) + w_c * qn
        hv = num / jnp.maximum(jnp.abs(den), jnp.exp(-m_t))
        h_ref[:, hs] = hv.astype(h_ref.dtype)
        wend_c = cum_end - cum_c[:, h:h + 1] + li_c[:, h:h + 1]
        wend_r = cum_end - cum_r[h:h + 1, :] + li_r[h:h + 1, :]
        m_new = jnp.maximum(cum_end + m_prev, jnp.max(wend_r, axis=1, keepdims=True))
        a_old = jnp.exp(cum_end + m_prev - m_new)
        e_c = jnp.exp(wend_c - m_new) * scale
        e_r = jnp.exp(wend_r - m_new) * scale
        c_st[h] = a_old * cmat + _dot_tn((k * e_c).astype(BF16), v)
        e_r8 = jnp.broadcast_to(e_r, (8, L)).astype(BF16)
        n_st[h:h + 1, :] = a_old * n_st[h:h + 1, :] + _dot(e_r8, kb)[0:1, :]
        m_st[h:h + 1, :] = jnp.broadcast_to(m_new, (1, m_st.shape[1]))


def mlstm_scan(q, k, v, gt_dir, gtt_dir, *, bsz, ctx_len, bwd):
    m = q.shape[0]
    t_all = m // bsz
    nc = t_all // SCAN_L
    ncc = ctx_len // SCAN_L
    L, nh = SCAN_L, ML_HEADS
    d = int(bwd)

    def row(bi, c):
        return bi * nc + _scan_chunk_index(bwd, c, ncc, nc)

    blk = pl.BlockSpec((L, ML_INNER), lambda bi, c: (row(bi, c), 0))
    return pl.pallas_call(
        functools.partial(_mlstm_scan_kernel, bwd=bwd),
        grid=(bsz, nc),
        in_specs=[blk, blk, blk,
                  pl.BlockSpec((None, L, 2 * nh), lambda bi, c: (d, row(bi, c), 0)),
                  pl.BlockSpec((None, 2 * nh, L), lambda bi, c: (d, 0, row(bi, c)))],
        out_specs=pl.BlockSpec((L, ML_INNER), lambda bi, c: (row(bi, c), 0)),
        out_shape=jax.ShapeDtypeStruct((m, ML_INNER), BF16),
        scratch_shapes=[pltpu.VMEM((nh, ML_DH, ML_DH), F32),
                        pltpu.VMEM((8, ML_DH), F32),
                        pltpu.VMEM((8, 128), F32)],
        compiler_params=_cparams(("arbitrary", "arbitrary")),
        name="mlstm_scan_bwd" if bwd else "mlstm_scan_fwd",
    )(q, k, v, gt_dir, gtt_dir)


def _mlstm_finish(hf_ref, hb_ref, xc_ref, z_ref, skip_ref, g_ref):
    hsum = hf_ref[...].astype(F32) + hb_ref[...].astype(F32)
    parts = []
    for h in range(ML_HEADS):
        sl = hsum[:, h * ML_DH:(h + 1) * ML_DH]
        parts.append(sl * lax.rsqrt(jnp.mean(sl * sl, axis=-1, keepdims=True) + EPS))
    hn = jnp.concatenate(parts, axis=1) * g_ref[...]
    return (hn + skip_ref[...] * xc_ref[...].astype(F32)) * _silu(z_ref[...].astype(F32))


def _ffn_kernel(x_ref, g_ref, mod_ref, w1_ref, w3_ref, w2_ref, o_ref, h_sc, acc_sc, *, shift_idx, scale_idx, gate_idx):
    f = pl.program_id(1)

    @pl.when(f == 0)
    def _():
        h_sc[...] = _norm_mod(x_ref[...], g_ref[...], mod_ref, shift_idx, scale_idx).astype(BF16)
        acc_sc[...] = jnp.zeros_like(acc_sc)

    h = h_sc[...]
    a = _silu(_dot(h, w1_ref[...])) * _dot(h, w3_ref[...])
    acc_sc[...] += _dot(a.astype(BF16), w2_ref[...])

    @pl.when(f == pl.num_programs(1) - 1)
    def _():
        o_ref[...] = _rows_gate_residual(x_ref[...], acc_sc[...], mod_ref, gate_idx)


def dense_ffn(x, g, rowmod, w1, w3, w2, layer, *, tm=1024, tf=512):
    m = x.shape[0]
    tm = _pick_tile(m, tm)
    gm = tm // ROW_GROUP
    nf = D_FF // tf
    return pl.pallas_call(
        functools.partial(_ffn_kernel, shift_idx=3, scale_idx=4, gate_idx=5),
        grid=(m // tm, nf),
        in_specs=[pl.BlockSpec((tm, D_MODEL), lambda i, f: (i, 0)),
                  pl.BlockSpec((1, D_MODEL), lambda i, f: (0, 0)),
                  pl.BlockSpec((gm, 6, D_MODEL), lambda i, f: (i, 0, 0)),
                  pl.BlockSpec((None, D_MODEL, tf), lambda i, f: (layer, 0, f)),
                  pl.BlockSpec((None, D_MODEL, tf), lambda i, f: (layer, 0, f)),
                  pl.BlockSpec((None, tf, D_MODEL), lambda i, f: (layer, f, 0))],
        out_specs=pl.BlockSpec((tm, D_MODEL), lambda i, f: (i, 0)),
        out_shape=jax.ShapeDtypeStruct((m, D_MODEL), F32),
        scratch_shapes=[pltpu.VMEM((tm, D_MODEL), BF16), pltpu.VMEM((tm, D_MODEL), F32)],
        compiler_params=_cparams(("parallel", "arbitrary")),
        name="dense_ffn",
    )(x, g.reshape(1, D_MODEL), rowmod, w1, w3, w2)


def _router_kernel(x_ref, g_ref, mod_ref, wr_ref, h_ref, route_ref, cnt_ref, cnt_sc, *, shift_idx, scale_idx, n_real):
    i = pl.program_id(0)

    @pl.when(i == 0)
    def _():
        cnt_sc[...] = jnp.zeros_like(cnt_sc)

    @pl.when(i < n_real)
    def _():
        h = _norm_mod(x_ref[...], g_ref[...], mod_ref, shift_idx, scale_idx)
        h_ref[...] = h.astype(h_ref.dtype)
        logits = _dot_hi(h, wr_ref[...])
        tm = logits.shape[0]
        lane = lax.broadcasted_iota(jnp.int32, logits.shape, 1)
        logits = jnp.where(lane < N_EXPERTS, logits, -jnp.inf)
        m1 = jnp.max(logits, axis=-1, keepdims=True)
        i1 = jnp.min(jnp.where(logits == m1, lane, 128), axis=-1, keepdims=True)
        rest = jnp.where(lane == i1, -jnp.inf, logits)
        m2 = jnp.max(rest, axis=-1, keepdims=True)
        i2 = jnp.min(jnp.where(rest == m2, lane, 128), axis=-1, keepdims=True)
        e2 = jnp.exp(m2 - m1)
        w1 = 1.0 / (1.0 + e2)
        w2 = e2 / (1.0 + e2)
        chosen = jnp.where(lane == i1, 1.0, jnp.where(lane == i2, 1.0, 0.0))
        ii = lax.broadcasted_iota(jnp.int32, (tm, tm), 0)
        jj = lax.broadcasted_iota(jnp.int32, (tm, tm), 1)
        strict = jnp.where(ii > jj, 1.0, 0.0).astype(BF16)
        prefix = _dot(strict, chosen.astype(BF16)) + cnt_sc[0:1, :]
        r1 = jnp.sum(jnp.where(lane == i1, prefix, 0.0), axis=-1, keepdims=True)
        r2 = jnp.sum(jnp.where(lane == i2, prefix, 0.0), axis=-1, keepdims=True)
        cnt_sc[...] = cnt_sc[...] + jnp.sum(chosen, axis=0, keepdims=True)
        route = jnp.zeros(logits.shape, F32)
        for col, val in enumerate((i1.astype(F32), i2.astype(F32), w1, w2, r1, r2)):
            route = jnp.where(lane == col, val, route)
        route_ref[...] = route

    @pl.when(i >= n_real)
    def _():
        h_ref[...] = jnp.zeros_like(h_ref)
        route_ref[...] = jnp.zeros_like(route_ref)

    cnt_ref[...] = cnt_sc[...]


def moe_router(x, g, rowmod, wr_pad, p_rows, *, tm=256):
    m = x.shape[0]
    n_real = m // tm
    gm = tm // ROW_GROUP
    clamp = lambda i: jnp.minimum(i, n_real - 1)
    return pl.pallas_call(
        functools.partial(_router_kernel, shift_idx=3, scale_idx=4, n_real=n_real),
        grid=(p_rows // tm,),
        in_specs=[pl.BlockSpec((tm, D_MODEL), lambda i: (clamp(i), 0)),
                  pl.BlockSpec((1, D_MODEL), lambda i: (0, 0)),
                  pl.BlockSpec((gm, 6, D_MODEL), lambda i: (clamp(i), 0, 0)),
                  pl.BlockSpec(wr_pad.shape, lambda i: (0, 0))],
        out_specs=[pl.BlockSpec((tm, D_MODEL), lambda i: (i, 0)), pl.BlockSpec((tm, 128), lambda i: (i, 0)),
                   pl.BlockSpec((8, 128), lambda i: (0, 0))],
        out_shape=[jax.ShapeDtypeStruct((p_rows, D_MODEL), BF16), jax.ShapeDtypeStruct((p_rows, 128), F32),
                   jax.ShapeDtypeStruct((8, 128), F32)],
        scratch_shapes=[pltpu.VMEM((8, 128), F32)],
        compiler_params=_cparams(("arbitrary",)),
        name="moe_router",
    )(x, g.reshape(1, D_MODEL), rowmod, wr_pad)


def _moe_ffn_kernel(te_ref, nt_ref, h_ref, w1_ref, w3_ref, w2_ref, o_ref, acc_sc):
    i = pl.program_id(0)
    f = pl.program_id(1)
    live = i < nt_ref[0]

    @pl.when(f == 0)
    def _():
        acc_sc[...] = jnp.zeros_like(acc_sc)

    @pl.when(live)
    def _():
        h = h_ref[...]
        a = _silu(_dot(h, w1_ref[...])) * _dot(h, w3_ref[...])
        acc_sc[...] += _dot(a.astype(BF16), w2_ref[...])

    @pl.when(f == pl.num_programs(1) - 1)
    def _():
        o_ref[...] = acc_sc[...].astype(o_ref.dtype)


def moe_grouped_ffn(h_sorted, tile_expert, n_tiles, w1, w3, w2, layer, *, tm=MOE_TM, tf=512):
    p = h_sorted.shape[0]
    nf = D_FF // tf
    grid_spec = pltpu.PrefetchScalarGridSpec(
        num_scalar_prefetch=2,
        grid=(p // tm, nf),
        in_specs=[pl.BlockSpec((tm, D_MODEL), lambda i, f, te, nt: (i, 0)),
                  pl.BlockSpec((None, None, D_MODEL, tf), lambda i, f, te, nt: (layer, te[i], 0, f)),
                  pl.BlockSpec((None, None, D_MODEL, tf), lambda i, f, te, nt: (layer, te[i], 0, f)),
                  pl.BlockSpec((None, None, tf, D_MODEL), lambda i, f, te, nt: (layer, te[i], f, 0))],
        out_specs=pl.BlockSpec((tm, D_MODEL), lambda i, f, te, nt: (i, 0)),
        scratch_shapes=[pltpu.VMEM((tm, D_MODEL), F32)])
    return pl.pallas_call(
        _moe_ffn_kernel,
        grid_spec=grid_spec,
        out_shape=jax.ShapeDtypeStruct((p, D_MODEL), BF16),
        compiler_params=_cparams(("arbitrary", "arbitrary")),
        name="moe_grouped_ffn",
    )(tile_expert, n_tiles, h_sorted, w1, w3, w2)


def _moe_combine_kernel(x_ref, ya_ref, yb_ref, route_ref, mod_ref, o_ref, *, gate_idx):
    r = route_ref[...]
    y = r[:, 2:3] * ya_ref[...].astype(F32) + r[:, 3:4] * yb_ref[...].astype(F32)
    o_ref[...] = _rows_gate_residual(x_ref[...], y, mod_ref, gate_idx)


def moe_combine(x, ya, yb, route, rowmod, *, tm=256):
    m = x.shape[0]
    row = pl.BlockSpec((tm, D_MODEL), lambda i: (i, 0))
    return pl.pallas_call(
        functools.partial(_moe_combine_kernel, gate_idx=5),
        grid=(m // tm,),
        in_specs=[row, row, row, pl.BlockSpec((tm, 128), lambda i: (i, 0)),
                  pl.BlockSpec((tm // ROW_GROUP, 6, D_MODEL), lambda i: (i, 0, 0))],
        out_specs=row,
        out_shape=jax.ShapeDtypeStruct((m, D_MODEL), F32),
        compiler_params=_cparams(("parallel",)),
        name="moe_combine",
    )(x, ya, yb, route, rowmod)


def moe_ffn(x, g, rowmod, w_router, w1, w3, w2, layer):
    m = x.shape[0]
    tm = MOE_TM
    n_tiles_max = (2 * m) // tm + N_EXPERTS
    p = n_tiles_max * tm
    wr_pad = jnp.zeros((D_MODEL, 128), F32).at[:, :N_EXPERTS].set(w_router)
    h, route, cnt = moe_router(x, g, rowmod, wr_pad, p)
    experts = jnp.arange(N_EXPERTS, dtype=jnp.int32)
    counts = cnt[0, :N_EXPERTS].astype(jnp.int32)
    tiles_per = (counts + tm - 1) // tm
    tile_end = jnp.cumsum(tiles_per)
    grp_start = (tile_end - tiles_per) * tm
    cnt_start = jnp.cumsum(counts) - counts
    e12 = route[:m, 0:2].astype(jnp.int32)
    r12 = route[:m, 4:6].astype(jnp.int32)
    pos12 = jnp.sum(jnp.where(e12[:, :, None] == experts, grp_start, 0), axis=-1) + r12
    order = jnp.argsort(e12.reshape(-1), stable=True).astype(jnp.int32)
    tile_expert = jnp.minimum(
        jnp.sum(jnp.arange(n_tiles_max, dtype=jnp.int32)[:, None] >= tile_end[None, :], axis=1), N_EXPERTS - 1
    ).astype(jnp.int32)
    n_tiles = tile_end[-1:].astype(jnp.int32)
    rank = jnp.arange(p, dtype=jnp.int32) - jnp.repeat(grp_start[tile_expert], tm)
    in_use = rank < jnp.repeat(counts[tile_expert], tm)
    sorted_idx = jnp.clip(jnp.repeat(cnt_start[tile_expert], tm) + rank, 0, 2 * m - 1)
    src_token = jnp.where(in_use, jnp.take(order, sorted_idx, mode="clip") // 2, 0)
    h_sorted = jnp.take(h, src_token, axis=0, mode="clip")
    y_sorted = moe_grouped_ffn(h_sorted, tile_expert, n_tiles, w1, w3, w2, layer)
    ya = jnp.take(y_sorted, pos12[:, 0], axis=0, mode="clip")
    yb = jnp.take(y_sorted, pos12[:, 1], axis=0, mode="clip")
    return moe_combine(x, ya, yb, route, rowmod)


def _dir_split(a, n):
    m = a.shape[0]
    a3 = a.reshape(m, 2, n)
    return jnp.transpose(a3, (1, 0, 2)), jnp.transpose(a3, (1, 2, 0))


def ssd_layer(x, g, rowmod, w_in, conv_w, conv_b, dt_bias, a_log, d_skip, norm_g, w_out, *, bsz, ctx_len):
    m = x.shape[0]
    n_main = SSD_INNER + SSD_INNER + 2 * SSD_GROUPS * SSD_N
    zx, dt_raw = in_projection(x, g, rowmod, [w_in[:, :n_main].astype(BF16), w_in[:, n_main:].astype(BF16)],
                               [BF16, F32], shift_idx=0, scale_idx=1)
    u = dwconv_silu(zx, SSD_INNER, n_main - SSD_INNER, conv_w, conv_b, bsz=bsz, ctx_len=ctx_len)
    dt_dir, dtt_dir = _dir_split(dt_raw, SSD_HEADS)
    yf = ssd_scan(u, dt_dir, dtt_dir, dt_bias, a_log, bsz=bsz, ctx_len=ctx_len, bwd=False)
    yb = ssd_scan(u, dt_dir, dtt_dir, dt_bias, a_log, bsz=bsz, ctx_len=ctx_len, bwd=True)
    tm = 256
    d_x = jnp.repeat(d_skip.astype(F32), SSD_P).reshape(1, SSD_INNER)
    specs = [pl.BlockSpec((tm, SSD_INNER), lambda i: (i, 0)),
             pl.BlockSpec((tm, SSD_INNER), lambda i: (i, 0)),
             pl.BlockSpec((tm, SSD_INNER), lambda i: (i, 0)),
             pl.BlockSpec((tm, SSD_INNER), lambda i: (i, 0)),
             pl.BlockSpec((1, SSD_INNER), lambda i: (0, 0)),
             pl.BlockSpec((1, SSD_INNER), lambda i: (0, 0))]
    return out_projection(_ssd_finish, [yf, yb, u, zx, d_x, norm_g.reshape(1, SSD_INNER)], specs,
                          w_out.astype(BF16), x, rowmod, gate_idx=2, tm=tm)


def hgrn_layer(x, g, rowmod, w_in, lb, norm_g, w_out, *, bsz, ctx_len):
    proj, = in_projection(x, g, rowmod, [w_in.astype(BF16)], [BF16], shift_idx=0, scale_idx=1)
    lb = lb.astype(F32).reshape(2, 1, D_MODEL)
    lbs = (jnp.log(lb), jnp.log1p(-lb), 1.0 - lb)
    of = gla_scan(proj, *lbs, bsz=bsz, ctx_len=ctx_len, bwd=False)
    ob = gla_scan(proj, *lbs, bsz=bsz, ctx_len=ctx_len, bwd=True)
    tm = 256
    specs = [pl.BlockSpec((tm, D_MODEL), lambda i: (i, 0)),
             pl.BlockSpec((tm, D_MODEL), lambda i: (i, 0)),
             pl.BlockSpec((tm, D_MODEL), lambda i: (i, 4)),
             pl.BlockSpec((1, D_MODEL), lambda i: (0, 0))]
    return out_projection(_hgrn_finish, [of, ob, proj, norm_g.reshape(1, D_MODEL)], specs,
                          w_out.astype(BF16), x, rowmod, gate_idx=2, tm=tm)


def _attn_head_perm():
    r_per = ATT_HEADS // ATT_KV
    heads = [(2 * p + j) * r_per + r for p in range(ATT_KV // 2) for r in range(r_per) for j in range(2)]
    return np.concatenate([np.arange(h * ATT_HD, (h + 1) * ATT_HD) for h in heads])


def _rope_tables(seq_len, ctx_len, grid_w):
    rows = seq_len // grid_w
    row = jnp.repeat(jnp.arange(rows, dtype=F32), grid_w)
    col = jnp.tile(jnp.arange(grid_w, dtype=F32), rows)
    inv = ROPE_THETA ** (-jnp.arange(ROPE_FREQS, dtype=F32) / ROPE_FREQS)
    ang_r = row[:, None] * inv
    ang_c = col[:, None] * inv
    ang = jnp.concatenate([ang_r, ang_r, ang_c, ang_c], axis=-1)
    cos = jnp.concatenate([jnp.ones((ctx_len, ATT_HD), F32), jnp.cos(ang)], axis=0)
    sin = jnp.concatenate([jnp.zeros((ctx_len, ATT_HD), F32), jnp.sin(ang)], axis=0)
    return jnp.tile(cos, (1, 2)), jnp.tile(sin, (1, 2))


def _rope_matrices():
    r64 = np.zeros((ATT_HD, ATT_HD), np.float32)
    fq = ROPE_FREQS
    for ax in range(2):
        o = ax * 2 * fq
        for i in range(fq):
            r64[o + fq + i, o + i] = -1.0
            r64[o + i, o + fq + i] = 1.0
    n = ATT_HEADS
    bd = np.kron(np.eye(n, dtype=np.float32), np.ones((ATT_HD, ATT_HD), np.float32))
    rot = np.kron(np.eye(n, dtype=np.float32), r64)
    return jnp.asarray(bd, BF16), jnp.asarray(rot, BF16)


def attn_layer(x, g, rowmod, w_qkv, q_g, k_g, w_o, *, bsz, ctx_len, grid_w):
    m = x.shape[0]
    seq_len = m // bsz - ctx_len
    perm = _attn_head_perm()
    nq = ATT_HEADS * ATT_HD
    w = jnp.concatenate([w_qkv[:, :nq][:, perm], w_qkv[:, nq:]], axis=1).astype(BF16)
    cos_t, sin_t = _rope_tables(seq_len, ctx_len, grid_w)
    bd, rot = _rope_matrices()
    qg = jnp.tile(q_g.astype(F32), ATT_HEADS).reshape(1, nq)
    kg = jnp.tile(k_g.astype(F32), ATT_KV).reshape(1, ATT_KV * ATT_HD)
    q, k, v = attn_qkv(x, g, rowmod, w, cos_t, sin_t, qg, kg, bd, rot, bsz=bsz, shift_idx=0, scale_idx=1)
    o = attention(q, k, v, bsz=bsz, ctx_len=ctx_len)
    tm = 256
    specs = [pl.BlockSpec((tm, nq), lambda i: (i, 0))]
    return out_projection(_identity_pro, [o], specs, w_o[perm, :].astype(BF16), x, rowmod, gate_idx=2, tm=tm)


def mlstm_layer(x, g, rowmod, w_up, conv_w, conv_b, w_q, w_k, w_v, w_gate, b_gate, skip, norm_g, w_down,
                *, bsz, ctx_len):
    up, = in_projection(x, g, rowmod, [w_up.astype(BF16)], [BF16], shift_idx=0, scale_idx=1)
    xc = dwconv_silu(up, 0, ML_INNER, conv_w, conv_b, bsz=bsz, ctx_len=ctx_len)
    q, k, v, gates = mlstm_qkv(xc, up, w_q.astype(BF16), w_k.astype(BF16), w_v.astype(BF16),
                               w_gate.astype(BF16), b_gate)
    gt_dir, gtt_dir = _dir_split(gates, 2 * ML_HEADS)
    hf = mlstm_scan(q, k, v, gt_dir, gtt_dir, bsz=bsz, ctx_len=ctx_len, bwd=False)
    hb = mlstm_scan(q, k, v, gt_dir, gtt_dir, bsz=bsz, ctx_len=ctx_len, bwd=True)
    tm = 256
    specs = [pl.BlockSpec((tm, ML_INNER), lambda i: (i, 0)),
             pl.BlockSpec((tm, ML_INNER), lambda i: (i, 0)),
             pl.BlockSpec((tm, ML_INNER), lambda i: (i, 0)),
             pl.BlockSpec((tm, ML_INNER), lambda i: (i, 1)),
             pl.BlockSpec((1, ML_INNER), lambda i: (0, 0)),
             pl.BlockSpec((1, ML_INNER), lambda i: (0, 0))]
    return out_projection(_mlstm_finish, [hf, hb, xc, up, skip.reshape(1, ML_INNER), norm_g.reshape(1, ML_INNER)],
                          specs, w_down.astype(BF16), x, rowmod, gate_idx=2, tm=tm)


def kernel(x, c, ctx, c_ctx, ada_w, ada_b, norm_g, ssd_w_in, ssd_conv_w, ssd_conv_b, ssd_dt_bias, ssd_a_log, ssd_d, ssd_norm_g, ssd_w_out, hgrn_w_in, hgrn_lb, hgrn_norm_g, hgrn_w_out, attn_w_qkv, attn_q_g, attn_k_g, attn_w_o, mlstm_w_up, mlstm_conv_w, mlstm_conv_b, mlstm_w_q, mlstm_w_k, mlstm_w_v, mlstm_w_gate, mlstm_b_gate, mlstm_skip, mlstm_norm_g, mlstm_w_down, ffn_w1, ffn_w3, ffn_w2, moe_router, moe_w1, moe_w3, moe_w2):
    bsz, seq_len, _ = x.shape
    ctx_len = ctx.shape[1]
    depth = ada_w.shape[0]
    grid_w = 64
    t_all = ctx_len + seq_len
    m = bsz * t_all
    xa = jnp.concatenate([ctx, x], axis=1).reshape(m, D_MODEL)
    c_pad = jnp.zeros((8, D_MODEL), F32).at[:bsz].set(c).at[bsz].set(c_ctx)
    groups_per_batch = t_all // ROW_GROUP
    ctx_groups = ctx_len // ROW_GROUP
    gidx = np.array([bsz if (gi % groups_per_batch) < ctx_groups else gi // groups_per_batch
                     for gi in range(m // ROW_GROUP)], np.int32)
    lb_all = jnp.cumsum(jax.nn.softmax(hgrn_lb.astype(F32), axis=1), axis=1)
    lb_all = lb_all - lb_all[:, :1]
    kw = dict(bsz=bsz, ctx_len=ctx_len)
    ffn_w = [w.astype(BF16) for w in (ffn_w1, ffn_w3, ffn_w2)]
    moe_w = [w.astype(BF16) for w in (moe_w1, moe_w3, moe_w2)]
    for i in range(depth):
        mod = ada_modulation(c_pad, ada_w[i], ada_b[i]).reshape(8, 6, D_MODEL)
        rowmod = mod[gidx]
        kind, j = i % 4, i // 4
        if kind == 0:
            xa = ssd_layer(xa, norm_g[i, 0], rowmod, ssd_w_in[j], ssd_conv_w[j], ssd_conv_b[j], ssd_dt_bias[j],
                           ssd_a_log[j], ssd_d[j], ssd_norm_g[j], ssd_w_out[j], **kw)
        elif kind == 1:
            xa = hgrn_layer(xa, norm_g[i, 0], rowmod, hgrn_w_in[j], lb_all[:, i], hgrn_norm_g[j], hgrn_w_out[j], **kw)
        elif kind == 2:
            xa = attn_layer(xa, norm_g[i, 0], rowmod, attn_w_qkv[j], attn_q_g[j], attn_k_g[j], attn_w_o[j],
                            grid_w=grid_w, **kw)
        else:
            xa = mlstm_layer(xa, norm_g[i, 0], rowmod, mlstm_w_up[j], mlstm_conv_w[j], mlstm_conv_b[j], mlstm_w_q[j],
                             mlstm_w_k[j], mlstm_w_v[j], mlstm_w_gate[j], mlstm_b_gate[j], mlstm_skip[j],
                             mlstm_norm_g[j], mlstm_w_down[j], **kw)
        if i % 2 == 0:
            xa = dense_ffn(xa, norm_g[i, 1], rowmod, *ffn_w, i // 2)
        else:
            xa = moe_ffn(xa, norm_g[i, 1], rowmod, moe_router[i // 2], *moe_w, i // 2)
    return xa.reshape(bsz, t_all, D_MODEL)[:, ctx_len:]
```

```python
import functools
import math

import jax
import jax.numpy as jnp
import numpy as np
from jax import lax
from jax.experimental import pallas as pl
from jax.experimental.pallas import tpu as pltpu

F32 = jnp.float32
BF16 = jnp.bfloat16
HI = lax.Precision.HIGHEST

D_MODEL = 1024
EPS = 1e-6
ROW_GROUP = 256
PROJ_TM = 512
CONV_W = 5
NEG_BIG = -1e30
VMEM_LIMIT = 56 << 20

SSD_INNER = 2 * D_MODEL
SSD_P = 64
SSD_HEADS = SSD_INNER // SSD_P
SSD_N = 128
SSD_GROUPS = 8
SSD_GW = SSD_INNER // SSD_GROUPS
SCAN_L = 128

HG_HEADS = 8
HG_DK = 128
HG_SUB = 8

ATT_HEADS = 16
ATT_KV = 4
ATT_HD = 64
ROPE_THETA = 10000.0
ROPE_FREQS = ATT_HD // 4
ATT_TQ = 128
ATT_SHIFT_MAX = 60.0

ML_INNER = 2 * D_MODEL
ML_HEADS = 4
ML_DH = ML_INNER // ML_HEADS
ML_L = 256

D_FF = 7 * D_MODEL // 2
N_EXPERTS = 8
MOE_TM = 512


def _cparams(sem):
    return pltpu.CompilerParams(dimension_semantics=sem, vmem_limit_bytes=VMEM_LIMIT)


def _dot(a, b):
    return jnp.dot(a, b, preferred_element_type=F32)


def _dot_hi(a, b):
    return jnp.dot(a, b, preferred_element_type=F32, precision=HI)


def _dot_nt(a, b):
    return lax.dot_general(a, b, (((1,), (1,)), ((), ())), preferred_element_type=F32)


def _dot_tn(a, b):
    return lax.dot_general(a, b, (((0,), (0,)), ((), ())), preferred_element_type=F32)


def _silu(x):
    return x * jax.nn.sigmoid(x)


def _log_sigmoid(x):
    return jnp.minimum(x, 0.0) - jnp.log1p(jnp.exp(-jnp.abs(x)))


def _softplus(x):
    return jnp.maximum(x, 0.0) + jnp.log1p(jnp.exp(-jnp.abs(x)))


def _pick_tile(m, pref):
    t = pref
    while m % t:
        t //= 2
    return t


def _rows_scale_shift(y, mod_ref, scale_idx, shift_idx):
    parts = []
    for gi in range(y.shape[0] // ROW_GROUP):
        sl = y[gi * ROW_GROUP:(gi + 1) * ROW_GROUP]
        parts.append(sl * (1.0 + mod_ref[gi, scale_idx:scale_idx + 1, :]) + mod_ref[gi, shift_idx:shift_idx + 1, :])
    return parts[0] if len(parts) == 1 else jnp.concatenate(parts, axis=0)


def _rows_gate_residual(x, acc, mod_ref, gate_idx):
    parts = []
    for gi in range(x.shape[0] // ROW_GROUP):
        sl = slice(gi * ROW_GROUP, (gi + 1) * ROW_GROUP)
        parts.append(x[sl] + mod_ref[gi, gate_idx:gate_idx + 1, :] * acc[sl])
    return parts[0] if len(parts) == 1 else jnp.concatenate(parts, axis=0)


def _norm_mod(x, g, mod_ref, shift_idx, scale_idx):
    y = x * lax.rsqrt(jnp.mean(x * x, axis=-1, keepdims=True) + EPS) * g
    return _rows_scale_shift(y, mod_ref, scale_idx, shift_idx)


def _split2(a):
    hi = a.astype(BF16)
    return hi, (a - hi.astype(F32)).astype(BF16)


def _dot_sel_l(sel, a):
    hi, lo = _split2(a)
    return _dot(sel, hi) + _dot(sel, lo)


def _dot_sel_r(a, sel):
    hi, lo = _split2(a)
    return _dot(hi, sel) + _dot(lo, sel)


def _scan_masks(n, bwd):
    ii = lax.broadcasted_iota(jnp.int32, (n, n), 0)
    jj = lax.broadcasted_iota(jnp.int32, (n, n), 1)
    visible = (ii <= jj) if bwd else (ii >= jj)
    visible_t = (ii >= jj) if bwd else (ii <= jj)
    tri = jnp.where(visible, 1.0, 0.0).astype(BF16)
    trit = jnp.where(visible_t, 1.0, 0.0).astype(BF16)
    return ii, jj, visible, tri, trit


def _ada_kernel(c_ref, w_ref, b_ref, o_ref):
    c = c_ref[...]
    o_ref[...] = _dot_hi(_silu(c), w_ref[...]) + b_ref[...]


def ada_modulation(c_pad, w, b):
    n = w.shape[1]
    tn = 1024
    return pl.pallas_call(
        _ada_kernel,
        grid=(n // tn,),
        in_specs=[pl.BlockSpec(c_pad.shape, lambda j: (0, 0)),
                  pl.BlockSpec((D_MODEL, tn), lambda j: (0, j)),
                  pl.BlockSpec((1, tn), lambda j: (0, j))],
        out_specs=pl.BlockSpec((c_pad.shape[0], tn), lambda j: (0, j)),
        out_shape=jax.ShapeDtypeStruct((c_pad.shape[0], n), F32),
        compiler_params=_cparams(("arbitrary",)),
        name="ada_modulation",
    )(c_pad, w, b.reshape(1, n))


def _inproj_kernel(x_ref, g_ref, mod_ref, *refs, n_w, shift_idx, scale_idx, tn):
    h = _norm_mod(x_ref[...], g_ref[...], mod_ref, shift_idx, scale_idx).astype(BF16)
    for w_ref, o_ref in zip(refs[:n_w], refs[n_w:]):
        n = o_ref.shape[1]
        step = min(tn, n)
        for j in range(n // step):
            o_ref[:, j * step:(j + 1) * step] = _dot(h, w_ref[:, j * step:(j + 1) * step]).astype(o_ref.dtype)


def in_projection(x, g, rowmod, ws, out_dtypes, *, shift_idx, scale_idx, tm=PROJ_TM, tn=512):
    m = x.shape[0]
    tm = _pick_tile(m, tm)
    gm = tm // ROW_GROUP
    in_specs = [pl.BlockSpec((tm, D_MODEL), lambda i: (i, 0)),
                pl.BlockSpec((1, D_MODEL), lambda i: (0, 0)),
                pl.BlockSpec((gm, 6, D_MODEL), lambda i: (i, 0, 0))]
    in_specs += [pl.BlockSpec(w.shape, lambda i: (0, 0)) for w in ws]
    out_specs = [pl.BlockSpec((tm, w.shape[1]), lambda i: (i, 0)) for w in ws]
    out_shape = [jax.ShapeDtypeStruct((m, w.shape[1]), dt) for w, dt in zip(ws, out_dtypes)]
    return pl.pallas_call(
        functools.partial(_inproj_kernel, n_w=len(ws), shift_idx=shift_idx, scale_idx=scale_idx, tn=tn),
        grid=(m // tm,),
        in_specs=in_specs, out_specs=out_specs, out_shape=out_shape,
        compiler_params=_cparams(("parallel",)),
        name="in_projection",
    )(x, g.reshape(1, D_MODEL), rowmod, *ws)


def _outproj_kernel(*refs, n_pro, pro_fn, gate_idx):
    pro_refs = refs[:n_pro]
    w_ref, x_ref, mod_ref, o_ref = refs[n_pro:]
    a = pro_fn(*pro_refs).astype(BF16)
    acc = _dot(a, w_ref[...])
    o_ref[...] = _rows_gate_residual(x_ref[...], acc, mod_ref, gate_idx)


def out_projection(pro_fn, pro_args, pro_specs, w, x, rowmod, *, gate_idx, tm):
    m = x.shape[0]
    gm = tm // ROW_GROUP
    in_specs = list(pro_specs) + [pl.BlockSpec(w.shape, lambda i: (0, 0)),
                                  pl.BlockSpec((tm, D_MODEL), lambda i: (i, 0)),
                                  pl.BlockSpec((gm, 6, D_MODEL), lambda i: (i, 0, 0))]
    return pl.pallas_call(
        functools.partial(_outproj_kernel, n_pro=len(pro_args), pro_fn=pro_fn, gate_idx=gate_idx),
        grid=(m // tm,),
        in_specs=in_specs,
        out_specs=pl.BlockSpec((tm, D_MODEL), lambda i: (i, 0)),
        out_shape=jax.ShapeDtypeStruct((m, D_MODEL), F32),
        compiler_params=_cparams(("parallel",)),
        name="out_projection",
    )(*pro_args, w, x, rowmod)


def _conv_kernel(u_ref, w_ref, b_ref, o_ref, *, ctx_len):
    x = u_ref[...].astype(F32)
    t_all = x.shape[0]
    t = lax.broadcasted_iota(jnp.int32, (t_all, 1), 0)
    in_ctx = t < ctx_len
    pad = CONV_W // 2
    acc = b_ref[...] + w_ref[pad:pad + 1, :] * x
    for off in range(-pad, pad + 1):
        if off == 0:
            continue
        xs = pltpu.roll(x, (-off) % t_all, 0)
        tt = t + off
        valid = (tt >= 0) & (tt < t_all) & ((tt < ctx_len) == in_ctx)
        acc = acc + w_ref[pad + off:pad + off + 1, :] * jnp.where(valid, xs, 0.0)
    o_ref[...] = _silu(acc).astype(o_ref.dtype)


def dwconv_silu(u, col_off, width, w, b, *, bsz, ctx_len, tc=256):
    m = u.shape[0]
    t_all = m // bsz
    cb = col_off // tc
    return pl.pallas_call(
        functools.partial(_conv_kernel, ctx_len=ctx_len),
        grid=(bsz, width // tc),
        in_specs=[pl.BlockSpec((t_all, tc), lambda bi, j: (bi, j + cb)),
                  pl.BlockSpec((CONV_W, tc), lambda bi, j: (0, j)),
                  pl.BlockSpec((1, tc), lambda bi, j: (0, j))],
        out_specs=pl.BlockSpec((t_all, tc), lambda bi, j: (bi, j)),
        out_shape=jax.ShapeDtypeStruct((m, width), BF16),
        compiler_params=_cparams(("parallel", "parallel")),
        name="dwconv_silu",
    )(u, w, b.reshape(1, width))


def _scan_chunk_index(bwd, c, n_ctx_chunks, n_chunks):
    if not bwd:
        return c
    return jnp.where(c < n_ctx_chunks, n_ctx_chunks - 1 - c, n_chunks - 1 + n_ctx_chunks - c)


def _ssd_scan_kernel(x_ref, b_ref, c_ref, dt_ref, dtt_ref, bias_ref, biast_ref, alog_ref, alogt_ref, exp_ref,
                     y_ref, state, *, bwd):
    ci = pl.program_id(1)
    L = SCAN_L
    H = SSD_HEADS

    @pl.when(ci == 0)
    def _():
        state[...] = jnp.zeros_like(state)

    _, _, visible, tri, trit = _scan_masks(L, bwd)
    dt = _softplus(dt_ref[...] + bias_ref[...])
    dtt = _softplus(dtt_ref[...] + biast_ref[...])
    a_neg = -jnp.exp(alog_ref[...])
    a_negt = -jnp.exp(alogt_ref[...])
    dta = dt * a_neg
    a_col = _dot_sel_l(tri, dta)
    a_row = _dot_sel_r(dtt * a_negt, trit)
    a_end = jnp.sum(dta, axis=0, keepdims=True)
    per_head = jnp.concatenate([jnp.exp(a_col), jnp.exp(a_end - a_col) * dt,
                                jnp.broadcast_to(jnp.exp(a_end), (16, H))], axis=0)
    per_chan = _dot_sel_r(per_head, exp_ref[...])
    ea_x, w_x, eend_x = per_chan[:L], per_chan[L:2 * L], per_chan[2 * L:2 * L + 1]
    lane = lax.broadcasted_iota(jnp.int32, (L, 2 * SSD_P), 1)
    for g in range(SSD_GROUPS):
        gs = slice(g * SSD_GW, (g + 1) * SSD_GW)
        ns = slice(g * SSD_N, (g + 1) * SSD_N)
        bg = b_ref[:, ns]
        cg = c_ref[:, ns]
        xgb = x_ref[:, gs]
        cb = _dot_nt(cg, bg)
        sg = state[:, gs]
        y_inter = _dot(cg, sg.astype(BF16)) * ea_x[:, gs]
        pieces = []
        for pr in range(SSD_GW // (2 * SSD_P)):
            ys = []
            for q in range(2):
                h = g * (SSD_GW // SSD_P) + 2 * pr + q
                rel = a_col[:, h:h + 1] - a_row[h:h + 1, :]
                gm = cb * jnp.exp(jnp.where(visible, rel, NEG_BIG)) * dtt[h:h + 1, :]
                ys.append(_dot(gm.astype(BF16), xgb[:, pr * 2 * SSD_P:(pr + 1) * 2 * SSD_P]))
            pieces.append(jnp.where(lane < SSD_P, ys[0], ys[1]))
        y = jnp.concatenate(pieces, axis=1) + y_inter
        y_ref[:, gs] = y.astype(y_ref.dtype)
        xw = (xgb.astype(F32) * w_x[:, gs]).astype(BF16)
        state[:, gs] = sg * eend_x[:, gs] + _dot_tn(bg, xw)


def ssd_scan(u, dt_dir, dtt_dir, dt_bias, a_log, *, bsz, ctx_len, bwd):
    m = u.shape[0]
    t_all = m // bsz
    nc = t_all // SCAN_L
    ncc = ctx_len // SCAN_L
    L, H = SCAN_L, SSD_HEADS
    d = int(bwd)

    def row(bi, c):
        return bi * nc + _scan_chunk_index(bwd, c, ncc, nc)

    bn = SSD_INNER // (SSD_GROUPS * SSD_N)
    expand = jnp.asarray(np.kron(np.eye(H, dtype=np.float32), np.ones((1, SSD_P), np.float32)), BF16)
    return pl.pallas_call(
        functools.partial(_ssd_scan_kernel, bwd=bwd),
        grid=(bsz, nc),
        in_specs=[pl.BlockSpec((L, SSD_INNER), lambda bi, c: (row(bi, c), 0)),
                  pl.BlockSpec((L, SSD_GROUPS * SSD_N), lambda bi, c: (row(bi, c), bn)),
                  pl.BlockSpec((L, SSD_GROUPS * SSD_N), lambda bi, c: (row(bi, c), bn + 1)),
                  pl.BlockSpec((None, L, H), lambda bi, c: (d, row(bi, c), 0)),
                  pl.BlockSpec((None, H, L), lambda bi, c: (d, 0, row(bi, c))),
                  pl.BlockSpec((None, 1, H), lambda bi, c: (d, 0, 0)),
                  pl.BlockSpec((None, H, 1), lambda bi, c: (d, 0, 0)),
                  pl.BlockSpec((None, 1, H), lambda bi, c: (d, 0, 0)),
                  pl.BlockSpec((None, H, 1), lambda bi, c: (d, 0, 0)),
                  pl.BlockSpec((H, SSD_INNER), lambda bi, c: (0, 0))],
        out_specs=pl.BlockSpec((L, SSD_INNER), lambda bi, c: (row(bi, c), 0)),
        out_shape=jax.ShapeDtypeStruct((m, SSD_INNER), BF16),
        scratch_shapes=[pltpu.VMEM((SSD_N, SSD_INNER), F32)],
        compiler_params=_cparams(("arbitrary", "arbitrary")),
        name="ssd_scan_bwd" if bwd else "ssd_scan_fwd",
    )(u, u, u, dt_dir, dtt_dir, dt_bias.reshape(2, 1, H), dt_bias.reshape(2, H, 1),
      a_log.reshape(2, 1, H), a_log.reshape(2, H, 1), expand)


def _ssd_finish(yf_ref, yb_ref, xs_ref, z_ref, d_ref, g_ref):
    y = yf_ref[...].astype(F32) + yb_ref[...].astype(F32) + d_ref[...] * xs_ref[...].astype(F32)
    y = y * _silu(z_ref[...].astype(F32))
    parts = []
    for g in range(SSD_GROUPS):
        sl = y[:, g * SSD_GW:(g + 1) * SSD_GW]
        parts.append(sl * lax.rsqrt(jnp.mean(sl * sl, axis=-1, keepdims=True) + EPS))
    return jnp.concatenate(parts, axis=1) * g_ref[...]


def _gla_scan_kernel(q_ref, f_ref, v_ref, llb_ref, l1m_ref, omlb_ref, place_ref, o_ref, state, *, bwd):
    ci = pl.program_id(1)
    L, C = SCAN_L, HG_SUB
    nb = L // C
    dk = HG_DK

    @pl.when(ci == 0)
    def _():
        state[...] = jnp.zeros_like(state)

    ii, jj, visible, tri, _ = _scan_masks(L, bwd)
    band_mask = ((ii // C) == (jj // C)) & visible
    off_mask = ((ii // C) < (jj // C)) if bwd else ((ii // C) > (jj // C))
    row_sub = lax.broadcasted_iota(jnp.int32, (L, 1), 0) // C
    last = 0 if bwd else L - 1
    blocks = range(1, nb) if bwd else range(nb - 1)

    def block_end(j):
        return j * C if bwd else (j + 1) * C - 1

    def head(h, carry):
        ls = slice(h * dk, (h + 1) * dk)
        q = _silu(q_ref[:, ls].astype(F32)) * (dk ** -0.5)
        f = f_ref[:, ls].astype(F32)
        v = v_ref[:, ls]
        a = llb_ref[:, ls]
        b = l1m_ref[:, ls] + _log_sigmoid(f)
        lf = jnp.maximum(a, b) + jnp.log1p(jnp.exp(-jnp.abs(a - b)))
        k = omlb_ref[:, ls] * jax.nn.sigmoid(-f)
        cum = _dot_sel_l(tri, lf)
        c_end = cum[last:last + 1, :]
        es = []
        for dl in range(C):
            if dl == 0:
                e = q * k
            else:
                sh = (L - dl) if bwd else dl
                ks = pltpu.roll(k, sh, 0)
                cs = pltpu.roll(cum, sh, 0)
                e = q * ks * jnp.exp(jnp.minimum(cum - cs, 0.0))
            es.append(e.astype(BF16))
        band = _dot(jnp.concatenate(es, axis=1), place_ref[...])
        band = pltpu.roll(band, 0, 1, stride=1, stride_axis=0)
        kparts = []
        for j in range(nb):
            rs = slice(j * C, (j + 1) * C)
            e_j = cum[block_end(j):block_end(j) + 1, :]
            kparts.append(k[rs] * jnp.exp(e_j - cum[rs]))
        kt = jnp.concatenate(kparts, axis=0)
        kcat = jnp.concatenate([jnp.where(row_sub == j, kt, 0.0).astype(BF16) for j in blocks], axis=1)
        qparts = []
        for j in blocks:
            e_j = cum[block_end(j):block_end(j) + 1, :]
            rows = slice(0, j * C) if bwd else slice((j + 1) * C, L)
            qt = q[rows] * jnp.exp(cum[rows] - e_j)
            pad = jnp.zeros((L - qt.shape[0], dk), F32)
            qparts.append(jnp.concatenate([qt, pad] if bwd else [pad, qt], axis=0).astype(BF16))
        off = _dot_nt(jnp.concatenate(qparts, axis=1), kcat)
        att = jnp.where(band_mask, band, jnp.where(off_mask, off, 0.0))
        st = state[h]
        o = _dot(att.astype(BF16), v) + _dot_nt((q * jnp.exp(cum)).astype(BF16), st.astype(BF16))
        o_ref[:, ls] = o.astype(o_ref.dtype)
        kw = (k * jnp.exp(c_end - cum)).astype(BF16)
        state[h] = st * jnp.exp(c_end) + _dot_tn(v, kw)
        return carry

    for h in range(HG_HEADS):
        head(h, 0)


def _gla_place_matrix(bwd):
    pm = np.zeros((HG_SUB * HG_DK, SCAN_L), np.float32)
    for dl in range(HG_SUB):
        pm[dl * HG_DK:(dl + 1) * HG_DK, dl if bwd else (SCAN_L - dl) % SCAN_L] = 1.0
    return jnp.asarray(pm, BF16)


def gla_scan(proj, llb, l1m, omlb, *, bsz, ctx_len, bwd):
    m = proj.shape[0]
    t_all = m // bsz
    nc = t_all // SCAN_L
    ncc = ctx_len // SCAN_L
    L = SCAN_L
    d = int(bwd)

    def row(bi, c):
        return bi * nc + _scan_chunk_index(bwd, c, ncc, nc)

    vec = pl.BlockSpec((None, 1, D_MODEL), lambda bi, c: (d, 0, 0))
    place = _gla_place_matrix(bwd)
    return pl.pallas_call(
        functools.partial(_gla_scan_kernel, bwd=bwd),
        grid=(bsz, nc),
        in_specs=[pl.BlockSpec((L, D_MODEL), lambda bi, c: (row(bi, c), 0)),
                  pl.BlockSpec((L, D_MODEL), lambda bi, c: (row(bi, c), 1 + d)),
                  pl.BlockSpec((L, D_MODEL), lambda bi, c: (row(bi, c), 3)),
                  vec, vec, vec,
                  pl.BlockSpec(place.shape, lambda bi, c: (0, 0))],
        out_specs=pl.BlockSpec((L, D_MODEL), lambda bi, c: (row(bi, c), 0)),
        out_shape=jax.ShapeDtypeStruct((m, D_MODEL), BF16),
        scratch_shapes=[pltpu.VMEM((HG_HEADS, HG_DK, HG_DK), F32)],
        compiler_params=_cparams(("arbitrary", "arbitrary")),
        name="gla_scan_bwd" if bwd else "gla_scan_fwd",
    )(proj, proj, proj, llb, l1m, omlb, place)


def _hgrn_finish(of_ref, ob_ref, gate_ref, g_ref):
    o = of_ref[...].astype(F32) + ob_ref[...].astype(F32)
    parts = []
    for h in range(HG_HEADS):
        sl = o[:, h * HG_DK:(h + 1) * HG_DK]
        parts.append(sl * lax.rsqrt(jnp.mean(sl * sl, axis=-1, keepdims=True) + EPS))
    return jnp.concatenate(parts, axis=1) * g_ref[...] * _silu(gate_ref[...].astype(F32))


def _attn_qkv_kernel(x_ref, g_ref, mod_ref, w_ref, cos_ref, sin_ref, qg_ref, kg_ref, bd_ref, rot_ref,
                     q_ref, k_ref, v_ref, *, shift_idx, scale_idx):
    h = _norm_mod(x_ref[...], g_ref[...], mod_ref, shift_idx, scale_idx).astype(BF16)
    nq = ATT_HEADS * ATT_HD
    nk = ATT_KV * ATT_HD

    def norm_rope(a, gain, width, out_scale):
        ss = _dot((a * a).astype(BF16), bd_ref[:width, :width])
        an = a * lax.rsqrt(ss * (1.0 / ATT_HD) + EPS) * gain
        rot = _dot(an.astype(BF16), rot_ref[:width, :width])
        cos = jnp.concatenate([cos_ref[...]] * (width // 128), axis=1)
        sin = jnp.concatenate([sin_ref[...]] * (width // 128), axis=1)
        return (an * cos + rot * sin) * out_scale

    q = _dot(h, w_ref[:, :nq])
    q_ref[...] = norm_rope(q, qg_ref[...], nq, ATT_HD ** -0.5 * math.log2(math.e)).astype(q_ref.dtype)
    k = _dot(h, w_ref[:, nq:nq + nk])
    k_ref[...] = norm_rope(k, kg_ref[...], nk, 1.0).astype(k_ref.dtype)
    v_ref[...] = _dot(h, w_ref[:, nq + nk:]).astype(v_ref.dtype)


def attn_qkv(x, g, rowmod, w, cos_t, sin_t, q_g, k_g, bd, rot, *, bsz, shift_idx, scale_idx):
    m = x.shape[0]
    tm = ROW_GROUP
    tpb = (m // bsz) // tm
    nq = ATT_HEADS * ATT_HD
    nk = ATT_KV * ATT_HD
    const = lambda shape: pl.BlockSpec(shape, lambda i: (0, 0))
    return pl.pallas_call(
        functools.partial(_attn_qkv_kernel, shift_idx=shift_idx, scale_idx=scale_idx),
        grid=(m // tm,),
        in_specs=[pl.BlockSpec((tm, D_MODEL), lambda i: (i, 0)),
                  const((1, D_MODEL)),
                  pl.BlockSpec((1, 6, D_MODEL), lambda i: (i, 0, 0)),
                  const(w.shape),
                  pl.BlockSpec((tm, 128), lambda i: (i % tpb, 0)),
                  pl.BlockSpec((tm, 128), lambda i: (i % tpb, 0)),
                  const((1, nq)), const((1, nk)), const(bd.shape), const(rot.shape)],
        out_specs=[pl.BlockSpec((tm, nq), lambda i: (i, 0)),
                   pl.BlockSpec((tm, nk), lambda i: (i, 0)),
                   pl.BlockSpec((tm, nk), lambda i: (i, 0))],
        out_shape=[jax.ShapeDtypeStruct((m, nq), BF16),
                   jax.ShapeDtypeStruct((m, nk), BF16),
                   jax.ShapeDtypeStruct((m, nk), BF16)],
        compiler_params=_cparams(("parallel",)),
        name="attn_qkv",
    )(x, g.reshape(1, D_MODEL), rowmod, w, cos_t, sin_t, q_g, k_g, bd, rot)


def _attn_kernel(q_ref, k_ref, v_ref, o_ref, kaug, vaug, kmax2, *, ctx_len):
    qt = pl.program_id(2)
    tq = q_ref.shape[0]
    hd = ATT_HD
    lane = lax.broadcasted_iota(jnp.int32, (1, 2 * hd), 1)
    n_r = q_ref.shape[1] // (2 * hd)
    aug_lane = (hd, 0)
    ii = lax.broadcasted_iota(jnp.int32, (2 * hd, 2 * hd), 0)
    jj = lax.broadcasted_iota(jnp.int32, (2 * hd, 2 * hd), 1)
    half_sum = jnp.where((ii // hd) == (jj // hd), 1.0, 0.0).astype(BF16)

    @pl.when(qt == 0)
    def _():
        v = v_ref[...]
        k = k_ref[...]
        one = jnp.ones((), BF16)
        vaug[0] = jnp.where(lane < hd, v, one)
        vaug[1] = jnp.where(lane < hd, one, v)
        kaug[0] = jnp.where(lane == aug_lane[0], one, k)
        kaug[1] = jnp.where(lane == aug_lane[1], one, k)
        kf = k.astype(F32)
        kn2 = _dot((kf * kf).astype(BF16), half_sum)
        kmax2[...] = jnp.broadcast_to(jnp.max(kn2, axis=0, keepdims=True), kmax2.shape)

    def queries(j):
        sel = (lane // hd) == j
        return jnp.concatenate(
            [jnp.where(sel, q_ref[:, r * 128:(r + 1) * 128], jnp.zeros((), BF16)) for r in range(n_r)], axis=0)

    def finish(outs):
        o = jnp.where(lane < hd, outs[0], outs[1])
        for r in range(n_r):
            o_ref[:, r * 128:(r + 1) * 128] = o[r * tq:(r + 1) * tq].astype(o_ref.dtype)

    def attend_exact(n_keys):
        outs = []
        for j in range(2):
            s = _dot_nt(queries(j), k_ref[:n_keys, :])
            mx = jnp.max(s, axis=-1, keepdims=True)
            p = jnp.exp2(s - mx).astype(BF16)
            ov = _dot(p, vaug[j, :n_keys, :])
            outs.append(ov / pltpu.roll(ov, hd, 1))
        finish(outs)

    def attend_shifted(n_keys, shifts):
        outs = []
        for j in range(2):
            qa = jnp.where(lane == aug_lane[j], (-shifts[j]).astype(BF16), queries(j))
            p = jnp.exp2(_dot_nt(qa, kaug[j, :n_keys, :])).astype(BF16)
            ov = _dot(p, vaug[j, :n_keys, :])
            outs.append(ov / pltpu.roll(ov, hd, 1))
        finish(outs)

    def attend(n_keys):
        shifts = []
        ones = jnp.ones((2 * hd, 2 * hd), BF16)
        for j in range(2):
            qf = queries(j).astype(F32)
            qn2 = _dot((qf * qf).astype(BF16), ones)
            shifts.append(jnp.sqrt(qn2 * kmax2[0:1, j * hd:j * hd + 1]) * 1.02)
        worst = jnp.max(jnp.maximum(shifts[0], shifts[1]))
        small = worst < ATT_SHIFT_MAX

        @pl.when(small)
        def _():
            attend_shifted(n_keys, shifts)

        @pl.when(jnp.logical_not(small))
        def _():
            attend_exact(n_keys)

    n_ctx_tiles = ctx_len // tq

    @pl.when(qt < n_ctx_tiles)
    def _():
        attend(ctx_len)

    @pl.when(qt >= n_ctx_tiles)
    def _():
        attend(k_ref.shape[0])


def attention(q, k, v, *, bsz, ctx_len):
    m = q.shape[0]
    t_all = m // bsz
    tq = ATT_TQ
    nqt = t_all // tq
    n_pairs = ATT_KV // 2
    qw = q.shape[1] // n_pairs
    return pl.pallas_call(
        functools.partial(_attn_kernel, ctx_len=ctx_len),
        grid=(bsz, n_pairs, nqt),
        in_specs=[pl.BlockSpec((tq, qw), lambda bi, p, t: (bi * nqt + t, p)),
                  pl.BlockSpec((t_all, 2 * ATT_HD), lambda bi, p, t: (bi, p)),
                  pl.BlockSpec((t_all, 2 * ATT_HD), lambda bi, p, t: (bi, p))],
        out_specs=pl.BlockSpec((tq, qw), lambda bi, p, t: (bi * nqt + t, p)),
        out_shape=jax.ShapeDtypeStruct(q.shape, BF16),
        scratch_shapes=[pltpu.VMEM((2, t_all, 2 * ATT_HD), BF16),
                        pltpu.VMEM((2, t_all, 2 * ATT_HD), BF16),
                        pltpu.VMEM((8, 2 * ATT_HD), F32)],
        compiler_params=_cparams(("arbitrary", "arbitrary", "arbitrary")),
        name="attention",
    )(q, k, v)


def _identity_pro(a_ref):
    return a_ref[...]


def _mlstm_qkv_kernel(xc_ref, xm_ref, wq_ref, wk_ref, wv_ref, wg_ref, bg_ref, q_ref, k_ref, v_ref, gate_ref):
    gates = jnp.zeros(gate_ref.shape, F32) + bg_ref[...]
    for which, (src, w_ref, o_ref) in enumerate(((xc_ref, wq_ref, q_ref), (xc_ref, wk_ref, k_ref),
                                                 (xm_ref, wv_ref, v_ref))):
        for h in range(ML_HEADS):
            hs = slice(h * ML_DH, (h + 1) * ML_DH)
            r = _dot(src[:, hs], w_ref[h]).astype(BF16)
            o_ref[:, hs] = r
            gates = gates + _dot(r, wg_ref[which * ML_INNER + h * ML_DH:which * ML_INNER + (h + 1) * ML_DH, :])
    gate_ref[...] = gates


def mlstm_qkv(xc, up, wq, wk, wv, wg, bg, *, tm=PROJ_TM):
    m = xc.shape[0]
    ng = wg.shape[1]
    const = lambda shape: pl.BlockSpec(shape, lambda i: (0,) * len(shape))
    row = pl.BlockSpec((tm, ML_INNER), lambda i: (i, 0))
    return pl.pallas_call(
        _mlstm_qkv_kernel,
        grid=(m // tm,),
        in_specs=[row, row, const(wq.shape), const(wk.shape), const(wv.shape), const(wg.shape), const((1, ng))],
        out_specs=[row, row, row, pl.BlockSpec((tm, ng), lambda i: (i, 0))],
        out_shape=[jax.ShapeDtypeStruct((m, ML_INNER), BF16)] * 3 + [jax.ShapeDtypeStruct((m, ng), F32)],
        compiler_params=_cparams(("parallel",)),
        name="mlstm_qkv",
    )(xc, up, wq, wk, wv, wg, bg.reshape(1, ng))


def _mlstm_scan_kernel(q_ref, k_ref, v_ref, gt_ref, gtt_ref, h_ref, c_st, n_st, m_st, *, bwd):
    ci = pl.program_id(1)
    L = ML_L
    nh = ML_HEADS
    scale = ML_DH ** -0.5

    @pl.when(ci == 0)
    def _():
        c_st[...] = jnp.zeros_like(c_st)
        n_st[...] = jnp.zeros_like(n_st)
        m_st[...] = jnp.full(m_st.shape, NEG_BIG, F32)

    _, _, visible, tri, trit = _scan_masks(L, bwd)
    gt = gt_ref[...]
    gtt = gtt_ref[...]
    li_c, li_r = gt[:, :nh], gtt[:nh, :]
    lf_c, lf_r = _log_sigmoid(gt[:, nh:]), _log_sigmoid(gtt[nh:, :])
    cum_c = _dot_sel_l(tri, lf_c)
    cum_r = _dot_sel_r(lf_r, trit)
    end_c = jnp.sum(lf_c, axis=0, keepdims=True)
    for h in range(nh):
        hs = slice(h * ML_DH, (h + 1) * ML_DH)
        q = q_ref[:, hs]
        kb = k_ref[:, hs]
        k = kb.astype(F32)
        v = v_ref[:, hs]
        m_prev = m_st[h:h + 1, 0:1]
        cum_end = end_c[:, h:h + 1]
        dmat = jnp.where(visible, cum_c[:, h:h + 1] - cum_r[h:h + 1, :] + li_r[h:h + 1, :], -jnp.inf)
        inter = cum_c[:, h:h + 1] + m_prev
        m_t = jnp.maximum(inter, jnp.max(dmat, axis=1, keepdims=True))
        w = jnp.exp(dmat - m_t)
        w_c = jnp.exp(inter - m_t)
        qk = _dot_nt(q, kb) * scale * w
        cmat = c_st[h]
        num = _dot(qk.astype(BF16), v) + w_c * _dot(q, cmat.astype(BF16))
        qn = jnp.sum(q.astype(F32) * n_st[h:h + 1, :], axis=1, keepdims=True)
        den = jnp.sum(qk, axis=1, keepdims=True) + w_c * qn
        hv = num / jnp.maximum(jnp.abs(den), jnp.exp(-m_t))
        h_ref[:, hs] = hv.astype(h_ref.dtype)
        wend_c = cum_end - cum_c[:, h:h + 1] + li_c[:, h:h + 1]
        wend_r = cum_end - cum_r[h:h + 1, :] + li_r[h:h + 1, :]
        m_new = jnp.maximum(cum_end + m_prev, jnp.max(wend_r, axis=1, keepdims=True))
        a_old = jnp.exp(cum_end + m_prev - m_new)
        e_c = jnp.exp(wend_c - m_new) * scale
        e_r = jnp.exp(wend_r - m_new) * scale
        c_st[h] = a_old * cmat + _dot_tn((k * e_c).astype(BF16), v)
        e_r8 = jnp.broadcast_to(e_r, (8, L)).astype(BF16)
        n_st[h:h + 1, :] = a_old * n_st[h:h + 1, :] + _dot(e_r8, kb)[0:1, :]
        m_st[h:h + 1, :] = jnp.broadcast_to(m_new, (1, m_st.shape[1]))


def mlstm_scan(q, k, v, gt_dir, gtt_dir, *, bsz, ctx_len, bwd):
    m = q.shape[0]
    t_all = m // bsz
    nc = t_all // ML_L
    ncc = ctx_len // ML_L
    L, nh = ML_L, ML_HEADS
    d = int(bwd)

    def row(bi, c):
        return bi * nc + _scan_chunk_index(bwd, c, ncc, nc)

    blk = pl.BlockSpec((L, ML_INNER), lambda bi, c: (row(bi, c), 0))
    return pl.pallas_call(
        functools.partial(_mlstm_scan_kernel, bwd=bwd),
        grid=(bsz, nc),
        in_specs=[blk, blk, blk,
                  pl.BlockSpec((None, L, 2 * nh), lambda bi, c: (d, row(bi, c), 0)),
                  pl.BlockSpec((None, 2 * nh, L), lambda bi, c: (d, 0, row(bi, c)))],
        out_specs=pl.BlockSpec((L, ML_INNER), lambda bi, c: (row(bi, c), 0)),
        out_shape=jax.ShapeDtypeStruct((m, ML_INNER), BF16),
        scratch_shapes=[pltpu.VMEM((nh, ML_DH, ML_DH), F32),
                        pltpu.VMEM((8, ML_DH), F32),
                        pltpu.VMEM((8, 128), F32)],
        compiler_params=_cparams(("arbitrary", "arbitrary")),
        name="mlstm_scan_bwd" if bwd else "mlstm_scan_fwd",
    )(q, k, v, gt_dir, gtt_dir)


def _mlstm_finish(hf_ref, hb_ref, xc_ref, z_ref, skip_ref, g_ref):
    hsum = hf_ref[...].astype(F32) + hb_ref[...].astype(F32)
    parts = []
    for h in range(ML_HEADS):
        sl = hsum[:, h * ML_DH:(h + 1) * ML_DH]
        parts.append(sl * lax.rsqrt(jnp.mean(sl * sl, axis=-1, keepdims=True) + EPS))
    hn = jnp.concatenate(parts, axis=1) * g_ref[...]
    return (hn + skip_ref[...] * xc_ref[...].astype(F32)) * _silu(z_ref[...].astype(F32))


def _ffn_kernel(x_ref, g_ref, mod_ref, w1_ref, w3_ref, w2_ref, o_ref, h_sc, acc_sc, *, shift_idx, scale_idx, gate_idx):
    f = pl.program_id(1)

    @pl.when(f == 0)
    def _():
        h_sc[...] = _norm_mod(x_ref[...], g_ref[...], mod_ref, shift_idx, scale_idx).astype(BF16)
        acc_sc[...] = jnp.zeros_like(acc_sc)

    h = h_sc[...]
    a = _silu(_dot(h, w1_ref[...])) * _dot(h, w3_ref[...])
    acc_sc[...] += _dot(a.astype(BF16), w2_ref[...])

    @pl.when(f == pl.num_programs(1) - 1)
    def _():
        o_ref[...] = _rows_gate_residual(x_ref[...], acc_sc[...], mod_ref, gate_idx)


def dense_ffn(x, g, rowmod, w1, w3, w2, layer, *, tm=1024, tf=512):
    m = x.shape[0]
    tm = _pick_tile(m, tm)
    gm = tm // ROW_GROUP
    nf = D_FF // tf
    return pl.pallas_call(
        functools.partial(_ffn_kernel, shift_idx=3, scale_idx=4, gate_idx=5),
        grid=(m // tm, nf),
        in_specs=[pl.BlockSpec((tm, D_MODEL), lambda i, f: (i, 0)),
                  pl.BlockSpec((1, D_MODEL), lambda i, f: (0, 0)),
                  pl.BlockSpec((gm, 6, D_MODEL), lambda i, f: (i, 0, 0)),
                  pl.BlockSpec((None, D_MODEL, tf), lambda i, f: (layer, 0, f)),
                  pl.BlockSpec((None, D_MODEL, tf), lambda i, f: (layer, 0, f)),
                  pl.BlockSpec((None, tf, D_MODEL), lambda i, f: (layer, f, 0))],
        out_specs=pl.BlockSpec((tm, D_MODEL), lambda i, f: (i, 0)),
        out_shape=jax.ShapeDtypeStruct((m, D_MODEL), F32),
        scratch_shapes=[pltpu.VMEM((tm, D_MODEL), BF16), pltpu.VMEM((tm, D_MODEL), F32)],
        compiler_params=_cparams(("parallel", "arbitrary")),
        name="dense_ffn",
    )(x, g.reshape(1, D_MODEL), rowmod, w1, w3, w2)


def _router_kernel(x_ref, g_ref, mod_ref, wr_ref, h_ref, route_ref, cnt_ref, cnt_sc, *, shift_idx, scale_idx, n_real):
    i = pl.program_id(0)

    @pl.when(i == 0)
    def _():
        cnt_sc[...] = jnp.zeros_like(cnt_sc)

    @pl.when(i < n_real)
    def _():
        h = _norm_mod(x_ref[...], g_ref[...], mod_ref, shift_idx, scale_idx)
        h_ref[...] = h.astype(h_ref.dtype)
        logits = _dot_hi(h, wr_ref[...])
        tm = logits.shape[0]
        lane = lax.broadcasted_iota(jnp.int32, logits.shape, 1)
        logits = jnp.where(lane < N_EXPERTS, logits, -jnp.inf)
        m1 = jnp.max(logits, axis=-1, keepdims=True)
        i1 = jnp.min(jnp.where(logits == m1, lane, 128), axis=-1, keepdims=True)
        rest = jnp.where(lane == i1, -jnp.inf, logits)
        m2 = jnp.max(rest, axis=-1, keepdims=True)
        i2 = jnp.min(jnp.where(rest == m2, lane, 128), axis=-1, keepdims=True)
        e2 = jnp.exp(m2 - m1)
        w1 = 1.0 / (1.0 + e2)
        w2 = e2 / (1.0 + e2)
        chosen = jnp.where(lane == i1, 1.0, jnp.where(lane == i2, 1.0, 0.0))
        ii = lax.broadcasted_iota(jnp.int32, (tm, tm), 0)
        jj = lax.broadcasted_iota(jnp.int32, (tm, tm), 1)
        strict = jnp.where(ii > jj, 1.0, 0.0).astype(BF16)
        prefix = _dot(strict, chosen.astype(BF16)) + cnt_sc[0:1, :]
        r1 = jnp.sum(jnp.where(lane == i1, prefix, 0.0), axis=-1, keepdims=True)
        r2 = jnp.sum(jnp.where(lane == i2, prefix, 0.0), axis=-1, keepdims=True)
        cnt_sc[...] = cnt_sc[...] + jnp.sum(chosen, axis=0, keepdims=True)
        route = jnp.zeros(logits.shape, F32)
        for col, val in enumerate((i1.astype(F32), i2.astype(F32), w1, w2, r1, r2)):
            route = jnp.where(lane == col, val, route)
        route_ref[...] = route

    @pl.when(i >= n_real)
    def _():
        h_ref[...] = jnp.zeros_like(h_ref)
        route_ref[...] = jnp.zeros_like(route_ref)

    cnt_ref[...] = cnt_sc[...]


def moe_router(x, g, rowmod, wr_pad, p_rows, *, tm=PROJ_TM):
    m = x.shape[0]
    n_real = m // tm
    gm = tm // ROW_GROUP
    clamp = lambda i: jnp.minimum(i, n_real - 1)
    return pl.pallas_call(
        functools.partial(_router_kernel, shift_idx=3, scale_idx=4, n_real=n_real),
        grid=(p_rows // tm,),
        in_specs=[pl.BlockSpec((tm, D_MODEL), lambda i: (clamp(i), 0)),
                  pl.BlockSpec((1, D_MODEL), lambda i: (0, 0)),
                  pl.BlockSpec((gm, 6, D_MODEL), lambda i: (clamp(i), 0, 0)),
                  pl.BlockSpec(wr_pad.shape, lambda i: (0, 0))],
        out_specs=[pl.BlockSpec((tm, D_MODEL), lambda i: (i, 0)), pl.BlockSpec((tm, 128), lambda i: (i, 0)),
                   pl.BlockSpec((8, 128), lambda i: (0, 0))],
        out_shape=[jax.ShapeDtypeStruct((p_rows, D_MODEL), BF16), jax.ShapeDtypeStruct((p_rows, 128), F32),
                   jax.ShapeDtypeStruct((8, 128), F32)],
        scratch_shapes=[pltpu.VMEM((8, 128), F32)],
        compiler_params=_cparams(("arbitrary",)),
        name="moe_router",
    )(x, g.reshape(1, D_MODEL), rowmod, wr_pad)


def _moe_ffn_kernel(te_ref, nt_ref, h_ref, w1_ref, w3_ref, w2_ref, o_ref, acc_sc):
    i = pl.program_id(0)
    f = pl.program_id(1)
    live = i < nt_ref[0]

    @pl.when(f == 0)
    def _():
        acc_sc[...] = jnp.zeros_like(acc_sc)

    @pl.when(live)
    def _():
        h = h_ref[...]
        a = _silu(_dot(h, w1_ref[...])) * _dot(h, w3_ref[...])
        acc_sc[...] += _dot(a.astype(BF16), w2_ref[...])

    @pl.when(f == pl.num_programs(1) - 1)
    def _():
        o_ref[...] = acc_sc[...].astype(o_ref.dtype)


def moe_grouped_ffn(h_sorted, tile_expert, n_tiles, w1, w3, w2, layer, *, tm=MOE_TM, tf=512):
    p = h_sorted.shape[0]
    nf = D_FF // tf
    grid_spec = pltpu.PrefetchScalarGridSpec(
        num_scalar_prefetch=2,
        grid=(p // tm, nf),
        in_specs=[pl.BlockSpec((tm, D_MODEL), lambda i, f, te, nt: (i, 0)),
                  pl.BlockSpec((None, None, D_MODEL, tf), lambda i, f, te, nt: (layer, te[i], 0, f)),
                  pl.BlockSpec((None, None, D_MODEL, tf), lambda i, f, te, nt: (layer, te[i], 0, f)),
                  pl.BlockSpec((None, None, tf, D_MODEL), lambda i, f, te, nt: (layer, te[i], f, 0))],
        out_specs=pl.BlockSpec((tm, D_MODEL), lambda i, f, te, nt: (i, 0)),
        scratch_shapes=[pltpu.VMEM((tm, D_MODEL), F32)])
    return pl.pallas_call(
        _moe_ffn_kernel,
        grid_spec=grid_spec,
        out_shape=jax.ShapeDtypeStruct((p, D_MODEL), BF16),
        compiler_params=_cparams(("arbitrary", "arbitrary")),
        name="moe_grouped_ffn",
    )(tile_expert, n_tiles, h_sorted, w1, w3, w2)


def _moe_combine_kernel(x_ref, ya_ref, yb_ref, route_ref, mod_ref, o_ref, *, gate_idx):
    r = route_ref[...]
    y = r[:, 2:3] * ya_ref[...].astype(F32) + r[:, 3:4] * yb_ref[...].astype(F32)
    o_ref[...] = _rows_gate_residual(x_ref[...], y, mod_ref, gate_idx)


def moe_combine(x, ya, yb, route, rowmod, *, tm=256):
    m = x.shape[0]
    row = pl.BlockSpec((tm, D_MODEL), lambda i: (i, 0))
    return pl.pallas_call(
        functools.partial(_moe_combine_kernel, gate_idx=5),
        grid=(m // tm,),
        in_specs=[row, row, row, pl.BlockSpec((tm, 128), lambda i: (i, 0)),
                  pl.BlockSpec((tm // ROW_GROUP, 6, D_MODEL), lambda i: (i, 0, 0))],
        out_specs=row,
        out_shape=jax.ShapeDtypeStruct((m, D_MODEL), F32),
        compiler_params=_cparams(("parallel",)),
        name="moe_combine",
    )(x, ya, yb, route, rowmod)


def moe_ffn(x, g, rowmod, w_router, w1, w3, w2, layer):
    m = x.shape[0]
    tm = MOE_TM
    n_tiles_max = (2 * m) // tm + N_EXPERTS
    p = n_tiles_max * tm
    wr_pad = jnp.zeros((D_MODEL, 128), F32).at[:, :N_EXPERTS].set(w_router)
    h, route, cnt = moe_router(x, g, rowmod, wr_pad, p)
    experts = jnp.arange(N_EXPERTS, dtype=jnp.int32)
    counts = cnt[0, :N_EXPERTS].astype(jnp.int32)
    tiles_per = (counts + tm - 1) // tm
    tile_end = jnp.cumsum(tiles_per)
    grp_start = (tile_end - tiles_per) * tm
    cnt_start = jnp.cumsum(counts) - counts
    e12 = route[:m, 0:2].astype(jnp.int32)
    r12 = route[:m, 4:6].astype(jnp.int32)
    pos12 = jnp.sum(jnp.where(e12[:, :, None] == experts, grp_start, 0), axis=-1) + r12
    order = jnp.argsort(e12.reshape(-1), stable=True).astype(jnp.int32)
    tile_expert = jnp.minimum(
        jnp.sum(jnp.arange(n_tiles_max, dtype=jnp.int32)[:, None] >= tile_end[None, :], axis=1), N_EXPERTS - 1
    ).astype(jnp.int32)
    n_tiles = tile_end[-1:].astype(jnp.int32)
    rank = jnp.arange(p, dtype=jnp.int32) - jnp.repeat(grp_start[tile_expert], tm)
    in_use = rank < jnp.repeat(counts[tile_expert], tm)
    sorted_idx = jnp.clip(jnp.repeat(cnt_start[tile_expert], tm) + rank, 0, 2 * m - 1)
    src_token = jnp.where(in_use, jnp.take(order, sorted_idx, mode="clip") // 2, 0)
    h_sorted = jnp.take(h, src_token, axis=0, mode="clip")
    y_sorted = moe_grouped_ffn(h_sorted, tile_expert, n_tiles, w1, w3, w2, layer)
    ya = jnp.take(y_sorted, pos12[:, 0], axis=0, mode="clip")
    yb = jnp.take(y_sorted, pos12[:, 1], axis=0, mode="clip")
    return moe_combine(x, ya, yb, route, rowmod)


def _dir_split(a, n):
    m = a.shape[0]
    a3 = a.reshape(m, 2, n)
    return jnp.transpose(a3, (1, 0, 2)), jnp.transpose(a3, (1, 2, 0))


def ssd_layer(x, g, rowmod, w_in, conv_w, conv_b, dt_bias, a_log, d_skip, norm_g, w_out, *, bsz, ctx_len):
    m = x.shape[0]
    n_main = SSD_INNER + SSD_INNER + 2 * SSD_GROUPS * SSD_N
    zx, dt_raw = in_projection(x, g, rowmod, [w_in[:, :n_main].astype(BF16), w_in[:, n_main:].astype(BF16)],
                               [BF16, F32], shift_idx=0, scale_idx=1)
    u = dwconv_silu(zx, SSD_INNER, n_main - SSD_INNER, conv_w, conv_b, bsz=bsz, ctx_len=ctx_len)
    dt_dir, dtt_dir = _dir_split(dt_raw, SSD_HEADS)
    yf = ssd_scan(u, dt_dir, dtt_dir, dt_bias, a_log, bsz=bsz, ctx_len=ctx_len, bwd=False)
    yb = ssd_scan(u, dt_dir, dtt_dir, dt_bias, a_log, bsz=bsz, ctx_len=ctx_len, bwd=True)
    tm = PROJ_TM
    d_x = jnp.repeat(d_skip.astype(F32), SSD_P).reshape(1, SSD_INNER)
    specs = [pl.BlockSpec((tm, SSD_INNER), lambda i: (i, 0)),
             pl.BlockSpec((tm, SSD_INNER), lambda i: (i, 0)),
             pl.BlockSpec((tm, SSD_INNER), lambda i: (i, 0)),
             pl.BlockSpec((tm, SSD_INNER), lambda i: (i, 0)),
             pl.BlockSpec((1, SSD_INNER), lambda i: (0, 0)),
             pl.BlockSpec((1, SSD_INNER), lambda i: (0, 0))]
    return out_projection(_ssd_finish, [yf, yb, u, zx, d_x, norm_g.reshape(1, SSD_INNER)], specs,
                          w_out.astype(BF16), x, rowmod, gate_idx=2, tm=tm)


def hgrn_layer(x, g, rowmod, w_in, lb, norm_g, w_out, *, bsz, ctx_len):
    proj, = in_projection(x, g, rowmod, [w_in.astype(BF16)], [BF16], shift_idx=0, scale_idx=1)
    lb = lb.astype(F32).reshape(2, 1, D_MODEL)
    lbs = (jnp.log(lb), jnp.log1p(-lb), 1.0 - lb)
    of = gla_scan(proj, *lbs, bsz=bsz, ctx_len=ctx_len, bwd=False)
    ob = gla_scan(proj, *lbs, bsz=bsz, ctx_len=ctx_len, bwd=True)
    tm = PROJ_TM
    specs = [pl.BlockSpec((tm, D_MODEL), lambda i: (i, 0)),
             pl.BlockSpec((tm, D_MODEL), lambda i: (i, 0)),
             pl.BlockSpec((tm, D_MODEL), lambda i: (i, 4)),
             pl.BlockSpec((1, D_MODEL), lambda i: (0, 0))]
    return out_projection(_hgrn_finish, [of, ob, proj, norm_g.reshape(1, D_MODEL)], specs,
                          w_out.astype(BF16), x, rowmod, gate_idx=2, tm=tm)


def _attn_head_perm():
    r_per = ATT_HEADS // ATT_KV
    heads = [(2 * p + j) * r_per + r for p in range(ATT_KV // 2) for r in range(r_per) for j in range(2)]
    return np.concatenate([np.arange(h * ATT_HD, (h + 1) * ATT_HD) for h in heads])


def _rope_tables(seq_len, ctx_len, grid_w):
    rows = seq_len // grid_w
    row = jnp.repeat(jnp.arange(rows, dtype=F32), grid_w)
    col = jnp.tile(jnp.arange(grid_w, dtype=F32), rows)
    inv = ROPE_THETA ** (-jnp.arange(ROPE_FREQS, dtype=F32) / ROPE_FREQS)
    ang_r = row[:, None] * inv
    ang_c = col[:, None] * inv
    ang = jnp.concatenate([ang_r, ang_r, ang_c, ang_c], axis=-1)
    cos = jnp.concatenate([jnp.ones((ctx_len, ATT_HD), F32), jnp.cos(ang)], axis=0)
    sin = jnp.concatenate([jnp.zeros((ctx_len, ATT_HD), F32), jnp.sin(ang)], axis=0)
    return jnp.tile(cos, (1, 2)), jnp.tile(sin, (1, 2))


def _rope_matrices():
    r64 = np.zeros((ATT_HD, ATT_HD), np.float32)
    fq = ROPE_FREQS
    for ax in range(2):
        o = ax * 2 * fq
        for i in range(fq):
            r64[o + fq + i, o + i] = -1.0
            r64[o + i, o + fq + i] = 1.0
    n = ATT_HEADS
    bd = np.kron(np.eye(n, dtype=np.float32), np.ones((ATT_HD, ATT_HD), np.float32))
    rot = np.kron(np.eye(n, dtype=np.float32), r64)
    return jnp.asarray(bd, BF16), jnp.asarray(rot, BF16)


def attn_layer(x, g, rowmod, w_qkv, q_g, k_g, w_o, *, bsz, ctx_len, grid_w):
    m = x.shape[0]
    seq_len = m // bsz - ctx_len
    perm = _attn_head_perm()
    nq = ATT_HEADS * ATT_HD
    w = jnp.concatenate([w_qkv[:, :nq][:, perm], w_qkv[:, nq:]], axis=1).astype(BF16)
    cos_t, sin_t = _rope_tables(seq_len, ctx_len, grid_w)
    bd, rot = _rope_matrices()
    qg = jnp.tile(q_g.astype(F32), ATT_HEADS).reshape(1, nq)
    kg = jnp.tile(k_g.astype(F32), ATT_KV).reshape(1, ATT_KV * ATT_HD)
    q, k, v = attn_qkv(x, g, rowmod, w, cos_t, sin_t, qg, kg, bd, rot, bsz=bsz, shift_idx=0, scale_idx=1)
    o = attention(q, k, v, bsz=bsz, ctx_len=ctx_len)
    tm = PROJ_TM
    specs = [pl.BlockSpec((tm, nq), lambda i: (i, 0))]
    return out_projection(_identity_pro, [o], specs, w_o[perm, :].astype(BF16), x, rowmod, gate_idx=2, tm=tm)


def mlstm_layer(x, g, rowmod, w_up, conv_w, conv_b, w_q, w_k, w_v, w_gate, b_gate, skip, norm_g, w_down,
                *, bsz, ctx_len):
    up, = in_projection(x, g, rowmod, [w_up.astype(BF16)], [BF16], shift_idx=0, scale_idx=1)
    xc = dwconv_silu(up, 0, ML_INNER, conv_w, conv_b, bsz=bsz, ctx_len=ctx_len)
    q, k, v, gates = mlstm_qkv(xc, up, w_q.astype(BF16), w_k.astype(BF16), w_v.astype(BF16),
                               w_gate.astype(BF16), b_gate)
    gt_dir, gtt_dir = _dir_split(gates, 2 * ML_HEADS)
    hf = mlstm_scan(q, k, v, gt_dir, gtt_dir, bsz=bsz, ctx_len=ctx_len, bwd=False)
    hb = mlstm_scan(q, k, v, gt_dir, gtt_dir, bsz=bsz, ctx_len=ctx_len, bwd=True)
    tm = PROJ_TM
    specs = [pl.BlockSpec((tm, ML_INNER), lambda i: (i, 0)),
             pl.BlockSpec((tm, ML_INNER), lambda i: (i, 0)),
             pl.BlockSpec((tm, ML_INNER), lambda i: (i, 0)),
             pl.BlockSpec((tm, ML_INNER), lambda i: (i, 1)),
             pl.BlockSpec((1, ML_INNER), lambda i: (0, 0)),
             pl.BlockSpec((1, ML_INNER), lambda i: (0, 0))]
    return out_projection(_mlstm_finish, [hf, hb, xc, up, skip.reshape(1, ML_INNER), norm_g.reshape(1, ML_INNER)],
                          specs, w_down.astype(BF16), x, rowmod, gate_idx=2, tm=tm)


def kernel(x, c, ctx, c_ctx, ada_w, ada_b, norm_g, ssd_w_in, ssd_conv_w, ssd_conv_b, ssd_dt_bias, ssd_a_log, ssd_d, ssd_norm_g, ssd_w_out, hgrn_w_in, hgrn_lb, hgrn_norm_g, hgrn_w_out, attn_w_qkv, attn_q_g, attn_k_g, attn_w_o, mlstm_w_up, mlstm_conv_w, mlstm_conv_b, mlstm_w_q, mlstm_w_k, mlstm_w_v, mlstm_w_gate, mlstm_b_gate, mlstm_skip, mlstm_norm_g, mlstm_w_down, ffn_w1, ffn_w3, ffn_w2, moe_router, moe_w1, moe_w3, moe_w2):
    bsz, seq_len, _ = x.shape
    ctx_len = ctx.shape[1]
    depth = ada_w.shape[0]
    grid_w = 64
    t_all = ctx_len + seq_len
    m = bsz * t_all
    xa = jnp.concatenate([ctx, x], axis=1).reshape(m, D_MODEL)
    c_pad = jnp.zeros((8, D_MODEL), F32).at[:bsz].set(c).at[bsz].set(c_ctx)
    groups_per_batch = t_all // ROW_GROUP
    ctx_groups = ctx_len // ROW_GROUP
    gidx = np.array([bsz if (gi % groups_per_batch) < ctx_groups else gi // groups_per_batch
                     for gi in range(m // ROW_GROUP)], np.int32)
    lb_all = jnp.cumsum(jax.nn.softmax(hgrn_lb.astype(F32), axis=1), axis=1)
    lb_all = lb_all - lb_all[:, :1]
    kw = dict(bsz=bsz, ctx_len=ctx_len)
    ffn_w = [w.astype(BF16) for w in (ffn_w1, ffn_w3, ffn_w2)]
    moe_w = [w.astype(BF16) for w in (moe_w1, moe_w3, moe_w2)]
    for i in range(depth):
        mod = ada_modulation(c_pad, ada_w[i], ada_b[i]).reshape(8, 6, D_MODEL)
        rowmod = mod[gidx]
        kind, j = i % 4, i // 4
        if kind == 0:
            xa = ssd_layer(xa, norm_g[i, 0], rowmod, ssd_w_in[j], ssd_conv_w[j], ssd_conv_b[j], ssd_dt_bias[j],
                           ssd_a_log[j], ssd_d[j], ssd_norm_g[j], ssd_w_out[j], **kw)
        elif kind == 1:
            xa = hgrn_layer(xa, norm_g[i, 0], rowmod, hgrn_w_in[j], lb_all[:, i], hgrn_norm_g[j], hgrn_w_out[j], **kw)
        elif kind == 2:
            xa = attn_layer(xa, norm_g[i, 0], rowmod, attn_w_qkv[j], attn_q_g[j], attn_k_g[j], attn_w_o[j],
                            grid_w=grid_w, **kw)
        else:
            xa = mlstm_layer(xa, norm_g[i, 0], rowmod, mlstm_w_up[j], mlstm_conv_w[j], mlstm_conv_b[j], mlstm_w_q[j],
                             mlstm_w_k[j], mlstm_w_v[j], mlstm_w_gate[j], mlstm_b_gate[j], mlstm_skip[j],
                             mlstm_norm_g[j], mlstm_w_down[j], **kw)
        if i % 2 == 0:
            xa = dense_ffn(xa, norm_g[i, 1], rowmod, *ffn_w, i // 2)
        else:
            xa = moe_ffn(xa, norm_g[i, 1], rowmod, moe_router[i // 2], *moe_w, i // 2)
    return xa.reshape(bsz, t_all, D_MODEL)[:, ctx_len:]
```

```python
import functools
import math

import jax
import jax.numpy as jnp
import numpy as np
from jax import lax
from jax.experimental import pallas as pl
from jax.experimental.pallas import tpu as pltpu

F32 = jnp.float32
BF16 = jnp.bfloat16
HI = lax.Precision.HIGHEST

D_MODEL = 1024
EPS = 1e-6
ROW_GROUP = 256
PROJ_TM = 512
CONV_W = 5
NEG_BIG = -1e30
LOG2E = math.log2(math.e)
VMEM_LIMIT = 56 << 20

SSD_INNER = 2 * D_MODEL
SSD_P = 64
SSD_HEADS = SSD_INNER // SSD_P
SSD_N = 128
SSD_GROUPS = 8
SSD_GW = SSD_INNER // SSD_GROUPS
SCAN_L = 128

HG_HEADS = 8
HG_DK = 128
HG_SUB = 8

ATT_HEADS = 16
ATT_KV = 4
ATT_HD = 64
ROPE_THETA = 10000.0
ROPE_FREQS = ATT_HD // 4
ATT_TQ = 128
ATT_SHIFT_MAX = 60.0

ML_INNER = 2 * D_MODEL
ML_HEADS = 4
ML_DH = ML_INNER // ML_HEADS
ML_L = 256

D_FF = 7 * D_MODEL // 2
N_EXPERTS = 8
MOE_TM = 1024


def _cparams(sem):
    return pltpu.CompilerParams(dimension_semantics=sem, vmem_limit_bytes=VMEM_LIMIT)


def _dot(a, b):
    return jnp.dot(a, b, preferred_element_type=F32)


def _dot_hi(a, b):
    return jnp.dot(a, b, preferred_element_type=F32, precision=HI)


def _dot_nt(a, b):
    return lax.dot_general(a, b, (((1,), (1,)), ((), ())), preferred_element_type=F32)


def _dot_tn(a, b):
    return lax.dot_general(a, b, (((0,), (0,)), ((), ())), preferred_element_type=F32)


def _silu(x):
    return x * jax.nn.sigmoid(x)


def _log_sigmoid(x):
    return jnp.minimum(x, 0.0) - jnp.log1p(jnp.exp(-jnp.abs(x)))


def _softplus(x):
    return jnp.maximum(x, 0.0) + jnp.log1p(jnp.exp(-jnp.abs(x)))


def _pick_tile(m, pref):
    t = pref
    while m % t:
        t //= 2
    return t


def _rows_scale_shift(y, mod_ref, scale_idx, shift_idx):
    parts = []
    for gi in range(y.shape[0] // ROW_GROUP):
        sl = y[gi * ROW_GROUP:(gi + 1) * ROW_GROUP]
        parts.append(sl * (1.0 + mod_ref[gi, scale_idx:scale_idx + 1, :]) + mod_ref[gi, shift_idx:shift_idx + 1, :])
    return parts[0] if len(parts) == 1 else jnp.concatenate(parts, axis=0)


def _rows_gate_residual(x, acc, mod_ref, gate_idx):
    parts = []
    for gi in range(x.shape[0] // ROW_GROUP):
        sl = slice(gi * ROW_GROUP, (gi + 1) * ROW_GROUP)
        parts.append(x[sl] + mod_ref[gi, gate_idx:gate_idx + 1, :] * acc[sl])
    return parts[0] if len(parts) == 1 else jnp.concatenate(parts, axis=0)


def _norm_mod(x, g, mod_ref, shift_idx, scale_idx):
    y = x * lax.rsqrt(jnp.mean(x * x, axis=-1, keepdims=True) + EPS) * g
    return _rows_scale_shift(y, mod_ref, scale_idx, shift_idx)


def _split2(a):
    hi = a.astype(BF16)
    return hi, (a - hi.astype(F32)).astype(BF16)


def _dot_sel_l(sel, a):
    hi, lo = _split2(a)
    return _dot(sel, hi) + _dot(sel, lo)


def _dot_sel_r(a, sel):
    hi, lo = _split2(a)
    return _dot(hi, sel) + _dot(lo, sel)


def _scan_masks(n, bwd):
    ii = lax.broadcasted_iota(jnp.int32, (n, n), 0)
    jj = lax.broadcasted_iota(jnp.int32, (n, n), 1)
    visible = (ii <= jj) if bwd else (ii >= jj)
    visible_t = (ii >= jj) if bwd else (ii <= jj)
    tri = jnp.where(visible, 1.0, 0.0).astype(BF16)
    trit = jnp.where(visible_t, 1.0, 0.0).astype(BF16)
    return ii, jj, visible, tri, trit


def _ada_kernel(c_ref, w_ref, b_ref, o_ref):
    c = c_ref[...]
    o_ref[...] = _dot_hi(_silu(c), w_ref[...]) + b_ref[...]


def ada_modulation(c_pad, w, b, layer):
    n = w.shape[2]
    tn = 1024
    return pl.pallas_call(
        _ada_kernel,
        grid=(n // tn,),
        in_specs=[pl.BlockSpec(c_pad.shape, lambda j: (0, 0)),
                  pl.BlockSpec((None, D_MODEL, tn), lambda j: (layer, 0, j)),
                  pl.BlockSpec((None, 1, tn), lambda j: (layer, 0, j))],
        out_specs=pl.BlockSpec((c_pad.shape[0], tn), lambda j: (0, j)),
        out_shape=jax.ShapeDtypeStruct((c_pad.shape[0], n), F32),
        compiler_params=_cparams(("arbitrary",)),
        name="ada_modulation",
    )(c_pad, w, b.reshape(b.shape[0], 1, n))


def _inproj_kernel(x_ref, g_ref, mod_ref, *refs, n_w, shift_idx, scale_idx, tn):
    h = _norm_mod(x_ref[...], g_ref[...], mod_ref, shift_idx, scale_idx).astype(BF16)
    for w_ref, o_ref in zip(refs[:n_w], refs[n_w:]):
        n = o_ref.shape[1]
        step = min(tn, n)
        for j in range(n // step):
            o_ref[:, j * step:(j + 1) * step] = _dot(h, w_ref[:, j * step:(j + 1) * step]).astype(o_ref.dtype)


def in_projection(x, g, rowmod, ws, out_dtypes, *, shift_idx, scale_idx, tm=PROJ_TM, tn=512):
    m = x.shape[0]
    tm = _pick_tile(m, tm)
    gm = tm // ROW_GROUP
    in_specs = [pl.BlockSpec((tm, D_MODEL), lambda i: (i, 0)),
                pl.BlockSpec((1, D_MODEL), lambda i: (0, 0)),
                pl.BlockSpec((gm, 6, D_MODEL), lambda i: (i, 0, 0))]
    in_specs += [pl.BlockSpec(w.shape, lambda i: (0, 0)) for w in ws]
    out_specs = [pl.BlockSpec((tm, w.shape[1]), lambda i: (i, 0)) for w in ws]
    out_shape = [jax.ShapeDtypeStruct((m, w.shape[1]), dt) for w, dt in zip(ws, out_dtypes)]
    return pl.pallas_call(
        functools.partial(_inproj_kernel, n_w=len(ws), shift_idx=shift_idx, scale_idx=scale_idx, tn=tn),
        grid=(m // tm,),
        in_specs=in_specs, out_specs=out_specs, out_shape=out_shape,
        compiler_params=_cparams(("parallel",)),
        name="in_projection",
    )(x, g.reshape(1, D_MODEL), rowmod, *ws)


def _outproj_kernel(*refs, n_pro, pro_fn, gate_idx):
    pro_refs = refs[:n_pro]
    w_ref, x_ref, mod_ref, o_ref = refs[n_pro:]
    a = pro_fn(*pro_refs).astype(BF16)
    acc = _dot(a, w_ref[...])
    o_ref[...] = _rows_gate_residual(x_ref[...], acc, mod_ref, gate_idx)


def out_projection(pro_fn, pro_args, pro_specs, w, x, rowmod, *, gate_idx, tm):
    m = x.shape[0]
    gm = tm // ROW_GROUP
    in_specs = list(pro_specs) + [pl.BlockSpec(w.shape, lambda i: (0, 0)),
                                  pl.BlockSpec((tm, D_MODEL), lambda i: (i, 0)),
                                  pl.BlockSpec((gm, 6, D_MODEL), lambda i: (i, 0, 0))]
    return pl.pallas_call(
        functools.partial(_outproj_kernel, n_pro=len(pro_args), pro_fn=pro_fn, gate_idx=gate_idx),
        grid=(m // tm,),
        in_specs=in_specs,
        out_specs=pl.BlockSpec((tm, D_MODEL), lambda i: (i, 0)),
        out_shape=jax.ShapeDtypeStruct((m, D_MODEL), F32),
        compiler_params=_cparams(("parallel",)),
        name="out_projection",
    )(*pro_args, w, x, rowmod)


def _conv_kernel(u_ref, w_ref, b_ref, o_ref, *, ctx_len):
    x = u_ref[...].astype(F32)
    t_all = x.shape[0]
    t = lax.broadcasted_iota(jnp.int32, (t_all, 1), 0)
    in_ctx = t < ctx_len
    pad = CONV_W // 2
    acc = b_ref[...] + w_ref[pad:pad + 1, :] * x
    for off in range(-pad, pad + 1):
        if off == 0:
            continue
        xs = pltpu.roll(x, (-off) % t_all, 0)
        tt = t + off
        valid = (tt >= 0) & (tt < t_all) & ((tt < ctx_len) == in_ctx)
        acc = acc + w_ref[pad + off:pad + off + 1, :] * jnp.where(valid, xs, 0.0)
    o_ref[...] = _silu(acc).astype(o_ref.dtype)


def dwconv_silu(u, col_off, width, w, b, *, bsz, ctx_len, tc=256):
    m = u.shape[0]
    t_all = m // bsz
    cb = col_off // tc
    return pl.pallas_call(
        functools.partial(_conv_kernel, ctx_len=ctx_len),
        grid=(bsz, width // tc),
        in_specs=[pl.BlockSpec((t_all, tc), lambda bi, j: (bi, j + cb)),
                  pl.BlockSpec((CONV_W, tc), lambda bi, j: (0, j)),
                  pl.BlockSpec((1, tc), lambda bi, j: (0, j))],
        out_specs=pl.BlockSpec((t_all, tc), lambda bi, j: (bi, j)),
        out_shape=jax.ShapeDtypeStruct((m, width), BF16),
        compiler_params=_cparams(("parallel", "parallel")),
        name="dwconv_silu",
    )(u, w, b.reshape(1, width))


def _scan_chunk_index(bwd, c, n_ctx_chunks, n_chunks):
    if not bwd:
        return c
    return jnp.where(c < n_ctx_chunks, n_ctx_chunks - 1 - c, n_chunks - 1 + n_ctx_chunks - c)


def _ssd_scan_kernel(x_ref, b_ref, c_ref, dt_ref, dtt_ref, bias_ref, biast_ref, alog_ref, alogt_ref, exp_ref,
                     y_ref, state, *, bwd):
    ci = pl.program_id(1)
    L = SCAN_L
    H = SSD_HEADS

    @pl.when(ci == 0)
    def _():
        state[...] = jnp.zeros_like(state)

    _, _, visible, tri, trit = _scan_masks(L, bwd)
    dt = _softplus(dt_ref[...] + bias_ref[...])
    dtt = _softplus(dtt_ref[...] + biast_ref[...])
    a_neg = -jnp.exp(alog_ref[...])
    a_negt = -jnp.exp(alogt_ref[...])
    dta = dt * a_neg
    a_col = _dot_sel_l(tri, dta)
    a_row = _dot_sel_r(dtt * a_negt, trit)
    a_end = jnp.sum(dta, axis=0, keepdims=True)
    per_head = jnp.concatenate([jnp.exp(a_col), jnp.exp(a_end - a_col) * dt,
                                jnp.broadcast_to(jnp.exp(a_end), (16, H))], axis=0)
    per_chan = _dot_sel_r(per_head, exp_ref[...])
    ea_x, w_x, eend_x = per_chan[:L], per_chan[L:2 * L], per_chan[2 * L:2 * L + 1]
    lane = lax.broadcasted_iota(jnp.int32, (L, 2 * SSD_P), 1)
    for g in range(SSD_GROUPS):
        gs = slice(g * SSD_GW, (g + 1) * SSD_GW)
        ns = slice(g * SSD_N, (g + 1) * SSD_N)
        bg = b_ref[:, ns]
        cg = c_ref[:, ns]
        xgb = x_ref[:, gs]
        cb = _dot_nt(cg, bg)
        sg = state[:, gs]
        y_inter = _dot(cg, sg.astype(BF16)) * ea_x[:, gs]
        pieces = []
        for pr in range(SSD_GW // (2 * SSD_P)):
            ys = []
            for q in range(2):
                h = g * (SSD_GW // SSD_P) + 2 * pr + q
                rel = a_col[:, h:h + 1] - a_row[h:h + 1, :]
                gm = cb * jnp.exp(jnp.where(visible, rel, NEG_BIG)) * dtt[h:h + 1, :]
                ys.append(_dot(gm.astype(BF16), xgb[:, pr * 2 * SSD_P:(pr + 1) * 2 * SSD_P]))
            pieces.append(jnp.where(lane < SSD_P, ys[0], ys[1]))
        y = jnp.concatenate(pieces, axis=1) + y_inter
        y_ref[:, gs] = y.astype(y_ref.dtype)
        xw = (xgb.astype(F32) * w_x[:, gs]).astype(BF16)
        state[:, gs] = sg * eend_x[:, gs] + _dot_tn(bg, xw)


def ssd_scan(u, dt_dir, dtt_dir, dt_bias, a_log, *, bsz, ctx_len, bwd):
    m = u.shape[0]
    t_all = m // bsz
    nc = t_all // SCAN_L
    ncc = ctx_len // SCAN_L
    L, H = SCAN_L, SSD_HEADS
    d = int(bwd)

    def row(bi, c):
        return bi * nc + _scan_chunk_index(bwd, c, ncc, nc)

    bn = SSD_INNER // (SSD_GROUPS * SSD_N)
    expand = jnp.asarray(np.kron(np.eye(H, dtype=np.float32), np.ones((1, SSD_P), np.float32)), BF16)
    return pl.pallas_call(
        functools.partial(_ssd_scan_kernel, bwd=bwd),
        grid=(bsz, nc),
        in_specs=[pl.BlockSpec((L, SSD_INNER), lambda bi, c: (row(bi, c), 0)),
                  pl.BlockSpec((L, SSD_GROUPS * SSD_N), lambda bi, c: (row(bi, c), bn)),
                  pl.BlockSpec((L, SSD_GROUPS * SSD_N), lambda bi, c: (row(bi, c), bn + 1)),
                  pl.BlockSpec((None, L, H), lambda bi, c: (d, row(bi, c), 0)),
                  pl.BlockSpec((None, H, L), lambda bi, c: (d, 0, row(bi, c))),
                  pl.BlockSpec((None, 1, H), lambda bi, c: (d, 0, 0)),
                  pl.BlockSpec((None, H, 1), lambda bi, c: (d, 0, 0)),
                  pl.BlockSpec((None, 1, H), lambda bi, c: (d, 0, 0)),
                  pl.BlockSpec((None, H, 1), lambda bi, c: (d, 0, 0)),
                  pl.BlockSpec((H, SSD_INNER), lambda bi, c: (0, 0))],
        out_specs=pl.BlockSpec((L, SSD_INNER), lambda bi, c: (row(bi, c), 0)),
        out_shape=jax.ShapeDtypeStruct((m, SSD_INNER), BF16),
        scratch_shapes=[pltpu.VMEM((SSD_N, SSD_INNER), F32)],
        compiler_params=_cparams(("arbitrary", "arbitrary")),
        name="ssd_scan_bwd" if bwd else "ssd_scan_fwd",
    )(u, u, u, dt_dir, dtt_dir, dt_bias.reshape(2, 1, H), dt_bias.reshape(2, H, 1),
      a_log.reshape(2, 1, H), a_log.reshape(2, H, 1), expand)


def _ssd_finish(yf_ref, yb_ref, xs_ref, z_ref, d_ref, g_ref):
    y = yf_ref[...].astype(F32) + yb_ref[...].astype(F32) + d_ref[...] * xs_ref[...].astype(F32)
    y = y * _silu(z_ref[...].astype(F32))
    parts = []
    for g in range(SSD_GROUPS):
        sl = y[:, g * SSD_GW:(g + 1) * SSD_GW]
        parts.append(sl * lax.rsqrt(jnp.mean(sl * sl, axis=-1, keepdims=True) + EPS))
    return jnp.concatenate(parts, axis=1) * g_ref[...]


def _gla_scan_kernel(q_ref, f_ref, v_ref, llb_ref, l1m_ref, omlb_ref, place_ref, o_ref, state, *, bwd):
    ci = pl.program_id(1)
    L, C = SCAN_L, HG_SUB
    nb = L // C
    dk = HG_DK

    @pl.when(ci == 0)
    def _():
        state[...] = jnp.zeros_like(state)

    ii, jj, visible, tri, _ = _scan_masks(L, bwd)
    band_mask = ((ii // C) == (jj // C)) & visible
    off_mask = ((ii // C) < (jj // C)) if bwd else ((ii // C) > (jj // C))
    last = 0 if bwd else L - 1
    blocks = range(1, nb) if bwd else range(nb - 1)

    def block_end(j):
        return j * C if bwd else (j + 1) * C - 1

    def head(h, carry):
        ls = slice(h * dk, (h + 1) * dk)
        q = _silu(q_ref[:, ls].astype(F32)) * (dk ** -0.5)
        f = f_ref[:, ls].astype(F32)
        v = v_ref[:, ls]
        a = llb_ref[:, ls]
        b = l1m_ref[:, ls] + _log_sigmoid(f)
        lf = (jnp.maximum(a, b) + jnp.log1p(jnp.exp(-jnp.abs(a - b)))) * LOG2E
        k = omlb_ref[:, ls] * jax.nn.sigmoid(-f)
        fd = jnp.exp2(lf)
        cum = _dot_sel_l(tri, lf)
        c_end = cum[last:last + 1, :]
        kd = k
        es = [(q * kd).astype(BF16)]
        for dl in range(1, C):
            kd = fd * pltpu.roll(kd, (L - 1) if bwd else 1, 0)
            es.append((q * kd).astype(BF16))
        band = _dot(jnp.concatenate(es, axis=1), place_ref[...])
        band = pltpu.roll(band, 0, 1, stride=1, stride_axis=0)
        ends = [cum[block_end(j):block_end(j) + 1, :] for j in range(nb)]
        ktb = jnp.concatenate([k[j * C:(j + 1) * C] * jnp.exp2(ends[j] - cum[j * C:(j + 1) * C]) for j in range(nb)],
                              axis=0).astype(BF16)
        sub16 = lax.broadcasted_iota(jnp.int32, (16, 1), 0) // C
        zero_b = jnp.zeros((), BF16)
        kcols, qcols = [], []
        pieces = [None] * nb
        for j in (blocks if bwd else reversed(blocks)):
            t0 = (j * C // 16) * 16
            tile = jnp.where(sub16 == (j * C - t0) // C, ktb[t0:t0 + 16], zero_b)
            kcols.append(jnp.concatenate(
                [x for x in (jnp.zeros((t0, dk), BF16) if t0 else None, tile,
                             jnp.zeros((L - t0 - 16, dk), BF16) if L - t0 - 16 else None) if x is not None], axis=0))
            near = j - 1 if bwd else j + 1
            prev = j - 1 if bwd else j + 1
            if 0 <= prev < nb and prev in blocks:
                step = jnp.exp2(ends[prev] - ends[j])
                for i in (range(0, near) if bwd else range(near + 1, nb)):
                    pieces[i] = pieces[i] * step
            rs = slice(near * C, (near + 1) * C)
            pieces[near] = q[rs] * jnp.exp2(cum[rs] - ends[j])
            zero_f = jnp.zeros((C, dk), F32)
            qcols.append(jnp.concatenate([zero_f if p is None else p for p in pieces], axis=0).astype(BF16))
        off = _dot_nt(jnp.concatenate(qcols, axis=1), jnp.concatenate(kcols, axis=1))
        att = jnp.where(band_mask, band, jnp.where(off_mask, off, 0.0))
        st = state[h]
        o = _dot(att.astype(BF16), v) + _dot_nt((q * jnp.exp2(cum)).astype(BF16), st.astype(BF16))
        o_ref[:, ls] = o.astype(o_ref.dtype)
        kw = (k * jnp.exp2(c_end - cum)).astype(BF16)
        state[h] = st * jnp.exp2(c_end) + _dot_tn(v, kw)
        return carry

    for h in range(HG_HEADS):
        head(h, 0)


def _gla_place_matrix(bwd):
    pm = np.zeros((HG_SUB * HG_DK, SCAN_L), np.float32)
    for dl in range(HG_SUB):
        pm[dl * HG_DK:(dl + 1) * HG_DK, dl if bwd else (SCAN_L - dl) % SCAN_L] = 1.0
    return jnp.asarray(pm, BF16)


def gla_scan(proj, llb, l1m, omlb, *, bsz, ctx_len, bwd):
    m = proj.shape[0]
    t_all = m // bsz
    nc = t_all // SCAN_L
    ncc = ctx_len // SCAN_L
    L = SCAN_L
    d = int(bwd)

    def row(bi, c):
        return bi * nc + _scan_chunk_index(bwd, c, ncc, nc)

    vec = pl.BlockSpec((None, 1, D_MODEL), lambda bi, c: (d, 0, 0))
    place = _gla_place_matrix(bwd)
    return pl.pallas_call(
        functools.partial(_gla_scan_kernel, bwd=bwd),
        grid=(bsz, nc),
        in_specs=[pl.BlockSpec((L, D_MODEL), lambda bi, c: (row(bi, c), 0)),
                  pl.BlockSpec((L, D_MODEL), lambda bi, c: (row(bi, c), 1 + d)),
                  pl.BlockSpec((L, D_MODEL), lambda bi, c: (row(bi, c), 3)),
                  vec, vec, vec,
                  pl.BlockSpec(place.shape, lambda bi, c: (0, 0))],
        out_specs=pl.BlockSpec((L, D_MODEL), lambda bi, c: (row(bi, c), 0)),
        out_shape=jax.ShapeDtypeStruct((m, D_MODEL), BF16),
        scratch_shapes=[pltpu.VMEM((HG_HEADS, HG_DK, HG_DK), F32)],
        compiler_params=_cparams(("arbitrary", "arbitrary")),
        name="gla_scan_bwd" if bwd else "gla_scan_fwd",
    )(proj, proj, proj, llb, l1m, omlb, place)


def _hgrn_finish(of_ref, ob_ref, gate_ref, g_ref):
    o = of_ref[...].astype(F32) + ob_ref[...].astype(F32)
    parts = []
    for h in range(HG_HEADS):
        sl = o[:, h * HG_DK:(h + 1) * HG_DK]
        parts.append(sl * lax.rsqrt(jnp.mean(sl * sl, axis=-1, keepdims=True) + EPS))
    return jnp.concatenate(parts, axis=1) * g_ref[...] * _silu(gate_ref[...].astype(F32))


def _attn_qkv_kernel(x_ref, g_ref, mod_ref, w_ref, cos_ref, sin_ref, qg_ref, kg_ref, bd_ref, rot_ref,
                     q_ref, k_ref, v_ref, *, shift_idx, scale_idx):
    h = _norm_mod(x_ref[...], g_ref[...], mod_ref, shift_idx, scale_idx).astype(BF16)
    nq = ATT_HEADS * ATT_HD
    nk = ATT_KV * ATT_HD

    def norm_rope(a, gain, width, out_scale):
        ss = _dot((a * a).astype(BF16), bd_ref[:width, :width])
        an = a * lax.rsqrt(ss * (1.0 / ATT_HD) + EPS) * gain
        rot = _dot(an.astype(BF16), rot_ref[:width, :width])
        cos = jnp.concatenate([cos_ref[...]] * (width // 128), axis=1)
        sin = jnp.concatenate([sin_ref[...]] * (width // 128), axis=1)
        return (an * cos + rot * sin) * out_scale

    q = _dot(h, w_ref[:, :nq])
    q_ref[...] = norm_rope(q, qg_ref[...], nq, ATT_HD ** -0.5 * math.log2(math.e)).astype(q_ref.dtype)
    k = _dot(h, w_ref[:, nq:nq + nk])
    k_ref[...] = norm_rope(k, kg_ref[...], nk, 1.0).astype(k_ref.dtype)
    v_ref[...] = _dot(h, w_ref[:, nq + nk:]).astype(v_ref.dtype)


def attn_qkv(x, g, rowmod, w, cos_t, sin_t, q_g, k_g, bd, rot, *, bsz, shift_idx, scale_idx):
    m = x.shape[0]
    tm = ROW_GROUP
    tpb = (m // bsz) // tm
    nq = ATT_HEADS * ATT_HD
    nk = ATT_KV * ATT_HD
    const = lambda shape: pl.BlockSpec(shape, lambda i: (0, 0))
    return pl.pallas_call(
        functools.partial(_attn_qkv_kernel, shift_idx=shift_idx, scale_idx=scale_idx),
        grid=(m // tm,),
        in_specs=[pl.BlockSpec((tm, D_MODEL), lambda i: (i, 0)),
                  const((1, D_MODEL)),
                  pl.BlockSpec((1, 6, D_MODEL), lambda i: (i, 0, 0)),
                  const(w.shape),
                  pl.BlockSpec((tm, 128), lambda i: (i % tpb, 0)),
                  pl.BlockSpec((tm, 128), lambda i: (i % tpb, 0)),
                  const((1, nq)), const((1, nk)), const(bd.shape), const(rot.shape)],
        out_specs=[pl.BlockSpec((tm, nq), lambda i: (i, 0)),
                   pl.BlockSpec((tm, nk), lambda i: (i, 0)),
                   pl.BlockSpec((tm, nk), lambda i: (i, 0))],
        out_shape=[jax.ShapeDtypeStruct((m, nq), BF16),
                   jax.ShapeDtypeStruct((m, nk), BF16),
                   jax.ShapeDtypeStruct((m, nk), BF16)],
        compiler_params=_cparams(("parallel",)),
        name="attn_qkv",
    )(x, g.reshape(1, D_MODEL), rowmod, w, cos_t, sin_t, q_g, k_g, bd, rot)


def _attn_kernel(q_ref, k_ref, v_ref, o_ref, kaug, vaug, kmax2, *, ctx_len):
    qt = pl.program_id(2)
    tq = q_ref.shape[0]
    hd = ATT_HD
    lane = lax.broadcasted_iota(jnp.int32, (1, 2 * hd), 1)
    n_r = q_ref.shape[1] // (2 * hd)
    aug_lane = (hd, 0)
    ii = lax.broadcasted_iota(jnp.int32, (2 * hd, 2 * hd), 0)
    jj = lax.broadcasted_iota(jnp.int32, (2 * hd, 2 * hd), 1)
    half_sum = jnp.where((ii // hd) == (jj // hd), 1.0, 0.0).astype(BF16)

    @pl.when(qt == 0)
    def _():
        v = v_ref[...]
        k = k_ref[...]
        one = jnp.ones((), BF16)
        vaug[0] = jnp.where(lane < hd, v, one)
        vaug[1] = jnp.where(lane < hd, one, v)
        kaug[0] = jnp.where(lane == aug_lane[0], one, k)
        kaug[1] = jnp.where(lane == aug_lane[1], one, k)
        kf = k.astype(F32)
        kn2 = _dot((kf * kf).astype(BF16), half_sum)
        kmax2[...] = jnp.broadcast_to(jnp.max(kn2, axis=0, keepdims=True), kmax2.shape)

    def queries(j):
        sel = (lane // hd) == j
        return jnp.concatenate(
            [jnp.where(sel, q_ref[:, r * 128:(r + 1) * 128], jnp.zeros((), BF16)) for r in range(n_r)], axis=0)

    def finish(outs):
        o = jnp.where(lane < hd, outs[0], outs[1])
        for r in range(n_r):
            o_ref[:, r * 128:(r + 1) * 128] = o[r * tq:(r + 1) * tq].astype(o_ref.dtype)

    def attend_exact(n_keys):
        outs = []
        for j in range(2):
            s = _dot_nt(queries(j), k_ref[:n_keys, :])
            mx = jnp.max(s, axis=-1, keepdims=True)
            p = jnp.exp2(s - mx).astype(BF16)
            ov = _dot(p, vaug[j, :n_keys, :])
            outs.append(ov / pltpu.roll(ov, hd, 1))
        finish(outs)

    def attend_shifted(n_keys, shifts):
        outs = []
        for j in range(2):
            qa = jnp.where(lane == aug_lane[j], (-shifts[j]).astype(BF16), queries(j))
            p = jnp.exp2(_dot_nt(qa, kaug[j, :n_keys, :])).astype(BF16)
            ov = _dot(p, vaug[j, :n_keys, :])
            outs.append(ov / pltpu.roll(ov, hd, 1))
        finish(outs)

    def attend(n_keys):
        shifts = []
        ones = jnp.ones((2 * hd, 2 * hd), BF16)
        for j in range(2):
            qf = queries(j).astype(F32)
            qn2 = _dot((qf * qf).astype(BF16), ones)
            shifts.append(jnp.sqrt(qn2 * kmax2[0:1, j * hd:j * hd + 1]) * 1.02)
        worst = jnp.max(jnp.maximum(shifts[0], shifts[1]))
        small = worst < ATT_SHIFT_MAX

        @pl.when(small)
        def _():
            attend_shifted(n_keys, shifts)

        @pl.when(jnp.logical_not(small))
        def _():
            attend_exact(n_keys)

    n_ctx_tiles = ctx_len // tq

    @pl.when(qt < n_ctx_tiles)
    def _():
        attend(ctx_len)

    @pl.when(qt >= n_ctx_tiles)
    def _():
        attend(k_ref.shape[0])


def attention(q, k, v, *, bsz, ctx_len):
    m = q.shape[0]
    t_all = m // bsz
    tq = ATT_TQ
    nqt = t_all // tq
    n_pairs = ATT_KV // 2
    qw = q.shape[1] // n_pairs
    return pl.pallas_call(
        functools.partial(_attn_kernel, ctx_len=ctx_len),
        grid=(bsz, n_pairs, nqt),
        in_specs=[pl.BlockSpec((tq, qw), lambda bi, p, t: (bi * nqt + t, p)),
                  pl.BlockSpec((t_all, 2 * ATT_HD), lambda bi, p, t: (bi, p)),
                  pl.BlockSpec((t_all, 2 * ATT_HD), lambda bi, p, t: (bi, p))],
        out_specs=pl.BlockSpec((tq, qw), lambda bi, p, t: (bi * nqt + t, p)),
        out_shape=jax.ShapeDtypeStruct(q.shape, BF16),
        scratch_shapes=[pltpu.VMEM((2, t_all, 2 * ATT_HD), BF16),
                        pltpu.VMEM((2, t_all, 2 * ATT_HD), BF16),
                        pltpu.VMEM((8, 2 * ATT_HD), F32)],
        compiler_params=_cparams(("arbitrary", "arbitrary", "arbitrary")),
        name="attention",
    )(q, k, v)


def _identity_pro(a_ref):
    return a_ref[...]


def _mlstm_qkv_kernel(xc_ref, xm_ref, wq_ref, wk_ref, wv_ref, wg_ref, bg_ref, q_ref, k_ref, v_ref, gate_ref):
    gates = jnp.zeros(gate_ref.shape, F32) + bg_ref[...]
    for which, (src, w_ref, o_ref) in enumerate(((xc_ref, wq_ref, q_ref), (xc_ref, wk_ref, k_ref),
                                                 (xm_ref, wv_ref, v_ref))):
        for h in range(ML_HEADS):
            hs = slice(h * ML_DH, (h + 1) * ML_DH)
            r = _dot(src[:, hs], w_ref[h]).astype(BF16)
            o_ref[:, hs] = r
            gates = gates + _dot(r, wg_ref[which * ML_INNER + h * ML_DH:which * ML_INNER + (h + 1) * ML_DH, :])
    gate_ref[...] = gates


def mlstm_qkv(xc, up, wq, wk, wv, wg, bg, *, tm=PROJ_TM):
    m = xc.shape[0]
    ng = wg.shape[1]
    const = lambda shape: pl.BlockSpec(shape, lambda i: (0,) * len(shape))
    row = pl.BlockSpec((tm, ML_INNER), lambda i: (i, 0))
    return pl.pallas_call(
        _mlstm_qkv_kernel,
        grid=(m // tm,),
        in_specs=[row, row, const(wq.shape), const(wk.shape), const(wv.shape), const(wg.shape), const((1, ng))],
        out_specs=[row, row, row, pl.BlockSpec((tm, ng), lambda i: (i, 0))],
        out_shape=[jax.ShapeDtypeStruct((m, ML_INNER), BF16)] * 3 + [jax.ShapeDtypeStruct((m, ng), F32)],
        compiler_params=_cparams(("parallel",)),
        name="mlstm_qkv",
    )(xc, up, wq, wk, wv, wg, bg.reshape(1, ng))


def _mlstm_scan_kernel(q_ref, k_ref, v_ref, gt_ref, gtt_ref, h_ref, c_st, n_st, m_st, *, bwd):
    ci = pl.program_id(1)
    L = ML_L
    nh = ML_HEADS
    scale = ML_DH ** -0.5

    @pl.when(ci == 0)
    def _():
        c_st[...] = jnp.zeros_like(c_st)
        n_st[...] = jnp.zeros_like(n_st)
        m_st[...] = jnp.full(m_st.shape, NEG_BIG, F32)

    _, _, visible, tri, trit = _scan_masks(L, bwd)
    gt = gt_ref[...]
    gtt = gtt_ref[...]
    li_c, li_r = gt[:, :nh], gtt[:nh, :]
    lf_c, lf_r = _log_sigmoid(gt[:, nh:]), _log_sigmoid(gtt[nh:, :])
    cum_c = _dot_sel_l(tri, lf_c)
    cum_r = _dot_sel_r(lf_r, trit)
    end_c = jnp.sum(lf_c, axis=0, keepdims=True)
    for h in range(nh):
        hs = slice(h * ML_DH, (h + 1) * ML_DH)
        q = q_ref[:, hs]
        kb = k_ref[:, hs]
        k = kb.astype(F32)
        v = v_ref[:, hs]
        m_prev = m_st[h:h + 1, 0:1]
        cum_end = end_c[:, h:h + 1]
        dmat = jnp.where(visible, cum_c[:, h:h + 1] - cum_r[h:h + 1, :] + li_r[h:h + 1, :], -jnp.inf)
        inter = cum_c[:, h:h + 1] + m_prev
        m_t = jnp.maximum(inter, jnp.max(dmat, axis=1, keepdims=True))
        w = jnp.exp(dmat - m_t)
        w_c = jnp.exp(inter - m_t)
        qk = _dot_nt(q, kb) * scale * w
        cmat = c_st[h]
        num = _dot(qk.astype(BF16), v) + w_c * _dot(q, cmat.astype(BF16))
        qn = jnp.sum(q.astype(F32) * n_st[h:h + 1, :], axis=1, keepdims=True)
        den = jnp.sum(qk, axis=1, keepdims=True) + w_c * qn
        hv = num / jnp.maximum(jnp.abs(den), jnp.exp(-m_t))
        h_ref[:, hs] = hv.astype(h_ref.dtype)
        wend_c = cum_end - cum_c[:, h:h + 1] + li_c[:, h:h + 1]
        wend_r = cum_end - cum_r[h:h + 1, :] + li_r[h:h + 1, :]
        m_new = jnp.maximum(cum_end + m_prev, jnp.max(wend_r, axis=1, keepdims=True))
        a_old = jnp.exp(cum_end + m_prev - m_new)
        e_c = jnp.exp(wend_c - m_new) * scale
        e_r = jnp.exp(wend_r - m_new) * scale
        c_st[h] = a_old * cmat + _dot_tn((k * e_c).astype(BF16), v)
        e_r8 = jnp.broadcast_to(e_r, (8, L)).astype(BF16)
        n_st[h:h + 1, :] = a_old * n_st[h:h + 1, :] + _dot(e_r8, kb)[0:1, :]
        m_st[h:h + 1, :] = jnp.broadcast_to(m_new, (1, m_st.shape[1]))


def mlstm_scan(q, k, v, gt_dir, gtt_dir, *, bsz, ctx_len, bwd):
    m = q.shape[0]
    t_all = m // bsz
    nc = t_all // ML_L
    ncc = ctx_len // ML_L
    L, nh = ML_L, ML_HEADS
    d = int(bwd)

    def row(bi, c):
        return bi * nc + _scan_chunk_index(bwd, c, ncc, nc)

    blk = pl.BlockSpec((L, ML_INNER), lambda bi, c: (row(bi, c), 0))
    return pl.pallas_call(
        functools.partial(_mlstm_scan_kernel, bwd=bwd),
        grid=(bsz, nc),
        in_specs=[blk, blk, blk,
                  pl.BlockSpec((None, L, 2 * nh), lambda bi, c: (d, row(bi, c), 0)),
                  pl.BlockSpec((None, 2 * nh, L), lambda bi, c: (d, 0, row(bi, c)))],
        out_specs=pl.BlockSpec((L, ML_INNER), lambda bi, c: (row(bi, c), 0)),
        out_shape=jax.ShapeDtypeStruct((m, ML_INNER), BF16),
        scratch_shapes=[pltpu.VMEM((nh, ML_DH, ML_DH), F32),
                        pltpu.VMEM((8, ML_DH), F32),
                        pltpu.VMEM((8, 128), F32)],
        compiler_params=_cparams(("arbitrary", "arbitrary")),
        name="mlstm_scan_bwd" if bwd else "mlstm_scan_fwd",
    )(q, k, v, gt_dir, gtt_dir)


def _mlstm_finish(hf_ref, hb_ref, xc_ref, z_ref, skip_ref, g_ref):
    hsum = hf_ref[...].astype(F32) + hb_ref[...].astype(F32)
    parts = []
    for h in range(ML_HEADS):
        sl = hsum[:, h * ML_DH:(h + 1) * ML_DH]
        parts.append(sl * lax.rsqrt(jnp.mean(sl * sl, axis=-1, keepdims=True) + EPS))
    hn = jnp.concatenate(parts, axis=1) * g_ref[...]
    return (hn + skip_ref[...] * xc_ref[...].astype(F32)) * _silu(z_ref[...].astype(F32))


def _ffn_kernel(x_ref, g_ref, mod_ref, w1_ref, w3_ref, w2_ref, o_ref, h_sc, acc_sc, *, shift_idx, scale_idx, gate_idx):
    f = pl.program_id(1)

    @pl.when(f == 0)
    def _():
        h_sc[...] = _norm_mod(x_ref[...], g_ref[...], mod_ref, shift_idx, scale_idx).astype(BF16)
        acc_sc[...] = jnp.zeros_like(acc_sc)

    h = h_sc[...]
    a = _silu(_dot(h, w1_ref[...])) * _dot(h, w3_ref[...])
    acc_sc[...] += _dot(a.astype(BF16), w2_ref[...])

    @pl.when(f == pl.num_programs(1) - 1)
    def _():
        o_ref[...] = _rows_gate_residual(x_ref[...], acc_sc[...], mod_ref, gate_idx)


def dense_ffn(x, g, rowmod, w1, w3, w2, layer, *, tm=1024, tf=512):
    m = x.shape[0]
    tm = _pick_tile(m, tm)
    gm = tm // ROW_GROUP
    nf = D_FF // tf
    return pl.pallas_call(
        functools.partial(_ffn_kernel, shift_idx=3, scale_idx=4, gate_idx=5),
        grid=(m // tm, nf),
        in_specs=[pl.BlockSpec((tm, D_MODEL), lambda i, f: (i, 0)),
                  pl.BlockSpec((1, D_MODEL), lambda i, f: (0, 0)),
                  pl.BlockSpec((gm, 6, D_MODEL), lambda i, f: (i, 0, 0)),
                  pl.BlockSpec((None, D_MODEL, tf), lambda i, f: (layer, 0, f)),
                  pl.BlockSpec((None, D_MODEL, tf), lambda i, f: (layer, 0, f)),
                  pl.BlockSpec((None, tf, D_MODEL), lambda i, f: (layer, f, 0))],
        out_specs=pl.BlockSpec((tm, D_MODEL), lambda i, f: (i, 0)),
        out_shape=jax.ShapeDtypeStruct((m, D_MODEL), F32),
        scratch_shapes=[pltpu.VMEM((tm, D_MODEL), BF16), pltpu.VMEM((tm, D_MODEL), F32)],
        compiler_params=_cparams(("parallel", "arbitrary")),
        name="dense_ffn",
    )(x, g.reshape(1, D_MODEL), rowmod, w1, w3, w2)


def _router_kernel(x_ref, g_ref, mod_ref, wr_ref, h_ref, route_ref, cnt_ref, cnt_sc, *, shift_idx, scale_idx, n_real):
    i = pl.program_id(0)

    @pl.when(i == 0)
    def _():
        cnt_sc[...] = jnp.zeros_like(cnt_sc)

    @pl.when(i < n_real)
    def _():
        h = _norm_mod(x_ref[...], g_ref[...], mod_ref, shift_idx, scale_idx)
        h_ref[...] = h.astype(h_ref.dtype)
        logits = _dot_hi(h, wr_ref[...])
        tm = logits.shape[0]
        lane = lax.broadcasted_iota(jnp.int32, logits.shape, 1)
        logits = jnp.where(lane < N_EXPERTS, logits, -jnp.inf)
        m1 = jnp.max(logits, axis=-1, keepdims=True)
        i1 = jnp.min(jnp.where(logits == m1, lane, 128), axis=-1, keepdims=True)
        rest = jnp.where(lane == i1, -jnp.inf, logits)
        m2 = jnp.max(rest, axis=-1, keepdims=True)
        i2 = jnp.min(jnp.where(rest == m2, lane, 128), axis=-1, keepdims=True)
        e2 = jnp.exp(m2 - m1)
        w1 = 1.0 / (1.0 + e2)
        w2 = e2 / (1.0 + e2)
        chosen = jnp.where(lane == i1, 1.0, jnp.where(lane == i2, 1.0, 0.0))
        ii = lax.broadcasted_iota(jnp.int32, (tm, tm), 0)
        jj = lax.broadcasted_iota(jnp.int32, (tm, tm), 1)
        strict = jnp.where(ii > jj, 1.0, 0.0).astype(BF16)
        prefix = _dot(strict, chosen.astype(BF16)) + cnt_sc[0:1, :]
        r1 = jnp.sum(jnp.where(lane == i1, prefix, 0.0), axis=-1, keepdims=True)
        r2 = jnp.sum(jnp.where(lane == i2, prefix, 0.0), axis=-1, keepdims=True)
        cnt_sc[...] = cnt_sc[...] + jnp.sum(chosen, axis=0, keepdims=True)
        route = jnp.zeros(logits.shape, F32)
        for col, val in enumerate((i1.astype(F32), i2.astype(F32), w1, w2, r1, r2)):
            route = jnp.where(lane == col, val, route)
        route_ref[...] = route

    @pl.when(i >= n_real)
    def _():
        h_ref[...] = jnp.zeros_like(h_ref)
        route_ref[...] = jnp.zeros_like(route_ref)

    cnt_ref[...] = cnt_sc[...]


def moe_router(x, g, rowmod, wr_pad, p_rows, *, tm=PROJ_TM):
    m = x.shape[0]
    n_real = m // tm
    gm = tm // ROW_GROUP
    clamp = lambda i: jnp.minimum(i, n_real - 1)
    return pl.pallas_call(
        functools.partial(_router_kernel, shift_idx=3, scale_idx=4, n_real=n_real),
        grid=(p_rows // tm,),
        in_specs=[pl.BlockSpec((tm, D_MODEL), lambda i: (clamp(i), 0)),
                  pl.BlockSpec((1, D_MODEL), lambda i: (0, 0)),
                  pl.BlockSpec((gm, 6, D_MODEL), lambda i: (clamp(i), 0, 0)),
                  pl.BlockSpec(wr_pad.shape, lambda i: (0, 0))],
        out_specs=[pl.BlockSpec((tm, D_MODEL), lambda i: (i, 0)), pl.BlockSpec((tm, 128), lambda i: (i, 0)),
                   pl.BlockSpec((8, 128), lambda i: (0, 0))],
        out_shape=[jax.ShapeDtypeStruct((p_rows, D_MODEL), BF16), jax.ShapeDtypeStruct((p_rows, 128), F32),
                   jax.ShapeDtypeStruct((8, 128), F32)],
        scratch_shapes=[pltpu.VMEM((8, 128), F32)],
        compiler_params=_cparams(("arbitrary",)),
        name="moe_router",
    )(x, g.reshape(1, D_MODEL), rowmod, wr_pad)


def _moe_ffn_kernel(te_ref, nt_ref, h_ref, w1_ref, w3_ref, w2_ref, o_ref, acc_sc):
    i = pl.program_id(0)
    f = pl.program_id(1)
    live = i < nt_ref[0]

    @pl.when(f == 0)
    def _():
        acc_sc[...] = jnp.zeros_like(acc_sc)

    @pl.when(live)
    def _():
        h = h_ref[...]
        a = _silu(_dot(h, w1_ref[...].astype(BF16))) * _dot(h, w3_ref[...].astype(BF16))
        acc_sc[...] += _dot(a.astype(BF16), w2_ref[...].astype(BF16))

    @pl.when(f == pl.num_programs(1) - 1)
    def _():
        o_ref[...] = acc_sc[...].astype(o_ref.dtype)


def moe_grouped_ffn(h_sorted, tile_expert, n_tiles, w1, w3, w2, layer, *, tm=MOE_TM, tf=512):
    p = h_sorted.shape[0]
    nf = D_FF // tf
    grid_spec = pltpu.PrefetchScalarGridSpec(
        num_scalar_prefetch=2,
        grid=(p // tm, nf),
        in_specs=[pl.BlockSpec((tm, D_MODEL), lambda i, f, te, nt: (i, 0)),
                  pl.BlockSpec((None, None, D_MODEL, tf), lambda i, f, te, nt: (layer, te[i], 0, f)),
                  pl.BlockSpec((None, None, D_MODEL, tf), lambda i, f, te, nt: (layer, te[i], 0, f)),
                  pl.BlockSpec((None, None, tf, D_MODEL), lambda i, f, te, nt: (layer, te[i], f, 0))],
        out_specs=pl.BlockSpec((tm, D_MODEL), lambda i, f, te, nt: (i, 0)),
        scratch_shapes=[pltpu.VMEM((tm, D_MODEL), F32)])
    return pl.pallas_call(
        _moe_ffn_kernel,
        grid_spec=grid_spec,
        out_shape=jax.ShapeDtypeStruct((p, D_MODEL), BF16),
        compiler_params=_cparams(("arbitrary", "arbitrary")),
        name="moe_grouped_ffn",
    )(tile_expert, n_tiles, h_sorted, w1, w3, w2)


def _moe_combine_kernel(x_ref, ya_ref, yb_ref, route_ref, mod_ref, o_ref, *, gate_idx):
    r = route_ref[...]
    y = r[:, 2:3] * ya_ref[...].astype(F32) + r[:, 3:4] * yb_ref[...].astype(F32)
    o_ref[...] = _rows_gate_residual(x_ref[...], y, mod_ref, gate_idx)


def moe_combine(x, ya, yb, route, rowmod, *, latent_only=None):
    m = x.shape[0]
    tm = ROW_GROUP
    if latent_only is None:
        n_out, src = m // tm, (lambda i: i)
    else:
        bsz, ctx_len = latent_only
        ctx_tiles = ctx_len // tm
        lat_tiles = m // bsz // tm - ctx_tiles
        n_out, src = bsz * lat_tiles, (lambda i: i + (i // lat_tiles + 1) * ctx_tiles)
    row = pl.BlockSpec((tm, D_MODEL), lambda i: (src(i), 0))
    return pl.pallas_call(
        functools.partial(_moe_combine_kernel, gate_idx=5),
        grid=(n_out,),
        in_specs=[row, row, row, pl.BlockSpec((tm, 128), lambda i: (src(i), 0)),
                  pl.BlockSpec((1, 6, D_MODEL), lambda i: (src(i), 0, 0))],
        out_specs=pl.BlockSpec((tm, D_MODEL), lambda i: (i, 0)),
        out_shape=jax.ShapeDtypeStruct((n_out * tm, D_MODEL), F32),
        compiler_params=_cparams(("parallel",)),
        name="moe_combine",
    )(x, ya, yb, route, rowmod)


def moe_ffn(x, g, rowmod, w_router, w1, w3, w2, layer, latent_only=None):
    m = x.shape[0]
    tm = MOE_TM
    n_tiles_max = (2 * m) // tm + N_EXPERTS
    p = n_tiles_max * tm
    wr_pad = jnp.zeros((D_MODEL, 128), F32).at[:, :N_EXPERTS].set(w_router)
    h, route, cnt = moe_router(x, g, rowmod, wr_pad, p)
    experts = jnp.arange(N_EXPERTS, dtype=jnp.int32)
    counts = cnt[0, :N_EXPERTS].astype(jnp.int32)
    tiles_per = (counts + tm - 1) // tm
    tile_end = jnp.cumsum(tiles_per)
    grp_start = (tile_end - tiles_per) * tm
    cnt_start = jnp.cumsum(counts) - counts
    e12 = route[:m, 0:2].astype(jnp.int32)
    r12 = route[:m, 4:6].astype(jnp.int32)
    pos12 = jnp.sum(jnp.where(e12[:, :, None] == experts, grp_start, 0), axis=-1) + r12
    order = jnp.argsort(e12.reshape(-1), stable=True).astype(jnp.int32)
    tile_expert = jnp.minimum(
        jnp.sum(jnp.arange(n_tiles_max, dtype=jnp.int32)[:, None] >= tile_end[None, :], axis=1), N_EXPERTS - 1
    ).astype(jnp.int32)
    n_tiles = tile_end[-1:].astype(jnp.int32)
    rank = jnp.arange(p, dtype=jnp.int32) - jnp.repeat(grp_start[tile_expert], tm)
    in_use = rank < jnp.repeat(counts[tile_expert], tm)
    sorted_idx = jnp.clip(jnp.repeat(cnt_start[tile_expert], tm) + rank, 0, 2 * m - 1)
    src_token = jnp.where(in_use, jnp.take(order, sorted_idx, mode="clip") // 2, 0)
    h_sorted = jnp.take(h, src_token, axis=0, mode="clip")
    y_sorted = moe_grouped_ffn(h_sorted, tile_expert, n_tiles, w1, w3, w2, layer)
    ya = jnp.take(y_sorted, pos12[:, 0], axis=0, mode="clip")
    yb = jnp.take(y_sorted, pos12[:, 1], axis=0, mode="clip")
    return moe_combine(x, ya, yb, route, rowmod, latent_only=latent_only)


def _dir_split(a, n):
    m = a.shape[0]
    a3 = a.reshape(m, 2, n)
    return jnp.transpose(a3, (1, 0, 2)), jnp.transpose(a3, (1, 2, 0))


def ssd_layer(x, g, rowmod, w_in, conv_w, conv_b, dt_bias, a_log, d_skip, norm_g, w_out, *, bsz, ctx_len):
    m = x.shape[0]
    n_main = SSD_INNER + SSD_INNER + 2 * SSD_GROUPS * SSD_N
    zx, dt_raw = in_projection(x, g, rowmod, [w_in[:, :n_main].astype(BF16), w_in[:, n_main:].astype(BF16)],
                               [BF16, F32], shift_idx=0, scale_idx=1)
    u = dwconv_silu(zx, SSD_INNER, n_main - SSD_INNER, conv_w, conv_b, bsz=bsz, ctx_len=ctx_len)
    dt_dir, dtt_dir = _dir_split(dt_raw, SSD_HEADS)
    yf = ssd_scan(u, dt_dir, dtt_dir, dt_bias, a_log, bsz=bsz, ctx_len=ctx_len, bwd=False)
    yb = ssd_scan(u, dt_dir, dtt_dir, dt_bias, a_log, bsz=bsz, ctx_len=ctx_len, bwd=True)
    tm = PROJ_TM
    d_x = jnp.repeat(d_skip.astype(F32), SSD_P).reshape(1, SSD_INNER)
    specs = [pl.BlockSpec((tm, SSD_INNER), lambda i: (i, 0)),
             pl.BlockSpec((tm, SSD_INNER), lambda i: (i, 0)),
             pl.BlockSpec((tm, SSD_INNER), lambda i: (i, 0)),
             pl.BlockSpec((tm, SSD_INNER), lambda i: (i, 0)),
             pl.BlockSpec((1, SSD_INNER), lambda i: (0, 0)),
             pl.BlockSpec((1, SSD_INNER), lambda i: (0, 0))]
    return out_projection(_ssd_finish, [yf, yb, u, zx, d_x, norm_g.reshape(1, SSD_INNER)], specs,
                          w_out.astype(BF16), x, rowmod, gate_idx=2, tm=tm)


def hgrn_layer(x, g, rowmod, w_in, lb, norm_g, w_out, *, bsz, ctx_len):
    proj, = in_projection(x, g, rowmod, [w_in.astype(BF16)], [BF16], shift_idx=0, scale_idx=1)
    lb = lb.astype(F32).reshape(2, 1, D_MODEL)
    lbs = (jnp.log(lb), jnp.log1p(-lb), 1.0 - lb)
    of = gla_scan(proj, *lbs, bsz=bsz, ctx_len=ctx_len, bwd=False)
    ob = gla_scan(proj, *lbs, bsz=bsz, ctx_len=ctx_len, bwd=True)
    tm = PROJ_TM
    specs = [pl.BlockSpec((tm, D_MODEL), lambda i: (i, 0)),
             pl.BlockSpec((tm, D_MODEL), lambda i: (i, 0)),
             pl.BlockSpec((tm, D_MODEL), lambda i: (i, 4)),
             pl.BlockSpec((1, D_MODEL), lambda i: (0, 0))]
    return out_projection(_hgrn_finish, [of, ob, proj, norm_g.reshape(1, D_MODEL)], specs,
                          w_out.astype(BF16), x, rowmod, gate_idx=2, tm=tm)


def _attn_head_perm():
    r_per = ATT_HEADS // ATT_KV
    heads = [(2 * p + j) * r_per + r for p in range(ATT_KV // 2) for r in range(r_per) for j in range(2)]
    return np.concatenate([np.arange(h * ATT_HD, (h + 1) * ATT_HD) for h in heads])


def _rope_tables(seq_len, ctx_len, grid_w):
    rows = seq_len // grid_w
    row = jnp.repeat(jnp.arange(rows, dtype=F32), grid_w)
    col = jnp.tile(jnp.arange(grid_w, dtype=F32), rows)
    inv = ROPE_THETA ** (-jnp.arange(ROPE_FREQS, dtype=F32) / ROPE_FREQS)
    ang_r = row[:, None] * inv
    ang_c = col[:, None] * inv
    ang = jnp.concatenate([ang_r, ang_r, ang_c, ang_c], axis=-1)
    cos = jnp.concatenate([jnp.ones((ctx_len, ATT_HD), F32), jnp.cos(ang)], axis=0)
    sin = jnp.concatenate([jnp.zeros((ctx_len, ATT_HD), F32), jnp.sin(ang)], axis=0)
    return jnp.tile(cos, (1, 2)), jnp.tile(sin, (1, 2))


def _rope_matrices():
    r64 = np.zeros((ATT_HD, ATT_HD), np.float32)
    fq = ROPE_FREQS
    for ax in range(2):
        o = ax * 2 * fq
        for i in range(fq):
            r64[o + fq + i, o + i] = -1.0
            r64[o + i, o + fq + i] = 1.0
    n = ATT_HEADS
    bd = np.kron(np.eye(n, dtype=np.float32), np.ones((ATT_HD, ATT_HD), np.float32))
    rot = np.kron(np.eye(n, dtype=np.float32), r64)
    return jnp.asarray(bd, BF16), jnp.asarray(rot, BF16)


def attn_layer(x, g, rowmod, w_qkv, q_g, k_g, w_o, *, bsz, ctx_len, grid_w):
    m = x.shape[0]
    seq_len = m // bsz - ctx_len
    perm = _attn_head_perm()
    nq = ATT_HEADS * ATT_HD
    w = jnp.concatenate([w_qkv[:, :nq][:, perm], w_qkv[:, nq:]], axis=1).astype(BF16)
    cos_t, sin_t = _rope_tables(seq_len, ctx_len, grid_w)
    bd, rot = _rope_matrices()
    qg = jnp.tile(q_g.astype(F32), ATT_HEADS).reshape(1, nq)
    kg = jnp.tile(k_g.astype(F32), ATT_KV).reshape(1, ATT_KV * ATT_HD)
    q, k, v = attn_qkv(x, g, rowmod, w, cos_t, sin_t, qg, kg, bd, rot, bsz=bsz, shift_idx=0, scale_idx=1)
    o = attention(q, k, v, bsz=bsz, ctx_len=ctx_len)
    tm = PROJ_TM
    specs = [pl.BlockSpec((tm, nq), lambda i: (i, 0))]
    return out_projection(_identity_pro, [o], specs, w_o[perm, :].astype(BF16), x, rowmod, gate_idx=2, tm=tm)


def mlstm_layer(x, g, rowmod, w_up, conv_w, conv_b, w_q, w_k, w_v, w_gate, b_gate, skip, norm_g, w_down,
                *, bsz, ctx_len):
    up, = in_projection(x, g, rowmod, [w_up.astype(BF16)], [BF16], shift_idx=0, scale_idx=1)
    xc = dwconv_silu(up, 0, ML_INNER, conv_w, conv_b, bsz=bsz, ctx_len=ctx_len)
    q, k, v, gates = mlstm_qkv(xc, up, w_q.astype(BF16), w_k.astype(BF16), w_v.astype(BF16),
                               w_gate.astype(BF16), b_gate)
    gt_dir, gtt_dir = _dir_split(gates, 2 * ML_HEADS)
    hf = mlstm_scan(q, k, v, gt_dir, gtt_dir, bsz=bsz, ctx_len=ctx_len, bwd=False)
    hb = mlstm_scan(q, k, v, gt_dir, gtt_dir, bsz=bsz, ctx_len=ctx_len, bwd=True)
    tm = PROJ_TM
    specs = [pl.BlockSpec((tm, ML_INNER), lambda i: (i, 0)),
             pl.BlockSpec((tm, ML_INNER), lambda i: (i, 0)),
             pl.BlockSpec((tm, ML_INNER), lambda i: (i, 0)),
             pl.BlockSpec((tm, ML_INNER), lambda i: (i, 1)),
             pl.BlockSpec((1, ML_INNER), lambda i: (0, 0)),
             pl.BlockSpec((1, ML_INNER), lambda i: (0, 0))]
    return out_projection(_mlstm_finish, [hf, hb, xc, up, skip.reshape(1, ML_INNER), norm_g.reshape(1, ML_INNER)],
                          specs, w_down.astype(BF16), x, rowmod, gate_idx=2, tm=tm)


def kernel(x, c, ctx, c_ctx, ada_w, ada_b, norm_g, ssd_w_in, ssd_conv_w, ssd_conv_b, ssd_dt_bias, ssd_a_log, ssd_d, ssd_norm_g, ssd_w_out, hgrn_w_in, hgrn_lb, hgrn_norm_g, hgrn_w_out, attn_w_qkv, attn_q_g, attn_k_g, attn_w_o, mlstm_w_up, mlstm_conv_w, mlstm_conv_b, mlstm_w_q, mlstm_w_k, mlstm_w_v, mlstm_w_gate, mlstm_b_gate, mlstm_skip, mlstm_norm_g, mlstm_w_down, ffn_w1, ffn_w3, ffn_w2, moe_router, moe_w1, moe_w3, moe_w2):
    bsz, seq_len, _ = x.shape
    ctx_len = ctx.shape[1]
    depth = ada_w.shape[0]
    grid_w = 64
    t_all = ctx_len + seq_len
    m = bsz * t_all
    xa = jnp.concatenate([ctx, x], axis=1).reshape(m, D_MODEL)
    c_pad = jnp.zeros((8, D_MODEL), F32).at[:bsz].set(c).at[bsz].set(c_ctx)
    groups_per_batch = t_all // ROW_GROUP
    ctx_groups = ctx_len // ROW_GROUP
    gidx = np.array([bsz if (gi % groups_per_batch) < ctx_groups else gi // groups_per_batch
                     for gi in range(m // ROW_GROUP)], np.int32)
    lb_all = jnp.cumsum(jax.nn.softmax(hgrn_lb.astype(F32), axis=1), axis=1)
    lb_all = lb_all - lb_all[:, :1]
    kw = dict(bsz=bsz, ctx_len=ctx_len)
    ffn_w = [w.astype(BF16) for w in (ffn_w1, ffn_w3, ffn_w2)]
    moe_w = (moe_w1, moe_w3, moe_w2)
    for i in range(depth):
        mod = ada_modulation(c_pad, ada_w, ada_b, i).reshape(8, 6, D_MODEL)
        rowmod = mod[gidx]
        kind, j = i % 4, i // 4
        if kind == 0:
            xa = ssd_layer(xa, norm_g[i, 0], rowmod, ssd_w_in[j], ssd_conv_w[j], ssd_conv_b[j], ssd_dt_bias[j],
                           ssd_a_log[j], ssd_d[j], ssd_norm_g[j], ssd_w_out[j], **kw)
        elif kind == 1:
            xa = hgrn_layer(xa, norm_g[i, 0], rowmod, hgrn_w_in[j], lb_all[:, i], hgrn_norm_g[j], hgrn_w_out[j], **kw)
        elif kind == 2:
            xa = attn_layer(xa, norm_g[i, 0], rowmod, attn_w_qkv[j], attn_q_g[j], attn_k_g[j], attn_w_o[j],
                            grid_w=grid_w, **kw)
        else:
            xa = mlstm_layer(xa, norm_g[i, 0], rowmod, mlstm_w_up[j], mlstm_conv_w[j], mlstm_conv_b[j], mlstm_w_q[j],
                             mlstm_w_k[j], mlstm_w_v[j], mlstm_w_gate[j], mlstm_b_gate[j], mlstm_skip[j],
                             mlstm_norm_g[j], mlstm_w_down[j], **kw)
        if i % 2 == 0:
            xa = dense_ffn(xa, norm_g[i, 1], rowmod, *ffn_w, i // 2)
        else:
            last = i == depth - 1
            xa = moe_ffn(xa, norm_g[i, 1], rowmod, moe_router[i // 2], *moe_w, i // 2,
                         latent_only=(bsz, ctx_len) if last else None)
            if last:
                return xa.reshape(bsz, seq_len, D_MODEL)
    return xa.reshape(bsz, t_all, D_MODEL)[:, ctx_len:]
```

```python
import functools
import math

import jax
import jax.numpy as jnp
import numpy as np
from jax import lax
from jax.experimental import pallas as pl
from jax.experimental.pallas import tpu as pltpu

F32 = jnp.float32
BF16 = jnp.bfloat16
HI = lax.Precision.HIGHEST

D_MODEL = 1024
EPS = 1e-6
ROW_GROUP = 256
PROJ_TM = 512
CONV_W = 5
NEG_BIG = -1e30
LOG2E = math.log2(math.e)
VMEM_LIMIT = 56 << 20

SSD_INNER = 2 * D_MODEL
SSD_P = 64
SSD_HEADS = SSD_INNER // SSD_P
SSD_N = 128
SSD_GROUPS = 8
SSD_GW = SSD_INNER // SSD_GROUPS
SCAN_L = 128

HG_HEADS = 8
HG_DK = 128
HG_SUB = 8

ATT_HEADS = 16
ATT_KV = 4
ATT_HD = 64
ROPE_THETA = 10000.0
ROPE_FREQS = ATT_HD // 4
ATT_TQ = 128
ATT_SHIFT_MAX = 60.0

ML_INNER = 2 * D_MODEL
ML_HEADS = 4
ML_DH = ML_INNER // ML_HEADS
ML_L = 256

D_FF = 7 * D_MODEL // 2
N_EXPERTS = 8
MOE_TM = 1024


def _cparams(sem):
    return pltpu.CompilerParams(dimension_semantics=sem, vmem_limit_bytes=VMEM_LIMIT)


def _dot(a, b):
    return jnp.dot(a, b, preferred_element_type=F32)


def _dot_hi(a, b):
    return jnp.dot(a, b, preferred_element_type=F32, precision=HI)


def _dot_nt(a, b):
    return lax.dot_general(a, b, (((1,), (1,)), ((), ())), preferred_element_type=F32)


def _dot_tn(a, b):
    return lax.dot_general(a, b, (((0,), (0,)), ((), ())), preferred_element_type=F32)


def _silu(x):
    return x * jax.nn.sigmoid(x)


def _log_sigmoid(x):
    return jnp.minimum(x, 0.0) - jnp.log1p(jnp.exp(-jnp.abs(x)))


def _softplus(x):
    return jnp.maximum(x, 0.0) + jnp.log1p(jnp.exp(-jnp.abs(x)))


def _pick_tile(m, pref):
    t = pref
    while m % t:
        t //= 2
    return t


def _rows_scale_shift(y, mod_ref, scale_idx, shift_idx):
    parts = []
    for gi in range(y.shape[0] // ROW_GROUP):
        sl = y[gi * ROW_GROUP:(gi + 1) * ROW_GROUP]
        parts.append(sl * (1.0 + mod_ref[gi, scale_idx:scale_idx + 1, :]) + mod_ref[gi, shift_idx:shift_idx + 1, :])
    return parts[0] if len(parts) == 1 else jnp.concatenate(parts, axis=0)


def _rows_gate_residual(x, acc, mod_ref, gate_idx):
    parts = []
    for gi in range(x.shape[0] // ROW_GROUP):
        sl = slice(gi * ROW_GROUP, (gi + 1) * ROW_GROUP)
        parts.append(x[sl] + mod_ref[gi, gate_idx:gate_idx + 1, :] * acc[sl])
    return parts[0] if len(parts) == 1 else jnp.concatenate(parts, axis=0)


def _norm_mod(x, g, mod_ref, shift_idx, scale_idx):
    y = x * lax.rsqrt(jnp.mean(x * x, axis=-1, keepdims=True) + EPS) * g
    return _rows_scale_shift(y, mod_ref, scale_idx, shift_idx)


def _split2(a):
    hi = a.astype(BF16)
    return hi, (a - hi.astype(F32)).astype(BF16)


def _dot_sel_l(sel, a):
    hi, lo = _split2(a)
    return _dot(sel, hi) + _dot(sel, lo)


def _dot_sel_r(a, sel):
    hi, lo = _split2(a)
    return _dot(hi, sel) + _dot(lo, sel)


def _scan_masks(n, bwd):
    ii = lax.broadcasted_iota(jnp.int32, (n, n), 0)
    jj = lax.broadcasted_iota(jnp.int32, (n, n), 1)
    visible = (ii <= jj) if bwd else (ii >= jj)
    visible_t = (ii >= jj) if bwd else (ii <= jj)
    tri = jnp.where(visible, 1.0, 0.0).astype(BF16)
    trit = jnp.where(visible_t, 1.0, 0.0).astype(BF16)
    return ii, jj, visible, tri, trit


def _ada_kernel(c_ref, w_ref, b_ref, o_ref):
    c = c_ref[...]
    o_ref[...] = _dot_hi(_silu(c), w_ref[...]) + b_ref[...]


def ada_modulation(c_pad, w, b, layer):
    n = w.shape[2]
    tn = 1024
    return pl.pallas_call(
        _ada_kernel,
        grid=(n // tn,),
        in_specs=[pl.BlockSpec(c_pad.shape, lambda j: (0, 0)),
                  pl.BlockSpec((None, D_MODEL, tn), lambda j: (layer, 0, j)),
                  pl.BlockSpec((None, 1, tn), lambda j: (layer, 0, j))],
        out_specs=pl.BlockSpec((c_pad.shape[0], tn), lambda j: (0, j)),
        out_shape=jax.ShapeDtypeStruct((c_pad.shape[0], n), F32),
        compiler_params=_cparams(("arbitrary",)),
        name="ada_modulation",
    )(c_pad, w, b.reshape(b.shape[0], 1, n))


def _inproj_kernel(x_ref, g_ref, mod_ref, *refs, n_w, shift_idx, scale_idx, tn):
    h = _norm_mod(x_ref[...], g_ref[...], mod_ref, shift_idx, scale_idx).astype(BF16)
    for w_ref, o_ref in zip(refs[:n_w], refs[n_w:]):
        n = o_ref.shape[1]
        step = min(tn, n)
        for j in range(n // step):
            o_ref[:, j * step:(j + 1) * step] = _dot(h, w_ref[:, j * step:(j + 1) * step]).astype(o_ref.dtype)


def in_projection(x, g, rowmod, ws, out_dtypes, *, shift_idx, scale_idx, tm=PROJ_TM, tn=512):
    m = x.shape[0]
    tm = _pick_tile(m, tm)
    gm = tm // ROW_GROUP
    in_specs = [pl.BlockSpec((tm, D_MODEL), lambda i: (i, 0)),
                pl.BlockSpec((1, D_MODEL), lambda i: (0, 0)),
                pl.BlockSpec((gm, 6, D_MODEL), lambda i: (i, 0, 0))]
    in_specs += [pl.BlockSpec(w.shape, lambda i: (0, 0)) for w in ws]
    out_specs = [pl.BlockSpec((tm, w.shape[1]), lambda i: (i, 0)) for w in ws]
    out_shape = [jax.ShapeDtypeStruct((m, w.shape[1]), dt) for w, dt in zip(ws, out_dtypes)]
    return pl.pallas_call(
        functools.partial(_inproj_kernel, n_w=len(ws), shift_idx=shift_idx, scale_idx=scale_idx, tn=tn),
        grid=(m // tm,),
        in_specs=in_specs, out_specs=out_specs, out_shape=out_shape,
        compiler_params=_cparams(("parallel",)),
        name="in_projection",
    )(x, g.reshape(1, D_MODEL), rowmod, *ws)


def _outproj_kernel(*refs, n_pro, pro_fn, gate_idx):
    pro_refs = refs[:n_pro]
    w_ref, x_ref, mod_ref, o_ref = refs[n_pro:]
    a = pro_fn(*pro_refs).astype(BF16)
    acc = _dot(a, w_ref[...])
    o_ref[...] = _rows_gate_residual(x_ref[...], acc, mod_ref, gate_idx)


def out_projection(pro_fn, pro_args, pro_specs, w, x, rowmod, *, gate_idx, tm):
    m = x.shape[0]
    gm = tm // ROW_GROUP
    in_specs = list(pro_specs) + [pl.BlockSpec(w.shape, lambda i: (0, 0)),
                                  pl.BlockSpec((tm, D_MODEL), lambda i: (i, 0)),
                                  pl.BlockSpec((gm, 6, D_MODEL), lambda i: (i, 0, 0))]
    return pl.pallas_call(
        functools.partial(_outproj_kernel, n_pro=len(pro_args), pro_fn=pro_fn, gate_idx=gate_idx),
        grid=(m // tm,),
        in_specs=in_specs,
        out_specs=pl.BlockSpec((tm, D_MODEL), lambda i: (i, 0)),
        out_shape=jax.ShapeDtypeStruct((m, D_MODEL), F32),
        compiler_params=_cparams(("parallel",)),
        name="out_projection",
    )(*pro_args, w, x, rowmod)


def _conv_kernel(u_ref, w_ref, b_ref, o_ref, *, ctx_len):
    x = u_ref[...].astype(F32)
    t_all = x.shape[0]
    t = lax.broadcasted_iota(jnp.int32, (t_all, 1), 0)
    in_ctx = t < ctx_len
    pad = CONV_W // 2
    acc = b_ref[...] + w_ref[pad:pad + 1, :] * x
    for off in range(-pad, pad + 1):
        if off == 0:
            continue
        xs = pltpu.roll(x, (-off) % t_all, 0)
        tt = t + off
        valid = (tt >= 0) & (tt < t_all) & ((tt < ctx_len) == in_ctx)
        acc = acc + w_ref[pad + off:pad + off + 1, :] * jnp.where(valid, xs, 0.0)
    o_ref[...] = _silu(acc).astype(o_ref.dtype)


def dwconv_silu(u, col_off, width, w, b, *, bsz, ctx_len, tc=256):
    m = u.shape[0]
    t_all = m // bsz
    cb = col_off // tc
    return pl.pallas_call(
        functools.partial(_conv_kernel, ctx_len=ctx_len),
        grid=(bsz, width // tc),
        in_specs=[pl.BlockSpec((t_all, tc), lambda bi, j: (bi, j + cb)),
                  pl.BlockSpec((CONV_W, tc), lambda bi, j: (0, j)),
                  pl.BlockSpec((1, tc), lambda bi, j: (0, j))],
        out_specs=pl.BlockSpec((t_all, tc), lambda bi, j: (bi, j)),
        out_shape=jax.ShapeDtypeStruct((m, width), BF16),
        compiler_params=_cparams(("parallel", "parallel")),
        name="dwconv_silu",
    )(u, w, b.reshape(1, width))


def _scan_chunk_index(bwd, c, n_ctx_chunks, n_chunks):
    if not bwd:
        return c
    return jnp.where(c < n_ctx_chunks, n_ctx_chunks - 1 - c, n_chunks - 1 + n_ctx_chunks - c)


def _ssd_scan_kernel(x_ref, b_ref, c_ref, dt_ref, dtt_ref, bias_ref, biast_ref, alog_ref, alogt_ref, exp_ref,
                     y_ref, state, *, bwd):
    ci = pl.program_id(1)
    L = SCAN_L
    H = SSD_HEADS

    @pl.when(ci == 0)
    def _():
        state[...] = jnp.zeros_like(state)

    _, _, visible, tri, trit = _scan_masks(L, bwd)
    dt = _softplus(dt_ref[...] + bias_ref[...])
    dtt = _softplus(dtt_ref[...] + biast_ref[...])
    a_neg = -jnp.exp(alog_ref[...])
    a_negt = -jnp.exp(alogt_ref[...])
    dta = dt * a_neg
    a_col = _dot_sel_l(tri, dta)
    a_row = _dot_sel_r(dtt * a_negt, trit)
    a_end = jnp.sum(dta, axis=0, keepdims=True)
    per_head = jnp.concatenate([jnp.exp(a_col), jnp.exp(a_end - a_col) * dt,
                                jnp.broadcast_to(jnp.exp(a_end), (16, H))], axis=0)
    per_chan = _dot_sel_r(per_head, exp_ref[...])
    ea_x, w_x, eend_x = per_chan[:L], per_chan[L:2 * L], per_chan[2 * L:2 * L + 1]
    lane = lax.broadcasted_iota(jnp.int32, (L, 2 * SSD_P), 1)
    for g in range(SSD_GROUPS):
        gs = slice(g * SSD_GW, (g + 1) * SSD_GW)
        ns = slice(g * SSD_N, (g + 1) * SSD_N)
        bg = b_ref[:, ns]
        cg = c_ref[:, ns]
        xgb = x_ref[:, gs]
        cb = _dot_nt(cg, bg)
        sg = state[:, gs]
        y_inter = _dot(cg, sg.astype(BF16)) * ea_x[:, gs]
        pieces = []
        for pr in range(SSD_GW // (2 * SSD_P)):
            ys = []
            for q in range(2):
                h = g * (SSD_GW // SSD_P) + 2 * pr + q
                rel = a_col[:, h:h + 1] - a_row[h:h + 1, :]
                gm = cb * jnp.exp(jnp.where(visible, rel, NEG_BIG)) * dtt[h:h + 1, :]
                ys.append(_dot(gm.astype(BF16), xgb[:, pr * 2 * SSD_P:(pr + 1) * 2 * SSD_P]))
            pieces.append(jnp.where(lane < SSD_P, ys[0], ys[1]))
        y = jnp.concatenate(pieces, axis=1) + y_inter
        y_ref[:, gs] = y.astype(y_ref.dtype)
        xw = (xgb.astype(F32) * w_x[:, gs]).astype(BF16)
        state[:, gs] = sg * eend_x[:, gs] + _dot_tn(bg, xw)


def ssd_scan(u, dt_dir, dtt_dir, dt_bias, a_log, *, bsz, ctx_len, bwd):
    m = u.shape[0]
    t_all = m // bsz
    nc = t_all // SCAN_L
    ncc = ctx_len // SCAN_L
    L, H = SCAN_L, SSD_HEADS
    d = int(bwd)

    def row(bi, c):
        return bi * nc + _scan_chunk_index(bwd, c, ncc, nc)

    bn = SSD_INNER // (SSD_GROUPS * SSD_N)
    expand = jnp.asarray(np.kron(np.eye(H, dtype=np.float32), np.ones((1, SSD_P), np.float32)), BF16)
    return pl.pallas_call(
        functools.partial(_ssd_scan_kernel, bwd=bwd),
        grid=(bsz, nc),
        in_specs=[pl.BlockSpec((L, SSD_INNER), lambda bi, c: (row(bi, c), 0)),
                  pl.BlockSpec((L, SSD_GROUPS * SSD_N), lambda bi, c: (row(bi, c), bn)),
                  pl.BlockSpec((L, SSD_GROUPS * SSD_N), lambda bi, c: (row(bi, c), bn + 1)),
                  pl.BlockSpec((None, L, H), lambda bi, c: (d, row(bi, c), 0)),
                  pl.BlockSpec((None, H, L), lambda bi, c: (d, 0, row(bi, c))),
                  pl.BlockSpec((None, 1, H), lambda bi, c: (d, 0, 0)),
                  pl.BlockSpec((None, H, 1), lambda bi, c: (d, 0, 0)),
                  pl.BlockSpec((None, 1, H), lambda bi, c: (d, 0, 0)),
                  pl.BlockSpec((None, H, 1), lambda bi, c: (d, 0, 0)),
                  pl.BlockSpec((H, SSD_INNER), lambda bi, c: (0, 0))],
        out_specs=pl.BlockSpec((L, SSD_INNER), lambda bi, c: (row(bi, c), 0)),
        out_shape=jax.ShapeDtypeStruct((m, SSD_INNER), BF16),
        scratch_shapes=[pltpu.VMEM((SSD_N, SSD_INNER), F32)],
        compiler_params=_cparams(("arbitrary", "arbitrary")),
        name="ssd_scan_bwd" if bwd else "ssd_scan_fwd",
    )(u, u, u, dt_dir, dtt_dir, dt_bias.reshape(2, 1, H), dt_bias.reshape(2, H, 1),
      a_log.reshape(2, 1, H), a_log.reshape(2, H, 1), expand)


def _ssd_finish(yf_ref, yb_ref, xs_ref, z_ref, d_ref, g_ref):
    y = yf_ref[...].astype(F32) + yb_ref[...].astype(F32) + d_ref[...] * xs_ref[...].astype(F32)
    y = y * _silu(z_ref[...].astype(F32))
    parts = []
    for g in range(SSD_GROUPS):
        sl = y[:, g * SSD_GW:(g + 1) * SSD_GW]
        parts.append(sl * lax.rsqrt(jnp.mean(sl * sl, axis=-1, keepdims=True) + EPS))
    return jnp.concatenate(parts, axis=1) * g_ref[...]


def _gla_direction(q_ref, f_ref, v_ref, llb_ref, l1m_ref, omlb_ref, place_ref, o_ref, state, *, bwd):
    L, C = SCAN_L, HG_SUB
    nb = L // C
    dk = HG_DK
    ii, jj, visible, tri, _ = _scan_masks(L, bwd)
    band_mask = ((ii // C) == (jj // C)) & visible
    off_mask = ((ii // C) < (jj // C)) if bwd else ((ii // C) > (jj // C))
    last = 0 if bwd else L - 1
    blocks = range(1, nb) if bwd else range(nb - 1)

    def block_end(j):
        return j * C if bwd else (j + 1) * C - 1

    def head(h, carry):
        ls = slice(h * dk, (h + 1) * dk)
        q = _silu(q_ref[:, ls].astype(F32)) * (dk ** -0.5)
        f = f_ref[:, ls].astype(F32)
        v = v_ref[:, ls]
        a = llb_ref[:, ls]
        b = l1m_ref[:, ls] + _log_sigmoid(f)
        lf = (jnp.maximum(a, b) + jnp.log1p(jnp.exp(-jnp.abs(a - b)))) * LOG2E
        k = omlb_ref[:, ls] * jax.nn.sigmoid(-f)
        fd = jnp.exp2(lf)
        cum = _dot_sel_l(tri, lf)
        c_end = cum[last:last + 1, :]
        kd = k
        es = [(q * kd).astype(BF16)]
        for dl in range(1, C):
            kd = fd * pltpu.roll(kd, (L - 1) if bwd else 1, 0)
            es.append((q * kd).astype(BF16))
        band = _dot(jnp.concatenate(es, axis=1), place_ref[...])
        band = pltpu.roll(band, 0, 1, stride=1, stride_axis=0)
        ends = [cum[block_end(j):block_end(j) + 1, :] for j in range(nb)]
        ktb = jnp.concatenate([k[j * C:(j + 1) * C] * jnp.exp2(ends[j] - cum[j * C:(j + 1) * C]) for j in range(nb)],
                              axis=0).astype(BF16)
        sub16 = lax.broadcasted_iota(jnp.int32, (16, 1), 0) // C
        zero_b = jnp.zeros((), BF16)
        kcols, qcols = [], []
        pieces = [None] * nb
        for j in (blocks if bwd else reversed(blocks)):
            t0 = (j * C // 16) * 16
            tile = jnp.where(sub16 == (j * C - t0) // C, ktb[t0:t0 + 16], zero_b)
            kcols.append(jnp.concatenate(
                [x for x in (jnp.zeros((t0, dk), BF16) if t0 else None, tile,
                             jnp.zeros((L - t0 - 16, dk), BF16) if L - t0 - 16 else None) if x is not None], axis=0))
            near = j - 1 if bwd else j + 1
            prev = j - 1 if bwd else j + 1
            if 0 <= prev < nb and prev in blocks:
                step = jnp.exp2(ends[prev] - ends[j])
                for i in (range(0, near) if bwd else range(near + 1, nb)):
                    pieces[i] = pieces[i] * step
            rs = slice(near * C, (near + 1) * C)
            pieces[near] = q[rs] * jnp.exp2(cum[rs] - ends[j])
            zero_f = jnp.zeros((C, dk), F32)
            qcols.append(jnp.concatenate([zero_f if p is None else p for p in pieces], axis=0).astype(BF16))
        off = _dot_nt(jnp.concatenate(qcols, axis=1), jnp.concatenate(kcols, axis=1))
        att = jnp.where(band_mask, band, jnp.where(off_mask, off, 0.0))
        st = state[h]
        o = _dot(att.astype(BF16), v) + _dot_nt((q * jnp.exp2(cum)).astype(BF16), st.astype(BF16))
        o_ref[:, ls] = o.astype(o_ref.dtype)
        kw = (k * jnp.exp2(c_end - cum)).astype(BF16)
        state[h] = st * jnp.exp2(c_end) + _dot_tn(v, kw)
        return carry

    return head


def _gla_scan_kernel(qf_ref, ff_ref, vf_ref, qb_ref, fb_ref, vb_ref, llb_ref, l1m_ref, omlb_ref, pf_ref, pb_ref,
                     of_ref, ob_ref, state):
    @pl.when(pl.program_id(1) == 0)
    def _():
        state[...] = jnp.zeros_like(state)

    fwd = _gla_direction(qf_ref, ff_ref, vf_ref, llb_ref.at[0], l1m_ref.at[0], omlb_ref.at[0], pf_ref, of_ref,
                         state.at[0], bwd=False)
    bwd = _gla_direction(qb_ref, fb_ref, vb_ref, llb_ref.at[1], l1m_ref.at[1], omlb_ref.at[1], pb_ref, ob_ref,
                         state.at[1], bwd=True)
    for h in range(HG_HEADS):
        fwd(h, 0)
        bwd(h, 0)


def _gla_place_matrix(bwd):
    pm = np.zeros((HG_SUB * HG_DK, SCAN_L), np.float32)
    for dl in range(HG_SUB):
        pm[dl * HG_DK:(dl + 1) * HG_DK, dl if bwd else (SCAN_L - dl) % SCAN_L] = 1.0
    return jnp.asarray(pm, BF16)


def gla_scan(proj, llb, l1m, omlb, *, bsz, ctx_len):
    m = proj.shape[0]
    t_all = m // bsz
    nc = t_all // SCAN_L
    ncc = ctx_len // SCAN_L
    L = SCAN_L

    def blk(bwd, col):
        return pl.BlockSpec((L, D_MODEL), lambda bi, c: (bi * nc + _scan_chunk_index(bwd, c, ncc, nc), col))

    vec = pl.BlockSpec((2, 1, D_MODEL), lambda bi, c: (0, 0, 0))
    pf, pb = _gla_place_matrix(False), _gla_place_matrix(True)
    const = pl.BlockSpec(pf.shape, lambda bi, c: (0, 0))
    return pl.pallas_call(
        _gla_scan_kernel,
        grid=(bsz, nc),
        in_specs=[blk(False, 0), blk(False, 1), blk(False, 3), blk(True, 0), blk(True, 2), blk(True, 3),
                  vec, vec, vec, const, const],
        out_specs=[blk(False, 0), blk(True, 0)],
        out_shape=[jax.ShapeDtypeStruct((m, D_MODEL), BF16)] * 2,
        scratch_shapes=[pltpu.VMEM((2, HG_HEADS, HG_DK, HG_DK), F32)],
        compiler_params=_cparams(("arbitrary", "arbitrary")),
        name="gla_scan",
    )(proj, proj, proj, proj, proj, proj, llb, l1m, omlb, pf, pb)


def _hgrn_finish(of_ref, ob_ref, gate_ref, g_ref):
    o = of_ref[...].astype(F32) + ob_ref[...].astype(F32)
    parts = []
    for h in range(HG_HEADS):
        sl = o[:, h * HG_DK:(h + 1) * HG_DK]
        parts.append(sl * lax.rsqrt(jnp.mean(sl * sl, axis=-1, keepdims=True) + EPS))
    return jnp.concatenate(parts, axis=1) * g_ref[...] * _silu(gate_ref[...].astype(F32))


def _attn_qkv_kernel(x_ref, g_ref, mod_ref, w_ref, cos_ref, sin_ref, qg_ref, kg_ref, bd_ref, rot_ref,
                     q_ref, k_ref, v_ref, *, shift_idx, scale_idx):
    h = _norm_mod(x_ref[...], g_ref[...], mod_ref, shift_idx, scale_idx).astype(BF16)
    nq = ATT_HEADS * ATT_HD
    nk = ATT_KV * ATT_HD

    def norm_rope(a, gain, width, out_scale):
        ss = _dot((a * a).astype(BF16), bd_ref[:width, :width])
        an = a * lax.rsqrt(ss * (1.0 / ATT_HD) + EPS) * gain
        rot = _dot(an.astype(BF16), rot_ref[:width, :width])
        cos = jnp.concatenate([cos_ref[...]] * (width // 128), axis=1)
        sin = jnp.concatenate([sin_ref[...]] * (width // 128), axis=1)
        return (an * cos + rot * sin) * out_scale

    q = _dot(h, w_ref[:, :nq])
    q_ref[...] = norm_rope(q, qg_ref[...], nq, ATT_HD ** -0.5 * math.log2(math.e)).astype(q_ref.dtype)
    k = _dot(h, w_ref[:, nq:nq + nk])
    k_ref[...] = norm_rope(k, kg_ref[...], nk, 1.0).astype(k_ref.dtype)
    v_ref[...] = _dot(h, w_ref[:, nq + nk:]).astype(v_ref.dtype)


def attn_qkv(x, g, rowmod, w, cos_t, sin_t, q_g, k_g, bd, rot, *, bsz, shift_idx, scale_idx):
    m = x.shape[0]
    tm = ROW_GROUP
    tpb = (m // bsz) // tm
    nq = ATT_HEADS * ATT_HD
    nk = ATT_KV * ATT_HD
    const = lambda shape: pl.BlockSpec(shape, lambda i: (0, 0))
    return pl.pallas_call(
        functools.partial(_attn_qkv_kernel, shift_idx=shift_idx, scale_idx=scale_idx),
        grid=(m // tm,),
        in_specs=[pl.BlockSpec((tm, D_MODEL), lambda i: (i, 0)),
                  const((1, D_MODEL)),
                  pl.BlockSpec((1, 6, D_MODEL), lambda i: (i, 0, 0)),
                  const(w.shape),
                  pl.BlockSpec((tm, 128), lambda i: (i % tpb, 0)),
                  pl.BlockSpec((tm, 128), lambda i: (i % tpb, 0)),
                  const((1, nq)), const((1, nk)), const(bd.shape), const(rot.shape)],
        out_specs=[pl.BlockSpec((tm, nq), lambda i: (i, 0)),
                   pl.BlockSpec((tm, nk), lambda i: (i, 0)),
                   pl.BlockSpec((tm, nk), lambda i: (i, 0))],
        out_shape=[jax.ShapeDtypeStruct((m, nq), BF16),
                   jax.ShapeDtypeStruct((m, nk), BF16),
                   jax.ShapeDtypeStruct((m, nk), BF16)],
        compiler_params=_cparams(("parallel",)),
        name="attn_qkv",
    )(x, g.reshape(1, D_MODEL), rowmod, w, cos_t, sin_t, q_g, k_g, bd, rot)


def _attn_kernel(q_ref, k_ref, v_ref, o_ref, kaug, vaug, kmax2, *, ctx_len):
    qt = pl.program_id(2)
    tq = q_ref.shape[0]
    hd = ATT_HD
    lane = lax.broadcasted_iota(jnp.int32, (1, 2 * hd), 1)
    n_r = q_ref.shape[1] // (2 * hd)
    aug_lane = (hd, 0)
    ii = lax.broadcasted_iota(jnp.int32, (2 * hd, 2 * hd), 0)
    jj = lax.broadcasted_iota(jnp.int32, (2 * hd, 2 * hd), 1)
    half_sum = jnp.where((ii // hd) == (jj // hd), 1.0, 0.0).astype(BF16)

    @pl.when(qt == 0)
    def _():
        v = v_ref[...]
        k = k_ref[...]
        one = jnp.ones((), BF16)
        vaug[0] = jnp.where(lane < hd, v, one)
        vaug[1] = jnp.where(lane < hd, one, v)
        kaug[0] = jnp.where(lane == aug_lane[0], one, k)
        kaug[1] = jnp.where(lane == aug_lane[1], one, k)
        kf = k.astype(F32)
        kn2 = _dot((kf * kf).astype(BF16), half_sum)
        kmax2[...] = jnp.broadcast_to(jnp.max(kn2, axis=0, keepdims=True), kmax2.shape)

    def queries(j):
        sel = (lane // hd) == j
        return jnp.concatenate(
            [jnp.where(sel, q_ref[:, r * 128:(r + 1) * 128], jnp.zeros((), BF16)) for r in range(n_r)], axis=0)

    def finish(outs):
        o = jnp.where(lane < hd, outs[0], outs[1])
        for r in range(n_r):
            o_ref[:, r * 128:(r + 1) * 128] = o[r * tq:(r + 1) * tq].astype(o_ref.dtype)

    def attend_exact(n_keys):
        outs = []
        for j in range(2):
            s = _dot_nt(queries(j), k_ref[:n_keys, :])
            mx = jnp.max(s, axis=-1, keepdims=True)
            p = jnp.exp2(s - mx).astype(BF16)
            ov = _dot(p, vaug[j, :n_keys, :])
            outs.append(ov / pltpu.roll(ov, hd, 1))
        finish(outs)

    def attend_shifted(n_keys, shifts):
        outs = []
        for j in range(2):
            qa = jnp.where(lane == aug_lane[j], (-shifts[j]).astype(BF16), queries(j))
            p = jnp.exp2(_dot_nt(qa, kaug[j, :n_keys, :])).astype(BF16)
            ov = _dot(p, vaug[j, :n_keys, :])
            outs.append(ov / pltpu.roll(ov, hd, 1))
        finish(outs)

    def attend(n_keys):
        shifts = []
        ones = jnp.ones((2 * hd, 2 * hd), BF16)
        for j in range(2):
            qf = queries(j).astype(F32)
            qn2 = _dot((qf * qf).astype(BF16), ones)
            shifts.append(jnp.sqrt(qn2 * kmax2[0:1, j * hd:j * hd + 1]) * 1.02)
        worst = jnp.max(jnp.maximum(shifts[0], shifts[1]))
        small = worst < ATT_SHIFT_MAX

        @pl.when(small)
        def _():
            attend_shifted(n_keys, shifts)

        @pl.when(jnp.logical_not(small))
        def _():
            attend_exact(n_keys)

    n_ctx_tiles = ctx_len // tq

    @pl.when(qt < n_ctx_tiles)
    def _():
        attend(ctx_len)

    @pl.when(qt >= n_ctx_tiles)
    def _():
        attend(k_ref.shape[0])


def attention(q, k, v, *, bsz, ctx_len):
    m = q.shape[0]
    t_all = m // bsz
    tq = ATT_TQ
    nqt = t_all // tq
    n_pairs = ATT_KV // 2
    qw = q.shape[1] // n_pairs
    return pl.pallas_call(
        functools.partial(_attn_kernel, ctx_len=ctx_len),
        grid=(bsz, n_pairs, nqt),
        in_specs=[pl.BlockSpec((tq, qw), lambda bi, p, t: (bi * nqt + t, p)),
                  pl.BlockSpec((t_all, 2 * ATT_HD), lambda bi, p, t: (bi, p)),
                  pl.BlockSpec((t_all, 2 * ATT_HD), lambda bi, p, t: (bi, p))],
        out_specs=pl.BlockSpec((tq, qw), lambda bi, p, t: (bi * nqt + t, p)),
        out_shape=jax.ShapeDtypeStruct(q.shape, BF16),
        scratch_shapes=[pltpu.VMEM((2, t_all, 2 * ATT_HD), BF16),
                        pltpu.VMEM((2, t_all, 2 * ATT_HD), BF16),
                        pltpu.VMEM((8, 2 * ATT_HD), F32)],
        compiler_params=_cparams(("arbitrary", "arbitrary", "arbitrary")),
        name="attention",
    )(q, k, v)


def _identity_pro(a_ref):
    return a_ref[...]


def _mlstm_qkv_kernel(xc_ref, xm_ref, wq_ref, wk_ref, wv_ref, wg_ref, bg_ref, q_ref, k_ref, v_ref, gate_ref):
    gates = jnp.zeros(gate_ref.shape, F32) + bg_ref[...]
    for which, (src, w_ref, o_ref) in enumerate(((xc_ref, wq_ref, q_ref), (xc_ref, wk_ref, k_ref),
                                                 (xm_ref, wv_ref, v_ref))):
        for h in range(ML_HEADS):
            hs = slice(h * ML_DH, (h + 1) * ML_DH)
            r = _dot(src[:, hs], w_ref[h]).astype(BF16)
            o_ref[:, hs] = r
            gates = gates + _dot(r, wg_ref[which * ML_INNER + h * ML_DH:which * ML_INNER + (h + 1) * ML_DH, :])
    gate_ref[...] = gates


def mlstm_qkv(xc, up, wq, wk, wv, wg, bg, *, tm=PROJ_TM):
    m = xc.shape[0]
    ng = wg.shape[1]
    const = lambda shape: pl.BlockSpec(shape, lambda i: (0,) * len(shape))
    row = pl.BlockSpec((tm, ML_INNER), lambda i: (i, 0))
    return pl.pallas_call(
        _mlstm_qkv_kernel,
        grid=(m // tm,),
        in_specs=[row, row, const(wq.shape), const(wk.shape), const(wv.shape), const(wg.shape), const((1, ng))],
        out_specs=[row, row, row, pl.BlockSpec((tm, ng), lambda i: (i, 0))],
        out_shape=[jax.ShapeDtypeStruct((m, ML_INNER), BF16)] * 3 + [jax.ShapeDtypeStruct((m, ng), F32)],
        compiler_params=_cparams(("parallel",)),
        name="mlstm_qkv",
    )(xc, up, wq, wk, wv, wg, bg.reshape(1, ng))


def _mlstm_scan_kernel(q_ref, k_ref, v_ref, gt_ref, gtt_ref, h_ref, c_st, n_st, m_st, *, bwd):
    ci = pl.program_id(1)
    L = ML_L
    nh = ML_HEADS
    scale = ML_DH ** -0.5

    @pl.when(ci == 0)
    def _():
        c_st[...] = jnp.zeros_like(c_st)
        n_st[...] = jnp.zeros_like(n_st)
        m_st[...] = jnp.full(m_st.shape, NEG_BIG, F32)

    _, _, visible, tri, trit = _scan_masks(L, bwd)
    gt = gt_ref[...]
    gtt = gtt_ref[...]
    li_c, li_r = gt[:, :nh], gtt[:nh, :]
    lf_c, lf_r = _log_sigmoid(gt[:, nh:]), _log_sigmoid(gtt[nh:, :])
    cum_c = _dot_sel_l(tri, lf_c)
    cum_r = _dot_sel_r(lf_r, trit)
    end_c = jnp.sum(lf_c, axis=0, keepdims=True)
    for h in range(nh):
        hs = slice(h * ML_DH, (h + 1) * ML_DH)
        q = q_ref[:, hs]
        kb = k_ref[:, hs]
        k = kb.astype(F32)
        v = v_ref[:, hs]
        m_prev = m_st[h:h + 1, 0:1]
        cum_end = end_c[:, h:h + 1]
        dmat = jnp.where(visible, cum_c[:, h:h + 1] - cum_r[h:h + 1, :] + li_r[h:h + 1, :], -jnp.inf)
        inter = cum_c[:, h:h + 1] + m_prev
        m_t = jnp.maximum(inter, jnp.max(dmat, axis=1, keepdims=True))
        w = jnp.exp(dmat - m_t)
        w_c = jnp.exp(inter - m_t)
        qk = _dot_nt(q, kb) * scale * w
        cmat = c_st[h]
        num = _dot(qk.astype(BF16), v) + w_c * _dot(q, cmat.astype(BF16))
        qn = jnp.sum(q.astype(F32) * n_st[h:h + 1, :], axis=1, keepdims=True)
        den = jnp.sum(qk, axis=1, keepdims=True) + w_c * qn
        hv = num / jnp.maximum(jnp.abs(den), jnp.exp(-m_t))
        h_ref[:, hs] = hv.astype(h_ref.dtype)
        wend_c = cum_end - cum_c[:, h:h + 1] + li_c[:, h:h + 1]
        wend_r = cum_end - cum_r[h:h + 1, :] + li_r[h:h + 1, :]
        m_new = jnp.maximum(cum_end + m_prev, jnp.max(wend_r, axis=1, keepdims=True))
        a_old = jnp.exp(cum_end + m_prev - m_new)
        e_c = jnp.exp(wend_c - m_new) * scale
        e_r = jnp.exp(wend_r - m_new) * scale
        c_st[h] = a_old * cmat + _dot_tn((k * e_c).astype(BF16), v)
        e_r8 = jnp.broadcast_to(e_r, (8, L)).astype(BF16)
        n_st[h:h + 1, :] = a_old * n_st[h:h + 1, :] + _dot(e_r8, kb)[0:1, :]
        m_st[h:h + 1, :] = jnp.broadcast_to(m_new, (1, m_st.shape[1]))


def mlstm_scan(q, k, v, gt_dir, gtt_dir, *, bsz, ctx_len, bwd):
    m = q.shape[0]
    t_all = m // bsz
    nc = t_all // ML_L
    ncc = ctx_len // ML_L
    L, nh = ML_L, ML_HEADS
    d = int(bwd)

    def row(bi, c):
        return bi * nc + _scan_chunk_index(bwd, c, ncc, nc)

    blk = pl.BlockSpec((L, ML_INNER), lambda bi, c: (row(bi, c), 0))
    return pl.pallas_call(
        functools.partial(_mlstm_scan_kernel, bwd=bwd),
        grid=(bsz, nc),
        in_specs=[blk, blk, blk,
                  pl.BlockSpec((None, L, 2 * nh), lambda bi, c: (d, row(bi, c), 0)),
                  pl.BlockSpec((None, 2 * nh, L), lambda bi, c: (d, 0, row(bi, c)))],
        out_specs=pl.BlockSpec((L, ML_INNER), lambda bi, c: (row(bi, c), 0)),
        out_shape=jax.ShapeDtypeStruct((m, ML_INNER), BF16),
        scratch_shapes=[pltpu.VMEM((nh, ML_DH, ML_DH), F32),
                        pltpu.VMEM((8, ML_DH), F32),
                        pltpu.VMEM((8, 128), F32)],
        compiler_params=_cparams(("arbitrary", "arbitrary")),
        name="mlstm_scan_bwd" if bwd else "mlstm_scan_fwd",
    )(q, k, v, gt_dir, gtt_dir)


def _mlstm_finish(hf_ref, hb_ref, xc_ref, z_ref, skip_ref, g_ref):
    hsum = hf_ref[...].astype(F32) + hb_ref[...].astype(F32)
    parts = []
    for h in range(ML_HEADS):
        sl = hsum[:, h * ML_DH:(h + 1) * ML_DH]
        parts.append(sl * lax.rsqrt(jnp.mean(sl * sl, axis=-1, keepdims=True) + EPS))
    hn = jnp.concatenate(parts, axis=1) * g_ref[...]
    return (hn + skip_ref[...] * xc_ref[...].astype(F32)) * _silu(z_ref[...].astype(F32))


def _ffn_kernel(x_ref, g_ref, mod_ref, w1_ref, w3_ref, w2_ref, o_ref, h_sc, acc_sc, *, shift_idx, scale_idx, gate_idx):
    f = pl.program_id(1)

    @pl.when(f == 0)
    def _():
        h_sc[...] = _norm_mod(x_ref[...], g_ref[...], mod_ref, shift_idx, scale_idx).astype(BF16)
        acc_sc[...] = jnp.zeros_like(acc_sc)

    h = h_sc[...]
    a = _silu(_dot(h, w1_ref[...])) * _dot(h, w3_ref[...])
    acc_sc[...] += _dot(a.astype(BF16), w2_ref[...])

    @pl.when(f == pl.num_programs(1) - 1)
    def _():
        o_ref[...] = _rows_gate_residual(x_ref[...], acc_sc[...], mod_ref, gate_idx)


def dense_ffn(x, g, rowmod, w1, w3, w2, layer, *, tm=1024, tf=512):
    m = x.shape[0]
    tm = _pick_tile(m, tm)
    gm = tm // ROW_GROUP
    nf = D_FF // tf
    return pl.pallas_call(
        functools.partial(_ffn_kernel, shift_idx=3, scale_idx=4, gate_idx=5),
        grid=(m // tm, nf),
        in_specs=[pl.BlockSpec((tm, D_MODEL), lambda i, f: (i, 0)),
                  pl.BlockSpec((1, D_MODEL), lambda i, f: (0, 0)),
                  pl.BlockSpec((gm, 6, D_MODEL), lambda i, f: (i, 0, 0)),
                  pl.BlockSpec((None, D_MODEL, tf), lambda i, f: (layer, 0, f)),
                  pl.BlockSpec((None, D_MODEL, tf), lambda i, f: (layer, 0, f)),
                  pl.BlockSpec((None, tf, D_MODEL), lambda i, f: (layer, f, 0))],
        out_specs=pl.BlockSpec((tm, D_MODEL), lambda i, f: (i, 0)),
        out_shape=jax.ShapeDtypeStruct((m, D_MODEL), F32),
        scratch_shapes=[pltpu.VMEM((tm, D_MODEL), BF16), pltpu.VMEM((tm, D_MODEL), F32)],
        compiler_params=_cparams(("parallel", "arbitrary")),
        name="dense_ffn",
    )(x, g.reshape(1, D_MODEL), rowmod, w1, w3, w2)


def _router_kernel(x_ref, g_ref, mod_ref, wr_ref, h_ref, route_ref, cnt_ref, cnt_sc, *, shift_idx, scale_idx, n_real):
    i = pl.program_id(0)

    @pl.when(i == 0)
    def _():
        cnt_sc[...] = jnp.zeros_like(cnt_sc)

    @pl.when(i < n_real)
    def _():
        h = _norm_mod(x_ref[...], g_ref[...], mod_ref, shift_idx, scale_idx)
        h_ref[...] = h.astype(h_ref.dtype)
        logits = _dot_hi(h, wr_ref[...])
        tm = logits.shape[0]
        lane = lax.broadcasted_iota(jnp.int32, logits.shape, 1)
        logits = jnp.where(lane < N_EXPERTS, logits, -jnp.inf)
        m1 = jnp.max(logits, axis=-1, keepdims=True)
        i1 = jnp.min(jnp.where(logits == m1, lane, 128), axis=-1, keepdims=True)
        rest = jnp.where(lane == i1, -jnp.inf, logits)
        m2 = jnp.max(rest, axis=-1, keepdims=True)
        i2 = jnp.min(jnp.where(rest == m2, lane, 128), axis=-1, keepdims=True)
        e2 = jnp.exp(m2 - m1)
        w1 = 1.0 / (1.0 + e2)
        w2 = e2 / (1.0 + e2)
        chosen = jnp.where(lane == i1, 1.0, jnp.where(lane == i2, 1.0, 0.0))
        ii = lax.broadcasted_iota(jnp.int32, (tm, tm), 0)
        jj = lax.broadcasted_iota(jnp.int32, (tm, tm), 1)
        strict = jnp.where(ii > jj, 1.0, 0.0).astype(BF16)
        prefix = _dot(strict, chosen.astype(BF16)) + cnt_sc[0:1, :]
        r1 = jnp.sum(jnp.where(lane == i1, prefix, 0.0), axis=-1, keepdims=True)
        r2 = jnp.sum(jnp.where(lane == i2, prefix, 0.0), axis=-1, keepdims=True)
        cnt_sc[...] = cnt_sc[...] + jnp.sum(chosen, axis=0, keepdims=True)
        route = jnp.zeros(logits.shape, F32)
        for col, val in enumerate((i1.astype(F32), i2.astype(F32), w1, w2, r1, r2)):
            route = jnp.where(lane == col, val, route)
        route_ref[...] = route

    @pl.when(i >= n_real)
    def _():
        h_ref[...] = jnp.zeros_like(h_ref)
        route_ref[...] = jnp.zeros_like(route_ref)

    cnt_ref[...] = cnt_sc[...]


def moe_router(x, g, rowmod, wr_pad, p_rows, *, tm=PROJ_TM):
    m = x.shape[0]
    n_real = m // tm
    gm = tm // ROW_GROUP
    clamp = lambda i: jnp.minimum(i, n_real - 1)
    return pl.pallas_call(
        functools.partial(_router_kernel, shift_idx=3, scale_idx=4, n_real=n_real),
        grid=(p_rows // tm,),
        in_specs=[pl.BlockSpec((tm, D_MODEL), lambda i: (clamp(i), 0)),
                  pl.BlockSpec((1, D_MODEL), lambda i: (0, 0)),
                  pl.BlockSpec((gm, 6, D_MODEL), lambda i: (clamp(i), 0, 0)),
                  pl.BlockSpec(wr_pad.shape, lambda i: (0, 0))],
        out_specs=[pl.BlockSpec((tm, D_MODEL), lambda i: (i, 0)), pl.BlockSpec((tm, 128), lambda i: (i, 0)),
                   pl.BlockSpec((8, 128), lambda i: (0, 0))],
        out_shape=[jax.ShapeDtypeStruct((p_rows, D_MODEL), BF16), jax.ShapeDtypeStruct((p_rows, 128), F32),
                   jax.ShapeDtypeStruct((8, 128), F32)],
        scratch_shapes=[pltpu.VMEM((8, 128), F32)],
        compiler_params=_cparams(("arbitrary",)),
        name="moe_router",
    )(x, g.reshape(1, D_MODEL), rowmod, wr_pad)


def _moe_ffn_kernel(te_ref, nt_ref, h_ref, w1_ref, w3_ref, w2_ref, o_ref, acc_sc):
    i = pl.program_id(0)
    f = pl.program_id(1)
    live = i < nt_ref[0]

    @pl.when(f == 0)
    def _():
        acc_sc[...] = jnp.zeros_like(acc_sc)

    @pl.when(live)
    def _():
        h = h_ref[...]
        a = _silu(_dot(h, w1_ref[...].astype(BF16))) * _dot(h, w3_ref[...].astype(BF16))
        acc_sc[...] += _dot(a.astype(BF16), w2_ref[...].astype(BF16))

    @pl.when(f == pl.num_programs(1) - 1)
    def _():
        o_ref[...] = acc_sc[...].astype(o_ref.dtype)


def moe_grouped_ffn(h_sorted, tile_expert, n_tiles, w1, w3, w2, layer, *, tm=MOE_TM, tf=512):
    p = h_sorted.shape[0]
    nf = D_FF // tf
    grid_spec = pltpu.PrefetchScalarGridSpec(
        num_scalar_prefetch=2,
        grid=(p // tm, nf),
        in_specs=[pl.BlockSpec((tm, D_MODEL), lambda i, f, te, nt: (i, 0)),
                  pl.BlockSpec((None, None, D_MODEL, tf), lambda i, f, te, nt: (layer, te[i], 0, f)),
                  pl.BlockSpec((None, None, D_MODEL, tf), lambda i, f, te, nt: (layer, te[i], 0, f)),
                  pl.BlockSpec((None, None, tf, D_MODEL), lambda i, f, te, nt: (layer, te[i], f, 0))],
        out_specs=pl.BlockSpec((tm, D_MODEL), lambda i, f, te, nt: (i, 0)),
        scratch_shapes=[pltpu.VMEM((tm, D_MODEL), F32)])
    return pl.pallas_call(
        _moe_ffn_kernel,
        grid_spec=grid_spec,
        out_shape=jax.ShapeDtypeStruct((p, D_MODEL), BF16),
        compiler_params=_cparams(("arbitrary", "arbitrary")),
        name="moe_grouped_ffn",
    )(tile_expert, n_tiles, h_sorted, w1, w3, w2)


def _moe_combine_kernel(x_ref, ya_ref, yb_ref, route_ref, mod_ref, o_ref, *, gate_idx):
    r = route_ref[...]
    y = r[:, 2:3] * ya_ref[...].astype(F32) + r[:, 3:4] * yb_ref[...].astype(F32)
    o_ref[...] = _rows_gate_residual(x_ref[...], y, mod_ref, gate_idx)


def moe_combine(x, ya, yb, route, rowmod, *, latent_only=None):
    m = x.shape[0]
    tm = ROW_GROUP
    if latent_only is None:
        n_out, src = m // tm, (lambda i: i)
    else:
        bsz, ctx_len = latent_only
        ctx_tiles = ctx_len // tm
        lat_tiles = m // bsz // tm - ctx_tiles
        n_out, src = bsz * lat_tiles, (lambda i: i + (i // lat_tiles + 1) * ctx_tiles)
    row = pl.BlockSpec((tm, D_MODEL), lambda i: (src(i), 0))
    return pl.pallas_call(
        functools.partial(_moe_combine_kernel, gate_idx=5),
        grid=(n_out,),
        in_specs=[row, row, row, pl.BlockSpec((tm, 128), lambda i: (src(i), 0)),
                  pl.BlockSpec((1, 6, D_MODEL), lambda i: (src(i), 0, 0))],
        out_specs=pl.BlockSpec((tm, D_MODEL), lambda i: (i, 0)),
        out_shape=jax.ShapeDtypeStruct((n_out * tm, D_MODEL), F32),
        compiler_params=_cparams(("parallel",)),
        name="moe_combine",
    )(x, ya, yb, route, rowmod)


def moe_ffn(x, g, rowmod, w_router, w1, w3, w2, layer, latent_only=None):
    m = x.shape[0]
    tm = MOE_TM
    n_tiles_max = (2 * m) // tm + N_EXPERTS
    p = n_tiles_max * tm
    wr_pad = jnp.zeros((D_MODEL, 128), F32).at[:, :N_EXPERTS].set(w_router)
    h, route, cnt = moe_router(x, g, rowmod, wr_pad, p)
    experts = jnp.arange(N_EXPERTS, dtype=jnp.int32)
    counts = cnt[0, :N_EXPERTS].astype(jnp.int32)
    tiles_per = (counts + tm - 1) // tm
    tile_end = jnp.cumsum(tiles_per)
    grp_start = (tile_end - tiles_per) * tm
    cnt_start = jnp.cumsum(counts) - counts
    e12 = route[:m, 0:2].astype(jnp.int32)
    r12 = route[:m, 4:6].astype(jnp.int32)
    pos12 = jnp.sum(jnp.where(e12[:, :, None] == experts, grp_start, 0), axis=-1) + r12
    order = jnp.argsort(e12.reshape(-1), stable=True).astype(jnp.int32)
    tile_expert = jnp.minimum(
        jnp.sum(jnp.arange(n_tiles_max, dtype=jnp.int32)[:, None] >= tile_end[None, :], axis=1), N_EXPERTS - 1
    ).astype(jnp.int32)
    n_tiles = tile_end[-1:].astype(jnp.int32)
    rank = jnp.arange(p, dtype=jnp.int32) - jnp.repeat(grp_start[tile_expert], tm)
    in_use = rank < jnp.repeat(counts[tile_expert], tm)
    sorted_idx = jnp.clip(jnp.repeat(cnt_start[tile_expert], tm) + rank, 0, 2 * m - 1)
    src_token = jnp.where(in_use, jnp.take(order, sorted_idx, mode="clip") // 2, jnp.arange(p, dtype=jnp.int32) % m)
    h_sorted = jnp.take(h, src_token, axis=0, mode="clip")
    y_sorted = moe_grouped_ffn(h_sorted, tile_expert, n_tiles, w1, w3, w2, layer)
    ya = jnp.take(y_sorted, pos12[:, 0], axis=0, mode="clip")
    yb = jnp.take(y_sorted, pos12[:, 1], axis=0, mode="clip")
    return moe_combine(x, ya, yb, route, rowmod, latent_only=latent_only)


def _dir_split(a, n):
    m = a.shape[0]
    a3 = a.reshape(m, 2, n)
    return jnp.transpose(a3, (1, 0, 2)), jnp.transpose(a3, (1, 2, 0))


def ssd_layer(x, g, rowmod, w_in, conv_w, conv_b, dt_bias, a_log, d_skip, norm_g, w_out, *, bsz, ctx_len):
    m = x.shape[0]
    n_main = SSD_INNER + SSD_INNER + 2 * SSD_GROUPS * SSD_N
    zx, dt_raw = in_projection(x, g, rowmod, [w_in[:, :n_main].astype(BF16), w_in[:, n_main:].astype(BF16)],
                               [BF16, F32], shift_idx=0, scale_idx=1)
    u = dwconv_silu(zx, SSD_INNER, n_main - SSD_INNER, conv_w, conv_b, bsz=bsz, ctx_len=ctx_len)
    dt_dir, dtt_dir = _dir_split(dt_raw, SSD_HEADS)
    yf = ssd_scan(u, dt_dir, dtt_dir, dt_bias, a_log, bsz=bsz, ctx_len=ctx_len, bwd=False)
    yb = ssd_scan(u, dt_dir, dtt_dir, dt_bias, a_log, bsz=bsz, ctx_len=ctx_len, bwd=True)
    tm = PROJ_TM
    d_x = jnp.repeat(d_skip.astype(F32), SSD_P).reshape(1, SSD_INNER)
    specs = [pl.BlockSpec((tm, SSD_INNER), lambda i: (i, 0)),
             pl.BlockSpec((tm, SSD_INNER), lambda i: (i, 0)),
             pl.BlockSpec((tm, SSD_INNER), lambda i: (i, 0)),
             pl.BlockSpec((tm, SSD_INNER), lambda i: (i, 0)),
             pl.BlockSpec((1, SSD_INNER), lambda i: (0, 0)),
             pl.BlockSpec((1, SSD_INNER), lambda i: (0, 0))]
    return out_projection(_ssd_finish, [yf, yb, u, zx, d_x, norm_g.reshape(1, SSD_INNER)], specs,
                          w_out.astype(BF16), x, rowmod, gate_idx=2, tm=tm)


def hgrn_layer(x, g, rowmod, w_in, lb, norm_g, w_out, *, bsz, ctx_len):
    proj, = in_projection(x, g, rowmod, [w_in.astype(BF16)], [BF16], shift_idx=0, scale_idx=1)
    lb = lb.astype(F32).reshape(2, 1, D_MODEL)
    lbs = (jnp.log(lb), jnp.log1p(-lb), 1.0 - lb)
    of, ob = gla_scan(proj, *lbs, bsz=bsz, ctx_len=ctx_len)
    tm = PROJ_TM
    specs = [pl.BlockSpec((tm, D_MODEL), lambda i: (i, 0)),
             pl.BlockSpec((tm, D_MODEL), lambda i: (i, 0)),
             pl.BlockSpec((tm, D_MODEL), lambda i: (i, 4)),
             pl.BlockSpec((1, D_MODEL), lambda i: (0, 0))]
    return out_projection(_hgrn_finish, [of, ob, proj, norm_g.reshape(1, D_MODEL)], specs,
                          w_out.astype(BF16), x, rowmod, gate_idx=2, tm=tm)


def _attn_head_perm():
    r_per = ATT_HEADS // ATT_KV
    heads = [(2 * p + j) * r_per + r for p in range(ATT_KV // 2) for r in range(r_per) for j in range(2)]
    return np.concatenate([np.arange(h * ATT_HD, (h + 1) * ATT_HD) for h in heads])


def _rope_tables(seq_len, ctx_len, grid_w):
    rows = seq_len // grid_w
    row = jnp.repeat(jnp.arange(rows, dtype=F32), grid_w)
    col = jnp.tile(jnp.arange(grid_w, dtype=F32), rows)
    inv = ROPE_THETA ** (-jnp.arange(ROPE_FREQS, dtype=F32) / ROPE_FREQS)
    ang_r = row[:, None] * inv
    ang_c = col[:, None] * inv
    ang = jnp.concatenate([ang_r, ang_r, ang_c, ang_c], axis=-1)
    cos = jnp.concatenate([jnp.ones((ctx_len, ATT_HD), F32), jnp.cos(ang)], axis=0)
    sin = jnp.concatenate([jnp.zeros((ctx_len, ATT_HD), F32), jnp.sin(ang)], axis=0)
    return jnp.tile(cos, (1, 2)), jnp.tile(sin, (1, 2))


def _rope_matrices():
    r64 = np.zeros((ATT_HD, ATT_HD), np.float32)
    fq = ROPE_FREQS
    for ax in range(2):
        o = ax * 2 * fq
        for i in range(fq):
            r64[o + fq + i, o + i] = -1.0
            r64[o + i, o + fq + i] = 1.0
    n = ATT_HEADS
    bd = np.kron(np.eye(n, dtype=np.float32), np.ones((ATT_HD, ATT_HD), np.float32))
    rot = np.kron(np.eye(n, dtype=np.float32), r64)
    return jnp.asarray(bd, BF16), jnp.asarray(rot, BF16)


def attn_layer(x, g, rowmod, w_qkv, q_g, k_g, w_o, *, bsz, ctx_len, grid_w):
    m = x.shape[0]
    seq_len = m // bsz - ctx_len
    perm = _attn_head_perm()
    nq = ATT_HEADS * ATT_HD
    w = jnp.concatenate([w_qkv[:, :nq][:, perm], w_qkv[:, nq:]], axis=1).astype(BF16)
    cos_t, sin_t = _rope_tables(seq_len, ctx_len, grid_w)
    bd, rot = _rope_matrices()
    qg = jnp.tile(q_g.astype(F32), ATT_HEADS).reshape(1, nq)
    kg = jnp.tile(k_g.astype(F32), ATT_KV).reshape(1, ATT_KV * ATT_HD)
    q, k, v = attn_qkv(x, g, rowmod, w, cos_t, sin_t, qg, kg, bd, rot, bsz=bsz, shift_idx=0, scale_idx=1)
    o = attention(q, k, v, bsz=bsz, ctx_len=ctx_len)
    tm = PROJ_TM
    specs = [pl.BlockSpec((tm, nq), lambda i: (i, 0))]
    return out_projection(_identity_pro, [o], specs, w_o[perm, :].astype(BF16), x, rowmod, gate_idx=2, tm=tm)


def mlstm_layer(x, g, rowmod, w_up, conv_w, conv_b, w_q, w_k, w_v, w_gate, b_gate, skip, norm_g, w_down,
                *, bsz, ctx_len):
    up, = in_projection(x, g, rowmod, [w_up.astype(BF16)], [BF16], shift_idx=0, scale_idx=1)
    xc = dwconv_silu(up, 0, ML_INNER, conv_w, conv_b, bsz=bsz, ctx_len=ctx_len)
    q, k, v, gates = mlstm_qkv(xc, up, w_q.astype(BF16), w_k.astype(BF16), w_v.astype(BF16),
                               w_gate.astype(BF16), b_gate)
    gt_dir, gtt_dir = _dir_split(gates, 2 * ML_HEADS)
    hf = mlstm_scan(q, k, v, gt_dir, gtt_dir, bsz=bsz, ctx_len=ctx_len, bwd=False)
    hb = mlstm_scan(q, k, v, gt_dir, gtt_dir, bsz=bsz, ctx_len=ctx_len, bwd=True)
    tm = PROJ_TM
    specs = [pl.BlockSpec((tm, ML_INNER), lambda i: (i, 0)),
             pl.BlockSpec((tm, ML_INNER), lambda i: (i, 0)),
             pl.BlockSpec((tm, ML_INNER), lambda i: (i, 0)),
             pl.BlockSpec((tm, ML_INNER), lambda i: (i, 1)),
             pl.BlockSpec((1, ML_INNER), lambda i: (0, 0)),
             pl.BlockSpec((1, ML_INNER), lambda i: (0, 0))]
    return out_projection(_mlstm_finish, [hf, hb, xc, up, skip.reshape(1, ML_INNER), norm_g.reshape(1, ML_INNER)],
                          specs, w_down.astype(BF16), x, rowmod, gate_idx=2, tm=tm)


def kernel(x, c, ctx, c_ctx, ada_w, ada_b, norm_g, ssd_w_in, ssd_conv_w, ssd_conv_b, ssd_dt_bias, ssd_a_log, ssd_d, ssd_norm_g, ssd_w_out, hgrn_w_in, hgrn_lb, hgrn_norm_g, hgrn_w_out, attn_w_qkv, attn_q_g, attn_k_g, attn_w_o, mlstm_w_up, mlstm_conv_w, mlstm_conv_b, mlstm_w_q, mlstm_w_k, mlstm_w_v, mlstm_w_gate, mlstm_b_gate, mlstm_skip, mlstm_norm_g, mlstm_w_down, ffn_w1, ffn_w3, ffn_w2, moe_router, moe_w1, moe_w3, moe_w2):
    bsz, seq_len, _ = x.shape
    ctx_len = ctx.shape[1]
    depth = ada_w.shape[0]
    grid_w = 64
    t_all = ctx_len + seq_len
    m = bsz * t_all
    xa = jnp.concatenate([ctx, x], axis=1).reshape(m, D_MODEL)
    c_pad = jnp.zeros((8, D_MODEL), F32).at[:bsz].set(c).at[bsz].set(c_ctx)
    groups_per_batch = t_all // ROW_GROUP
    ctx_groups = ctx_len // ROW_GROUP
    gidx = np.array([bsz if (gi % groups_per_batch) < ctx_groups else gi // groups_per_batch
                     for gi in range(m // ROW_GROUP)], np.int32)
    lb_all = jnp.cumsum(jax.nn.softmax(hgrn_lb.astype(F32), axis=1), axis=1)
    lb_all = lb_all - lb_all[:, :1]
    kw = dict(bsz=bsz, ctx_len=ctx_len)
    ffn_w = [w.astype(BF16) for w in (ffn_w1, ffn_w3, ffn_w2)]
    moe_w = (moe_w1, moe_w3, moe_w2)
    for i in range(depth):
        mod = ada_modulation(c_pad, ada_w, ada_b, i).reshape(8, 6, D_MODEL)
        rowmod = mod[gidx]
        kind, j = i % 4, i // 4
        if kind == 0:
            xa = ssd_layer(xa, norm_g[i, 0], rowmod, ssd_w_in[j], ssd_conv_w[j], ssd_conv_b[j], ssd_dt_bias[j],
                           ssd_a_log[j], ssd_d[j], ssd_norm_g[j], ssd_w_out[j], **kw)
        elif kind == 1:
            xa = hgrn_layer(xa, norm_g[i, 0], rowmod, hgrn_w_in[j], lb_all[:, i], hgrn_norm_g[j], hgrn_w_out[j], **kw)
        elif kind == 2:
            xa = attn_layer(xa, norm_g[i, 0], rowmod, attn_w_qkv[j], attn_q_g[j], attn_k_g[j], attn_w_o[j],
                            grid_w=grid_w, **kw)
        else:
            xa = mlstm_layer(xa, norm_g[i, 0], rowmod, mlstm_w_up[j], mlstm_conv_w[j], mlstm_conv_b[j], mlstm_w_q[j],
                             mlstm_w_k[j], mlstm_w_v[j], mlstm_w_gate[j], mlstm_b_gate[j], mlstm_skip[j],
                             mlstm_norm_g[j], mlstm_w_down[j], **kw)
        if i % 2 == 0:
            xa = dense_ffn(xa, norm_g[i, 1], rowmod, *ffn_w, i // 2)
        else:
            last = i == depth - 1
            xa = moe_ffn(xa, norm_g[i, 1], rowmod, moe_router[i // 2], *moe_w, i // 2,
                         latent_only=(bsz, ctx_len) if last else None)
            if last:
                return xa.reshape(bsz, seq_len, D_MODEL)
    return xa.reshape(bsz, t_all, D_MODEL)[:, ctx_len:]
```

```python
import functools
import math

import jax
import jax.numpy as jnp
import numpy as np
from jax import lax
from jax.experimental import pallas as pl
from jax.experimental.pallas import tpu as pltpu

F32 = jnp.float32
BF16 = jnp.bfloat16
HI = lax.Precision.HIGHEST

D_MODEL = 1024
EPS = 1e-6
ROW_GROUP = 256
PROJ_TM = 512
CONV_W = 5
NEG_BIG = -1e30
LOG2E = math.log2(math.e)
VMEM_LIMIT = 56 << 20

SSD_INNER = 2 * D_MODEL
SSD_P = 64
SSD_HEADS = SSD_INNER // SSD_P
SSD_N = 128
SSD_GROUPS = 8
SSD_GW = SSD_INNER // SSD_GROUPS
SCAN_L = 128

HG_HEADS = 8
HG_DK = 128
HG_SUB = 8

ATT_HEADS = 16
ATT_KV = 4
ATT_HD = 64
ROPE_THETA = 10000.0
ROPE_FREQS = ATT_HD // 4
ATT_TQ = 128
ATT_ONES = 16
ATT_SHIFT_MAX = 60.0

ML_INNER = 2 * D_MODEL
ML_HEADS = 4
ML_DH = ML_INNER // ML_HEADS
ML_L = 256

D_FF = 7 * D_MODEL // 2
N_EXPERTS = 8
MOE_TM = 1024


def _cparams(sem):
    return pltpu.CompilerParams(dimension_semantics=sem, vmem_limit_bytes=VMEM_LIMIT)


def _dot(a, b):
    return jnp.dot(a, b, preferred_element_type=F32)


def _dot_hi(a, b):
    return jnp.dot(a, b, preferred_element_type=F32, precision=HI)


def _dot_nt(a, b):
    return lax.dot_general(a, b, (((1,), (1,)), ((), ())), preferred_element_type=F32)


def _dot_tn(a, b):
    return lax.dot_general(a, b, (((0,), (0,)), ((), ())), preferred_element_type=F32)


def _silu(x):
    return x * jax.nn.sigmoid(x)


def _log_sigmoid(x):
    return jnp.minimum(x, 0.0) - jnp.log1p(jnp.exp(-jnp.abs(x)))


def _softplus(x):
    return jnp.maximum(x, 0.0) + jnp.log1p(jnp.exp(-jnp.abs(x)))


def _pick_tile(m, pref):
    t = pref
    while m % t:
        t //= 2
    return t


def _rows_scale_shift(y, mod_ref, scale_idx, shift_idx):
    parts = []
    for gi in range(y.shape[0] // ROW_GROUP):
        sl = y[gi * ROW_GROUP:(gi + 1) * ROW_GROUP]
        parts.append(sl * (1.0 + mod_ref[gi, scale_idx:scale_idx + 1, :]) + mod_ref[gi, shift_idx:shift_idx + 1, :])
    return parts[0] if len(parts) == 1 else jnp.concatenate(parts, axis=0)


def _rows_gate_residual(x, acc, mod_ref, gate_idx):
    parts = []
    for gi in range(x.shape[0] // ROW_GROUP):
        sl = slice(gi * ROW_GROUP, (gi + 1) * ROW_GROUP)
        parts.append(x[sl] + mod_ref[gi, gate_idx:gate_idx + 1, :] * acc[sl])
    return parts[0] if len(parts) == 1 else jnp.concatenate(parts, axis=0)


def _norm_mod(x, g, mod_ref, shift_idx, scale_idx):
    y = x * lax.rsqrt(jnp.mean(x * x, axis=-1, keepdims=True) + EPS) * g
    return _rows_scale_shift(y, mod_ref, scale_idx, shift_idx)


def _split2(a):
    hi = a.astype(BF16)
    return hi, (a - hi.astype(F32)).astype(BF16)


def _dot_sel_l(sel, a):
    hi, lo = _split2(a)
    return _dot(sel, hi) + _dot(sel, lo)


def _dot_sel_r(a, sel):
    hi, lo = _split2(a)
    return _dot(hi, sel) + _dot(lo, sel)


def _scan_masks(n, bwd):
    ii = lax.broadcasted_iota(jnp.int32, (n, n), 0)
    jj = lax.broadcasted_iota(jnp.int32, (n, n), 1)
    visible = (ii <= jj) if bwd else (ii >= jj)
    visible_t = (ii >= jj) if bwd else (ii <= jj)
    tri = jnp.where(visible, 1.0, 0.0).astype(BF16)
    trit = jnp.where(visible_t, 1.0, 0.0).astype(BF16)
    return ii, jj, visible, tri, trit


def _ada_kernel(c_ref, w_ref, b_ref, o_ref):
    c = c_ref[...]
    o_ref[...] = _dot_hi(_silu(c), w_ref[...]) + b_ref[...]


def ada_modulation(c_pad, w, b, layer):
    n = w.shape[2]
    tn = 1024
    return pl.pallas_call(
        _ada_kernel,
        grid=(n // tn,),
        in_specs=[pl.BlockSpec(c_pad.shape, lambda j: (0, 0)),
                  pl.BlockSpec((None, D_MODEL, tn), lambda j: (layer, 0, j)),
                  pl.BlockSpec((None, 1, tn), lambda j: (layer, 0, j))],
        out_specs=pl.BlockSpec((c_pad.shape[0], tn), lambda j: (0, j)),
        out_shape=jax.ShapeDtypeStruct((c_pad.shape[0], n), F32),
        compiler_params=_cparams(("arbitrary",)),
        name="ada_modulation",
    )(c_pad, w, b.reshape(b.shape[0], 1, n))


def _inproj_kernel(x_ref, g_ref, mod_ref, *refs, n_w, shift_idx, scale_idx, tn):
    h = _norm_mod(x_ref[...], g_ref[...], mod_ref, shift_idx, scale_idx).astype(BF16)
    for w_ref, o_ref in zip(refs[:n_w], refs[n_w:]):
        n = o_ref.shape[1]
        step = min(tn, n)
        for j in range(n // step):
            o_ref[:, j * step:(j + 1) * step] = _dot(h, w_ref[:, j * step:(j + 1) * step]).astype(o_ref.dtype)


def in_projection(x, g, rowmod, ws, out_dtypes, *, shift_idx, scale_idx, tm=PROJ_TM, tn=512):
    m = x.shape[0]
    tm = _pick_tile(m, tm)
    gm = tm // ROW_GROUP
    in_specs = [pl.BlockSpec((tm, D_MODEL), lambda i: (i, 0)),
                pl.BlockSpec((1, D_MODEL), lambda i: (0, 0)),
                pl.BlockSpec((gm, 6, D_MODEL), lambda i: (i, 0, 0))]
    in_specs += [pl.BlockSpec(w.shape, lambda i: (0, 0)) for w in ws]
    out_specs = [pl.BlockSpec((tm, w.shape[1]), lambda i: (i, 0)) for w in ws]
    out_shape = [jax.ShapeDtypeStruct((m, w.shape[1]), dt) for w, dt in zip(ws, out_dtypes)]
    return pl.pallas_call(
        functools.partial(_inproj_kernel, n_w=len(ws), shift_idx=shift_idx, scale_idx=scale_idx, tn=tn),
        grid=(m // tm,),
        in_specs=in_specs, out_specs=out_specs, out_shape=out_shape,
        compiler_params=_cparams(("parallel",)),
        name="in_projection",
    )(x, g.reshape(1, D_MODEL), rowmod, *ws)


def _outproj_kernel(*refs, n_pro, pro_fn, gate_idx):
    pro_refs = refs[:n_pro]
    w_ref, x_ref, mod_ref, o_ref = refs[n_pro:]
    a = pro_fn(*pro_refs).astype(BF16)
    acc = _dot(a, w_ref[...])
    o_ref[...] = _rows_gate_residual(x_ref[...], acc, mod_ref, gate_idx)


def out_projection(pro_fn, pro_args, pro_specs, w, x, rowmod, *, gate_idx, tm):
    m = x.shape[0]
    gm = tm // ROW_GROUP
    in_specs = list(pro_specs) + [pl.BlockSpec(w.shape, lambda i: (0, 0)),
                                  pl.BlockSpec((tm, D_MODEL), lambda i: (i, 0)),
                                  pl.BlockSpec((gm, 6, D_MODEL), lambda i: (i, 0, 0))]
    return pl.pallas_call(
        functools.partial(_outproj_kernel, n_pro=len(pro_args), pro_fn=pro_fn, gate_idx=gate_idx),
        grid=(m // tm,),
        in_specs=in_specs,
        out_specs=pl.BlockSpec((tm, D_MODEL), lambda i: (i, 0)),
        out_shape=jax.ShapeDtypeStruct((m, D_MODEL), F32),
        compiler_params=_cparams(("parallel",)),
        name="out_projection",
    )(*pro_args, w, x, rowmod)


def _conv_kernel(u_ref, w_ref, b_ref, o_ref, *, ctx_len):
    x = u_ref[...].astype(F32)
    t_all = x.shape[0]
    t = lax.broadcasted_iota(jnp.int32, (t_all, 1), 0)
    in_ctx = t < ctx_len
    pad = CONV_W // 2
    acc = b_ref[...] + w_ref[pad:pad + 1, :] * x
    for off in range(-pad, pad + 1):
        if off == 0:
            continue
        xs = pltpu.roll(x, (-off) % t_all, 0)
        tt = t + off
        valid = (tt >= 0) & (tt < t_all) & ((tt < ctx_len) == in_ctx)
        acc = acc + w_ref[pad + off:pad + off + 1, :] * jnp.where(valid, xs, 0.0)
    o_ref[...] = _silu(acc).astype(o_ref.dtype)


def dwconv_silu(u, col_off, width, w, b, *, bsz, ctx_len, tc=256):
    m = u.shape[0]
    t_all = m // bsz
    cb = col_off // tc
    return pl.pallas_call(
        functools.partial(_conv_kernel, ctx_len=ctx_len),
        grid=(bsz, width // tc),
        in_specs=[pl.BlockSpec((t_all, tc), lambda bi, j: (bi, j + cb)),
                  pl.BlockSpec((CONV_W, tc), lambda bi, j: (0, j)),
                  pl.BlockSpec((1, tc), lambda bi, j: (0, j))],
        out_specs=pl.BlockSpec((t_all, tc), lambda bi, j: (bi, j)),
        out_shape=jax.ShapeDtypeStruct((m, width), BF16),
        compiler_params=_cparams(("parallel", "parallel")),
        name="dwconv_silu",
    )(u, w, b.reshape(1, width))


def _scan_chunk_index(bwd, c, n_ctx_chunks, n_chunks):
    if not bwd:
        return c
    return jnp.where(c < n_ctx_chunks, n_ctx_chunks - 1 - c, n_chunks - 1 + n_ctx_chunks - c)


def _ssd_scan_kernel(x_ref, b_ref, c_ref, dt_ref, dtt_ref, bias_ref, biast_ref, alog_ref, alogt_ref, exp_ref,
                     y_ref, state, *, bwd):
    ci = pl.program_id(1)
    L = SCAN_L
    H = SSD_HEADS

    @pl.when(ci == 0)
    def _():
        state[...] = jnp.zeros_like(state)

    _, _, visible, tri, trit = _scan_masks(L, bwd)
    dt = _softplus(dt_ref[...] + bias_ref[...])
    dtt = _softplus(dtt_ref[...] + biast_ref[...])
    a_neg = -jnp.exp(alog_ref[...])
    a_negt = -jnp.exp(alogt_ref[...])
    dta = dt * a_neg
    a_col = _dot_sel_l(tri, dta)
    a_row = _dot_sel_r(dtt * a_negt, trit)
    a_end = jnp.sum(dta, axis=0, keepdims=True)
    per_head = jnp.concatenate([jnp.exp(a_col), jnp.exp(a_end - a_col) * dt,
                                jnp.broadcast_to(jnp.exp(a_end), (16, H))], axis=0)
    per_chan = _dot_sel_r(per_head, exp_ref[...])
    ea_x, w_x, eend_x = per_chan[:L], per_chan[L:2 * L], per_chan[2 * L:2 * L + 1]
    lane = lax.broadcasted_iota(jnp.int32, (L, 2 * SSD_P), 1)
    for g in range(SSD_GROUPS):
        gs = slice(g * SSD_GW, (g + 1) * SSD_GW)
        ns = slice(g * SSD_N, (g + 1) * SSD_N)
        bg = b_ref[:, ns]
        cg = c_ref[:, ns]
        xgb = x_ref[:, gs]
        cb = _dot_nt(cg, bg)
        sg = state[:, gs]
        y_inter = _dot(cg, sg.astype(BF16)) * ea_x[:, gs]
        pieces = []
        for pr in range(SSD_GW // (2 * SSD_P)):
            ys = []
            for q in range(2):
                h = g * (SSD_GW // SSD_P) + 2 * pr + q
                rel = a_col[:, h:h + 1] - a_row[h:h + 1, :]
                gm = cb * jnp.exp(jnp.where(visible, rel, NEG_BIG)) * dtt[h:h + 1, :]
                ys.append(_dot(gm.astype(BF16), xgb[:, pr * 2 * SSD_P:(pr + 1) * 2 * SSD_P]))
            pieces.append(jnp.where(lane < SSD_P, ys[0], ys[1]))
        y = jnp.concatenate(pieces, axis=1) + y_inter
        y_ref[:, gs] = y.astype(y_ref.dtype)
        xw = (xgb.astype(F32) * w_x[:, gs]).astype(BF16)
        state[:, gs] = sg * eend_x[:, gs] + _dot_tn(bg, xw)


def ssd_scan(u, dt_dir, dtt_dir, dt_bias, a_log, *, bsz, ctx_len, bwd):
    m = u.shape[0]
    t_all = m // bsz
    nc = t_all // SCAN_L
    ncc = ctx_len // SCAN_L
    L, H = SCAN_L, SSD_HEADS
    d = int(bwd)

    def row(bi, c):
        return bi * nc + _scan_chunk_index(bwd, c, ncc, nc)

    bn = SSD_INNER // (SSD_GROUPS * SSD_N)
    expand = jnp.asarray(np.kron(np.eye(H, dtype=np.float32), np.ones((1, SSD_P), np.float32)), BF16)
    return pl.pallas_call(
        functools.partial(_ssd_scan_kernel, bwd=bwd),
        grid=(bsz, nc),
        in_specs=[pl.BlockSpec((L, SSD_INNER), lambda bi, c: (row(bi, c), 0)),
                  pl.BlockSpec((L, SSD_GROUPS * SSD_N), lambda bi, c: (row(bi, c), bn)),
                  pl.BlockSpec((L, SSD_GROUPS * SSD_N), lambda bi, c: (row(bi, c), bn + 1)),
                  pl.BlockSpec((None, L, H), lambda bi, c: (d, row(bi, c), 0)),
                  pl.BlockSpec((None, H, L), lambda bi, c: (d, 0, row(bi, c))),
                  pl.BlockSpec((None, 1, H), lambda bi, c: (d, 0, 0)),
                  pl.BlockSpec((None, H, 1), lambda bi, c: (d, 0, 0)),
                  pl.BlockSpec((None, 1, H), lambda bi, c: (d, 0, 0)),
                  pl.BlockSpec((None, H, 1), lambda bi, c: (d, 0, 0)),
                  pl.BlockSpec((H, SSD_INNER), lambda bi, c: (0, 0))],
        out_specs=pl.BlockSpec((L, SSD_INNER), lambda bi, c: (row(bi, c), 0)),
        out_shape=jax.ShapeDtypeStruct((m, SSD_INNER), BF16),
        scratch_shapes=[pltpu.VMEM((SSD_N, SSD_INNER), F32)],
        compiler_params=_cparams(("arbitrary", "arbitrary")),
        name="ssd_scan_bwd" if bwd else "ssd_scan_fwd",
    )(u, u, u, dt_dir, dtt_dir, dt_bias.reshape(2, 1, H), dt_bias.reshape(2, H, 1),
      a_log.reshape(2, 1, H), a_log.reshape(2, H, 1), expand)


def _ssd_finish(yf_ref, yb_ref, xs_ref, z_ref, d_ref, g_ref):
    y = yf_ref[...].astype(F32) + yb_ref[...].astype(F32) + d_ref[...] * xs_ref[...].astype(F32)
    y = y * _silu(z_ref[...].astype(F32))
    parts = []
    for g in range(SSD_GROUPS):
        sl = y[:, g * SSD_GW:(g + 1) * SSD_GW]
        parts.append(sl * lax.rsqrt(jnp.mean(sl * sl, axis=-1, keepdims=True) + EPS))
    return jnp.concatenate(parts, axis=1) * g_ref[...]


def _gla_direction(q_ref, f_ref, v_ref, llb_ref, l1m_ref, omlb_ref, place_ref, o_ref, state, *, bwd):
    L, C = SCAN_L, HG_SUB
    nb = L // C
    dk = HG_DK
    ii, jj, visible, tri, _ = _scan_masks(L, bwd)
    band_mask = ((ii // C) == (jj // C)) & visible
    off_mask = ((ii // C) < (jj // C)) if bwd else ((ii // C) > (jj // C))
    last = 0 if bwd else L - 1
    blocks = range(1, nb) if bwd else range(nb - 1)

    def block_end(j):
        return j * C if bwd else (j + 1) * C - 1

    def head(h, carry):
        ls = slice(h * dk, (h + 1) * dk)
        q = _silu(q_ref[:, ls].astype(F32)) * (dk ** -0.5)
        f = f_ref[:, ls].astype(F32)
        v = v_ref[:, ls]
        a = llb_ref[:, ls]
        b = l1m_ref[:, ls] + _log_sigmoid(f)
        lf = (jnp.maximum(a, b) + jnp.log1p(jnp.exp(-jnp.abs(a - b)))) * LOG2E
        k = omlb_ref[:, ls] * jax.nn.sigmoid(-f)
        fd = jnp.exp2(lf)
        cum = _dot_sel_l(tri, lf)
        c_end = cum[last:last + 1, :]
        kd = k
        es = [(q * kd).astype(BF16)]
        for dl in range(1, C):
            kd = fd * pltpu.roll(kd, (L - 1) if bwd else 1, 0)
            es.append((q * kd).astype(BF16))
        band = _dot(jnp.concatenate(es, axis=1), place_ref[...])
        band = pltpu.roll(band, 0, 1, stride=1, stride_axis=0)
        ends = [cum[block_end(j):block_end(j) + 1, :] for j in range(nb)]
        ktb = jnp.concatenate([k[j * C:(j + 1) * C] * jnp.exp2(ends[j] - cum[j * C:(j + 1) * C]) for j in range(nb)],
                              axis=0).astype(BF16)
        sub16 = lax.broadcasted_iota(jnp.int32, (16, 1), 0) // C
        zero_b = jnp.zeros((), BF16)
        kcols, qcols = [], []
        pieces = [None] * nb
        for j in (blocks if bwd else reversed(blocks)):
            t0 = (j * C // 16) * 16
            tile = jnp.where(sub16 == (j * C - t0) // C, ktb[t0:t0 + 16], zero_b)
            kcols.append(jnp.concatenate(
                [x for x in (jnp.zeros((t0, dk), BF16) if t0 else None, tile,
                             jnp.zeros((L - t0 - 16, dk), BF16) if L - t0 - 16 else None) if x is not None], axis=0))
            near = j - 1 if bwd else j + 1
            prev = j - 1 if bwd else j + 1
            if 0 <= prev < nb and prev in blocks:
                step = jnp.exp2(ends[prev] - ends[j])
                for i in (range(0, near) if bwd else range(near + 1, nb)):
                    pieces[i] = pieces[i] * step
            rs = slice(near * C, (near + 1) * C)
            pieces[near] = q[rs] * jnp.exp2(cum[rs] - ends[j])
            zero_f = jnp.zeros((C, dk), F32)
            qcols.append(jnp.concatenate([zero_f if p is None else p for p in pieces], axis=0).astype(BF16))
        off = _dot_nt(jnp.concatenate(qcols, axis=1), jnp.concatenate(kcols, axis=1))
        att = jnp.where(band_mask, band, jnp.where(off_mask, off, 0.0))
        st = state[h]
        o = _dot(att.astype(BF16), v) + _dot_nt((q * jnp.exp2(cum)).astype(BF16), st.astype(BF16))
        o_ref[:, ls] = o.astype(o_ref.dtype)
        kw = (k * jnp.exp2(c_end - cum)).astype(BF16)
        state[h] = st * jnp.exp2(c_end) + _dot_tn(v, kw)
        return carry

    return head


def _gla_scan_kernel(qf_ref, ff_ref, vf_ref, qb_ref, fb_ref, vb_ref, llb_ref, l1m_ref, omlb_ref, pf_ref, pb_ref,
                     of_ref, ob_ref, state):
    @pl.when(pl.program_id(1) == 0)
    def _():
        state[...] = jnp.zeros_like(state)

    fwd = _gla_direction(qf_ref, ff_ref, vf_ref, llb_ref.at[0], l1m_ref.at[0], omlb_ref.at[0], pf_ref, of_ref,
                         state.at[0], bwd=False)
    bwd = _gla_direction(qb_ref, fb_ref, vb_ref, llb_ref.at[1], l1m_ref.at[1], omlb_ref.at[1], pb_ref, ob_ref,
                         state.at[1], bwd=True)
    for h in range(HG_HEADS):
        fwd(h, 0)
        bwd(h, 0)


def _gla_place_matrix(bwd):
    pm = np.zeros((HG_SUB * HG_DK, SCAN_L), np.float32)
    for dl in range(HG_SUB):
        pm[dl * HG_DK:(dl + 1) * HG_DK, dl if bwd else (SCAN_L - dl) % SCAN_L] = 1.0
    return jnp.asarray(pm, BF16)


def gla_scan(proj, llb, l1m, omlb, *, bsz, ctx_len):
    m = proj.shape[0]
    t_all = m // bsz
    nc = t_all // SCAN_L
    ncc = ctx_len // SCAN_L
    L = SCAN_L

    def blk(bwd, col):
        return pl.BlockSpec((L, D_MODEL), lambda bi, c: (bi * nc + _scan_chunk_index(bwd, c, ncc, nc), col))

    vec = pl.BlockSpec((2, 1, D_MODEL), lambda bi, c: (0, 0, 0))
    pf, pb = _gla_place_matrix(False), _gla_place_matrix(True)
    const = pl.BlockSpec(pf.shape, lambda bi, c: (0, 0))
    return pl.pallas_call(
        _gla_scan_kernel,
        grid=(bsz, nc),
        in_specs=[blk(False, 0), blk(False, 1), blk(False, 3), blk(True, 0), blk(True, 2), blk(True, 3),
                  vec, vec, vec, const, const],
        out_specs=[blk(False, 0), blk(True, 0)],
        out_shape=[jax.ShapeDtypeStruct((m, D_MODEL), BF16)] * 2,
        scratch_shapes=[pltpu.VMEM((2, HG_HEADS, HG_DK, HG_DK), F32)],
        compiler_params=_cparams(("arbitrary", "arbitrary")),
        name="gla_scan",
    )(proj, proj, proj, proj, proj, proj, llb, l1m, omlb, pf, pb)


def _hgrn_finish(of_ref, ob_ref, gate_ref, g_ref):
    o = of_ref[...].astype(F32) + ob_ref[...].astype(F32)
    parts = []
    for h in range(HG_HEADS):
        sl = o[:, h * HG_DK:(h + 1) * HG_DK]
        parts.append(sl * lax.rsqrt(jnp.mean(sl * sl, axis=-1, keepdims=True) + EPS))
    return jnp.concatenate(parts, axis=1) * g_ref[...] * _silu(gate_ref[...].astype(F32))


def _attn_qkv_kernel(x_ref, g_ref, mod_ref, w_ref, cos_ref, sin_ref, qg_ref, kg_ref, bd_ref, rot_ref,
                     q_ref, k_ref, v_ref, *, shift_idx, scale_idx):
    h = _norm_mod(x_ref[...], g_ref[...], mod_ref, shift_idx, scale_idx).astype(BF16)
    nq = ATT_HEADS * ATT_HD
    nk = ATT_KV * ATT_HD

    def norm_rope(a, gain, width, out_scale):
        ss = _dot((a * a).astype(BF16), bd_ref[:width, :width])
        an = a * lax.rsqrt(ss * (1.0 / ATT_HD) + EPS) * gain
        rot = _dot(an.astype(BF16), rot_ref[:width, :width])
        cos = jnp.concatenate([cos_ref[...]] * (width // 128), axis=1)
        sin = jnp.concatenate([sin_ref[...]] * (width // 128), axis=1)
        return (an * cos + rot * sin) * out_scale

    q = _dot(h, w_ref[:, :nq])
    q_ref[...] = norm_rope(q, qg_ref[...], nq, ATT_HD ** -0.5 * math.log2(math.e)).astype(q_ref.dtype)
    k = _dot(h, w_ref[:, nq:nq + nk])
    k_ref[...] = norm_rope(k, kg_ref[...], nk, 1.0).astype(k_ref.dtype)
    v_ref[...] = _dot(h, w_ref[:, nq + nk:]).astype(v_ref.dtype)


def attn_qkv(x, g, rowmod, w, cos_t, sin_t, q_g, k_g, bd, rot, *, bsz, shift_idx, scale_idx):
    m = x.shape[0]
    tm = ROW_GROUP
    tpb = (m // bsz) // tm
    nq = ATT_HEADS * ATT_HD
    nk = ATT_KV * ATT_HD
    const = lambda shape: pl.BlockSpec(shape, lambda i: (0, 0))
    return pl.pallas_call(
        functools.partial(_attn_qkv_kernel, shift_idx=shift_idx, scale_idx=scale_idx),
        grid=(m // tm,),
        in_specs=[pl.BlockSpec((tm, D_MODEL), lambda i: (i, 0)),
                  const((1, D_MODEL)),
                  pl.BlockSpec((1, 6, D_MODEL), lambda i: (i, 0, 0)),
                  const(w.shape),
                  pl.BlockSpec((tm, 128), lambda i: (i % tpb, 0)),
                  pl.BlockSpec((tm, 128), lambda i: (i % tpb, 0)),
                  const((1, nq)), const((1, nk)), const(bd.shape), const(rot.shape)],
        out_specs=[pl.BlockSpec((tm, nq), lambda i: (i, 0)),
                   pl.BlockSpec((tm, nk), lambda i: (i, 0)),
                   pl.BlockSpec((tm, nk), lambda i: (i, 0))],
        out_shape=[jax.ShapeDtypeStruct((m, nq), BF16),
                   jax.ShapeDtypeStruct((m, nk), BF16),
                   jax.ShapeDtypeStruct((m, nk), BF16)],
        compiler_params=_cparams(("parallel",)),
        name="attn_qkv",
    )(x, g.reshape(1, D_MODEL), rowmod, w, cos_t, sin_t, q_g, k_g, bd, rot)


def _attn_kernel(q_ref, k_ref, v_ref, o_ref, kaug, vtaug, kmax2, *, ctx_len):
    qt = pl.program_id(2)
    tq = q_ref.shape[0]
    hd = ATT_HD
    lane = lax.broadcasted_iota(jnp.int32, (1, 2 * hd), 1)
    n_r = q_ref.shape[1] // (2 * hd)
    aug_lane = (hd, 0)
    ii = lax.broadcasted_iota(jnp.int32, (2 * hd, 2 * hd), 0)
    jj = lax.broadcasted_iota(jnp.int32, (2 * hd, 2 * hd), 1)
    half_sum = jnp.where((ii // hd) == (jj // hd), 1.0, 0.0).astype(BF16)

    @pl.when(qt == 0)
    def _():
        k = k_ref[...]
        one = jnp.ones((), BF16)
        vt = v_ref[...].astype(F32).T.astype(BF16)
        ones_rows = jnp.ones((ATT_ONES, vt.shape[1]), BF16)
        for j in range(2):
            vtaug[j, 0:hd, :] = vt[j * hd:(j + 1) * hd, :]
            vtaug[j, hd:hd + ATT_ONES, :] = ones_rows
            kaug[j] = jnp.where(lane == aug_lane[j], one, k)
        kf = k.astype(F32)
        kn2 = _dot((kf * kf).astype(BF16), half_sum)
        kmax2[...] = jnp.broadcast_to(jnp.max(kn2, axis=0, keepdims=True), kmax2.shape)

    def queries(j):
        sel = (lane // hd) == j
        return jnp.concatenate(
            [jnp.where(sel, q_ref[:, r * 128:(r + 1) * 128], jnp.zeros((), BF16)) for r in range(n_r)], axis=0)

    def values(j, pt, n_keys):
        ot = _dot(vtaug[j, :, :n_keys], pt)
        return ot[0:hd] / ot[hd:hd + 1]

    def finish(tops):
        o = jnp.concatenate(tops, axis=0).T
        for r in range(n_r):
            o_ref[:, r * 128:(r + 1) * 128] = o[r * tq:(r + 1) * tq].astype(o_ref.dtype)

    def attend_exact(n_keys):
        tops = []
        for j in range(2):
            st = _dot_nt(k_ref[:n_keys, :], queries(j))
            mx = jnp.max(st, axis=0, keepdims=True)
            tops.append(values(j, jnp.exp2(st - mx).astype(BF16), n_keys))
        finish(tops)

    def attend_shifted(n_keys, shifts):
        tops = []
        for j in range(2):
            qa = jnp.where(lane == aug_lane[j], (-shifts[j]).astype(BF16), queries(j))
            pt = jnp.exp2(_dot_nt(kaug[j, :n_keys, :], qa)).astype(BF16)
            tops.append(values(j, pt, n_keys))
        finish(tops)

    def attend(n_keys):
        shifts = []
        ones = jnp.ones((2 * hd, 2 * hd), BF16)
        for j in range(2):
            qf = queries(j).astype(F32)
            qn2 = _dot((qf * qf).astype(BF16), ones)
            shifts.append(jnp.sqrt(qn2 * kmax2[0:1, j * hd:j * hd + 1]) * 1.02)
        worst = jnp.max(jnp.maximum(shifts[0], shifts[1]))
        small = worst < ATT_SHIFT_MAX

        @pl.when(small)
        def _():
            attend_shifted(n_keys, shifts)

        @pl.when(jnp.logical_not(small))
        def _():
            attend_exact(n_keys)

    n_ctx_tiles = ctx_len // tq

    @pl.when(qt < n_ctx_tiles)
    def _():
        attend(ctx_len)

    @pl.when(qt >= n_ctx_tiles)
    def _():
        attend(k_ref.shape[0])


def attention(q, k, v, *, bsz, ctx_len):
    m = q.shape[0]
    t_all = m // bsz
    tq = ATT_TQ
    nqt = t_all // tq
    n_pairs = ATT_KV // 2
    qw = q.shape[1] // n_pairs
    return pl.pallas_call(
        functools.partial(_attn_kernel, ctx_len=ctx_len),
        grid=(bsz, n_pairs, nqt),
        in_specs=[pl.BlockSpec((tq, qw), lambda bi, p, t: (bi * nqt + t, p)),
                  pl.BlockSpec((t_all, 2 * ATT_HD), lambda bi, p, t: (bi, p)),
                  pl.BlockSpec((t_all, 2 * ATT_HD), lambda bi, p, t: (bi, p))],
        out_specs=pl.BlockSpec((tq, qw), lambda bi, p, t: (bi * nqt + t, p)),
        out_shape=jax.ShapeDtypeStruct(q.shape, BF16),
        scratch_shapes=[pltpu.VMEM((2, t_all, 2 * ATT_HD), BF16),
                        pltpu.VMEM((2, ATT_HD + ATT_ONES, t_all), BF16),
                        pltpu.VMEM((8, 2 * ATT_HD), F32)],
        compiler_params=_cparams(("arbitrary", "arbitrary", "arbitrary")),
        name="attention",
    )(q, k, v)


def _identity_pro(a_ref):
    return a_ref[...]


def _mlstm_qkv_kernel(xc_ref, xm_ref, wq_ref, wk_ref, wv_ref, wg_ref, bg_ref, q_ref, k_ref, v_ref, gate_ref):
    gates = jnp.zeros(gate_ref.shape, F32) + bg_ref[...]
    for which, (src, w_ref, o_ref) in enumerate(((xc_ref, wq_ref, q_ref), (xc_ref, wk_ref, k_ref),
                                                 (xm_ref, wv_ref, v_ref))):
        for h in range(ML_HEADS):
            hs = slice(h * ML_DH, (h + 1) * ML_DH)
            r = _dot(src[:, hs], w_ref[h]).astype(BF16)
            o_ref[:, hs] = r
            gates = gates + _dot(r, wg_ref[which * ML_INNER + h * ML_DH:which * ML_INNER + (h + 1) * ML_DH, :])
    gate_ref[...] = gates


def mlstm_qkv(xc, up, wq, wk, wv, wg, bg, *, tm=PROJ_TM):
    m = xc.shape[0]
    ng = wg.shape[1]
    const = lambda shape: pl.BlockSpec(shape, lambda i: (0,) * len(shape))
    row = pl.BlockSpec((tm, ML_INNER), lambda i: (i, 0))
    return pl.pallas_call(
        _mlstm_qkv_kernel,
        grid=(m // tm,),
        in_specs=[row, row, const(wq.shape), const(wk.shape), const(wv.shape), const(wg.shape), const((1, ng))],
        out_specs=[row, row, row, pl.BlockSpec((tm, ng), lambda i: (i, 0))],
        out_shape=[jax.ShapeDtypeStruct((m, ML_INNER), BF16)] * 3 + [jax.ShapeDtypeStruct((m, ng), F32)],
        compiler_params=_cparams(("parallel",)),
        name="mlstm_qkv",
    )(xc, up, wq, wk, wv, wg, bg.reshape(1, ng))


def _mlstm_scan_kernel(q_ref, k_ref, v_ref, gt_ref, gtt_ref, h_ref, c_st, n_st, m_st, *, bwd):
    ci = pl.program_id(1)
    L = ML_L
    nh = ML_HEADS
    scale = ML_DH ** -0.5

    @pl.when(ci == 0)
    def _():
        c_st[...] = jnp.zeros_like(c_st)
        n_st[...] = jnp.zeros_like(n_st)
        m_st[...] = jnp.full(m_st.shape, NEG_BIG, F32)

    _, _, visible, tri, trit = _scan_masks(L, bwd)
    gt = gt_ref[...]
    gtt = gtt_ref[...]
    li_c, li_r = gt[:, :nh], gtt[:nh, :]
    lf_c, lf_r = _log_sigmoid(gt[:, nh:]), _log_sigmoid(gtt[nh:, :])
    cum_c = _dot_sel_l(tri, lf_c)
    cum_r = _dot_sel_r(lf_r, trit)
    end_c = jnp.sum(lf_c, axis=0, keepdims=True)
    for h in range(nh):
        hs = slice(h * ML_DH, (h + 1) * ML_DH)
        q = q_ref[:, hs]
        kb = k_ref[:, hs]
        k = kb.astype(F32)
        v = v_ref[:, hs]
        m_prev = m_st[h:h + 1, 0:1]
        cum_end = end_c[:, h:h + 1]
        dmat = jnp.where(visible, cum_c[:, h:h + 1] - cum_r[h:h + 1, :] + li_r[h:h + 1, :], -jnp.inf)
        inter = cum_c[:, h:h + 1] + m_prev
        m_t = jnp.maximum(inter, jnp.max(dmat, axis=1, keepdims=True))
        w = jnp.exp(dmat - m_t)
        w_c = jnp.exp(inter - m_t)
        qk = _dot_nt(q, kb) * scale * w
        cmat = c_st[h]
        num = _dot(qk.astype(BF16), v) + w_c * _dot(q, cmat.astype(BF16))
        qn = jnp.sum(q.astype(F32) * n_st[h:h + 1, :], axis=1, keepdims=True)
        den = jnp.sum(qk, axis=1, keepdims=True) + w_c * qn
        hv = num / jnp.maximum(jnp.abs(den), jnp.exp(-m_t))
        h_ref[:, hs] = hv.astype(h_ref.dtype)
        wend_c = cum_end - cum_c[:, h:h + 1] + li_c[:, h:h + 1]
        wend_r = cum_end - cum_r[h:h + 1, :] + li_r[h:h + 1, :]
        m_new = jnp.maximum(cum_end + m_prev, jnp.max(wend_r, axis=1, keepdims=True))
        a_old = jnp.exp(cum_end + m_prev - m_new)
        e_c = jnp.exp(wend_c - m_new) * scale
        e_r = jnp.exp(wend_r - m_new) * scale
        c_st[h] = a_old * cmat + _dot_tn((k * e_c).astype(BF16), v)
        e_r8 = jnp.broadcast_to(e_r, (8, L)).astype(BF16)
        n_st[h:h + 1, :] = a_old * n_st[h:h + 1, :] + _dot(e_r8, kb)[0:1, :]
        m_st[h:h + 1, :] = jnp.broadcast_to(m_new, (1, m_st.shape[1]))


def mlstm_scan(q, k, v, gt_dir, gtt_dir, *, bsz, ctx_len, bwd):
    m = q.shape[0]
    t_all = m // bsz
    nc = t_all // ML_L
    ncc = ctx_len // ML_L
    L, nh = ML_L, ML_HEADS
    d = int(bwd)

    def row(bi, c):
        return bi * nc + _scan_chunk_index(bwd, c, ncc, nc)

    blk = pl.BlockSpec((L, ML_INNER), lambda bi, c: (row(bi, c), 0))
    return pl.pallas_call(
        functools.partial(_mlstm_scan_kernel, bwd=bwd),
        grid=(bsz, nc),
        in_specs=[blk, blk, blk,
                  pl.BlockSpec((None, L, 2 * nh), lambda bi, c: (d, row(bi, c), 0)),
                  pl.BlockSpec((None, 2 * nh, L), lambda bi, c: (d, 0, row(bi, c)))],
        out_specs=pl.BlockSpec((L, ML_INNER), lambda bi, c: (row(bi, c), 0)),
        out_shape=jax.ShapeDtypeStruct((m, ML_INNER), BF16),
        scratch_shapes=[pltpu.VMEM((nh, ML_DH, ML_DH), F32),
                        pltpu.VMEM((8, ML_DH), F32),
                        pltpu.VMEM((8, 128), F32)],
        compiler_params=_cparams(("arbitrary", "arbitrary")),
        name="mlstm_scan_bwd" if bwd else "mlstm_scan_fwd",
    )(q, k, v, gt_dir, gtt_dir)


def _mlstm_finish(hf_ref, hb_ref, xc_ref, z_ref, skip_ref, g_ref):
    hsum = hf_ref[...].astype(F32) + hb_ref[...].astype(F32)
    parts = []
    for h in range(ML_HEADS):
        sl = hsum[:, h * ML_DH:(h + 1) * ML_DH]
        parts.append(sl * lax.rsqrt(jnp.mean(sl * sl, axis=-1, keepdims=True) + EPS))
    hn = jnp.concatenate(parts, axis=1) * g_ref[...]
    return (hn + skip_ref[...] * xc_ref[...].astype(F32)) * _silu(z_ref[...].astype(F32))


def _ffn_kernel(x_ref, g_ref, mod_ref, w1_ref, w3_ref, w2_ref, o_ref, h_sc, acc_sc, *, shift_idx, scale_idx, gate_idx):
    f = pl.program_id(1)

    @pl.when(f == 0)
    def _():
        h_sc[...] = _norm_mod(x_ref[...], g_ref[...], mod_ref, shift_idx, scale_idx).astype(BF16)
        acc_sc[...] = jnp.zeros_like(acc_sc)

    h = h_sc[...]
    a = _silu(_dot(h, w1_ref[...])) * _dot(h, w3_ref[...])
    acc_sc[...] += _dot(a.astype(BF16), w2_ref[...])

    @pl.when(f == pl.num_programs(1) - 1)
    def _():
        o_ref[...] = _rows_gate_residual(x_ref[...], acc_sc[...], mod_ref, gate_idx)


def dense_ffn(x, g, rowmod, w1, w3, w2, layer, *, tm=1024, tf=512):
    m = x.shape[0]
    tm = _pick_tile(m, tm)
    gm = tm // ROW_GROUP
    nf = D_FF // tf
    return pl.pallas_call(
        functools.partial(_ffn_kernel, shift_idx=3, scale_idx=4, gate_idx=5),
        grid=(m // tm, nf),
        in_specs=[pl.BlockSpec((tm, D_MODEL), lambda i, f: (i, 0)),
                  pl.BlockSpec((1, D_MODEL), lambda i, f: (0, 0)),
                  pl.BlockSpec((gm, 6, D_MODEL), lambda i, f: (i, 0, 0)),
                  pl.BlockSpec((None, D_MODEL, tf), lambda i, f: (layer, 0, f)),
                  pl.BlockSpec((None, D_MODEL, tf), lambda i, f: (layer, 0, f)),
                  pl.BlockSpec((None, tf, D_MODEL), lambda i, f: (layer, f, 0))],
        out_specs=pl.BlockSpec((tm, D_MODEL), lambda i, f: (i, 0)),
        out_shape=jax.ShapeDtypeStruct((m, D_MODEL), F32),
        scratch_shapes=[pltpu.VMEM((tm, D_MODEL), BF16), pltpu.VMEM((tm, D_MODEL), F32)],
        compiler_params=_cparams(("parallel", "arbitrary")),
        name="dense_ffn",
    )(x, g.reshape(1, D_MODEL), rowmod, w1, w3, w2)


def _router_kernel(x_ref, g_ref, mod_ref, wr_ref, h_ref, route_ref, cnt_ref, cnt_sc, *, shift_idx, scale_idx, n_real):
    i = pl.program_id(0)

    @pl.when(i == 0)
    def _():
        cnt_sc[...] = jnp.zeros_like(cnt_sc)

    @pl.when(i < n_real)
    def _():
        h = _norm_mod(x_ref[...], g_ref[...], mod_ref, shift_idx, scale_idx)
        h_ref[...] = h.astype(h_ref.dtype)
        logits = _dot_hi(h, wr_ref[...])
        tm = logits.shape[0]
        lane = lax.broadcasted_iota(jnp.int32, logits.shape, 1)
        logits = jnp.where(lane < N_EXPERTS, logits, -jnp.inf)
        m1 = jnp.max(logits, axis=-1, keepdims=True)
        i1 = jnp.min(jnp.where(logits == m1, lane, 128), axis=-1, keepdims=True)
        rest = jnp.where(lane == i1, -jnp.inf, logits)
        m2 = jnp.max(rest, axis=-1, keepdims=True)
        i2 = jnp.min(jnp.where(rest == m2, lane, 128), axis=-1, keepdims=True)
        e2 = jnp.exp(m2 - m1)
        w1 = 1.0 / (1.0 + e2)
        w2 = e2 / (1.0 + e2)
        chosen = jnp.where(lane == i1, 1.0, jnp.where(lane == i2, 1.0, 0.0))
        ii = lax.broadcasted_iota(jnp.int32, (tm, tm), 0)
        jj = lax.broadcasted_iota(jnp.int32, (tm, tm), 1)
        strict = jnp.where(ii > jj, 1.0, 0.0).astype(BF16)
        prefix = _dot(strict, chosen.astype(BF16)) + cnt_sc[0:1, :]
        r1 = jnp.sum(jnp.where(lane == i1, prefix, 0.0), axis=-1, keepdims=True)
        r2 = jnp.sum(jnp.where(lane == i2, prefix, 0.0), axis=-1, keepdims=True)
        cnt_sc[...] = cnt_sc[...] + jnp.sum(chosen, axis=0, keepdims=True)
        route = jnp.zeros(logits.shape, F32)
        for col, val in enumerate((i1.astype(F32), i2.astype(F32), w1, w2, r1, r2)):
            route = jnp.where(lane == col, val, route)
        route_ref[...] = route

    @pl.when(i >= n_real)
    def _():
        h_ref[...] = jnp.zeros_like(h_ref)
        route_ref[...] = jnp.zeros_like(route_ref)

    cnt_ref[...] = cnt_sc[...]


def moe_router(x, g, rowmod, wr_pad, p_rows, *, tm=PROJ_TM):
    m = x.shape[0]
    n_real = m // tm
    gm = tm // ROW_GROUP
    clamp = lambda i: jnp.minimum(i, n_real - 1)
    return pl.pallas_call(
        functools.partial(_router_kernel, shift_idx=3, scale_idx=4, n_real=n_real),
        grid=(p_rows // tm,),
        in_specs=[pl.BlockSpec((tm, D_MODEL), lambda i: (clamp(i), 0)),
                  pl.BlockSpec((1, D_MODEL), lambda i: (0, 0)),
                  pl.BlockSpec((gm, 6, D_MODEL), lambda i: (clamp(i), 0, 0)),
                  pl.BlockSpec(wr_pad.shape, lambda i: (0, 0))],
        out_specs=[pl.BlockSpec((tm, D_MODEL), lambda i: (i, 0)), pl.BlockSpec((tm, 128), lambda i: (i, 0)),
                   pl.BlockSpec((8, 128), lambda i: (0, 0))],
        out_shape=[jax.ShapeDtypeStruct((p_rows, D_MODEL), BF16), jax.ShapeDtypeStruct((p_rows, 128), F32),
                   jax.ShapeDtypeStruct((8, 128), F32)],
        scratch_shapes=[pltpu.VMEM((8, 128), F32)],
        compiler_params=_cparams(("arbitrary",)),
        name="moe_router",
    )(x, g.reshape(1, D_MODEL), rowmod, wr_pad)


def _moe_ffn_kernel(te_ref, nt_ref, h_ref, w1_ref, w3_ref, w2_ref, o_ref, acc_sc):
    i = pl.program_id(0)
    f = pl.program_id(1)
    live = i < nt_ref[0]

    @pl.when(f == 0)
    def _():
        acc_sc[...] = jnp.zeros_like(acc_sc)

    @pl.when(live)
    def _():
        h = h_ref[...]
        a = _silu(_dot(h, w1_ref[...].astype(BF16))) * _dot(h, w3_ref[...].astype(BF16))
        acc_sc[...] += _dot(a.astype(BF16), w2_ref[...].astype(BF16))

    @pl.when(f == pl.num_programs(1) - 1)
    def _():
        o_ref[...] = acc_sc[...].astype(o_ref.dtype)


def moe_grouped_ffn(h_sorted, tile_expert, n_tiles, w1, w3, w2, layer, *, tm=MOE_TM, tf=512):
    p = h_sorted.shape[0]
    nf = D_FF // tf
    grid_spec = pltpu.PrefetchScalarGridSpec(
        num_scalar_prefetch=2,
        grid=(p // tm, nf),
        in_specs=[pl.BlockSpec((tm, D_MODEL), lambda i, f, te, nt: (i, 0)),
                  pl.BlockSpec((None, None, D_MODEL, tf), lambda i, f, te, nt: (layer, te[i], 0, f)),
                  pl.BlockSpec((None, None, D_MODEL, tf), lambda i, f, te, nt: (layer, te[i], 0, f)),
                  pl.BlockSpec((None, None, tf, D_MODEL), lambda i, f, te, nt: (layer, te[i], f, 0))],
        out_specs=pl.BlockSpec((tm, D_MODEL), lambda i, f, te, nt: (i, 0)),
        scratch_shapes=[pltpu.VMEM((tm, D_MODEL), F32)])
    return pl.pallas_call(
        _moe_ffn_kernel,
        grid_spec=grid_spec,
        out_shape=jax.ShapeDtypeStruct((p, D_MODEL), BF16),
        compiler_params=_cparams(("arbitrary", "arbitrary")),
        name="moe_grouped_ffn",
    )(tile_expert, n_tiles, h_sorted, w1, w3, w2)


def _moe_combine_kernel(x_ref, ya_ref, yb_ref, route_ref, mod_ref, o_ref, *, gate_idx):
    r = route_ref[...]
    y = r[:, 2:3] * ya_ref[...].astype(F32) + r[:, 3:4] * yb_ref[...].astype(F32)
    o_ref[...] = _rows_gate_residual(x_ref[...], y, mod_ref, gate_idx)


def moe_combine(x, ya, yb, route, rowmod, *, latent_only=None):
    m = x.shape[0]
    tm = ROW_GROUP
    if latent_only is None:
        n_out, src = m // tm, (lambda i: i)
    else:
        bsz, ctx_len = latent_only
        ctx_tiles = ctx_len // tm
        lat_tiles = m // bsz // tm - ctx_tiles
        n_out, src = bsz * lat_tiles, (lambda i: i + (i // lat_tiles + 1) * ctx_tiles)
    row = pl.BlockSpec((tm, D_MODEL), lambda i: (src(i), 0))
    return pl.pallas_call(
        functools.partial(_moe_combine_kernel, gate_idx=5),
        grid=(n_out,),
        in_specs=[row, row, row, pl.BlockSpec((tm, 128), lambda i: (src(i), 0)),
                  pl.BlockSpec((1, 6, D_MODEL), lambda i: (src(i), 0, 0))],
        out_specs=pl.BlockSpec((tm, D_MODEL), lambda i: (i, 0)),
        out_shape=jax.ShapeDtypeStruct((n_out * tm, D_MODEL), F32),
        compiler_params=_cparams(("parallel",)),
        name="moe_combine",
    )(x, ya, yb, route, rowmod)


def moe_ffn(x, g, rowmod, w_router, w1, w3, w2, layer, latent_only=None):
    m = x.shape[0]
    tm = MOE_TM
    n_tiles_max = (2 * m) // tm + N_EXPERTS
    p = n_tiles_max * tm
    wr_pad = jnp.zeros((D_MODEL, 128), F32).at[:, :N_EXPERTS].set(w_router)
    h, route, cnt = moe_router(x, g, rowmod, wr_pad, p)
    experts = jnp.arange(N_EXPERTS, dtype=jnp.int32)
    counts = cnt[0, :N_EXPERTS].astype(jnp.int32)
    tiles_per = (counts + tm - 1) // tm
    tile_end = jnp.cumsum(tiles_per)
    grp_start = (tile_end - tiles_per) * tm
    cnt_start = jnp.cumsum(counts) - counts
    e12 = route[:m, 0:2].astype(jnp.int32)
    r12 = route[:m, 4:6].astype(jnp.int32)
    pos12 = jnp.sum(jnp.where(e12[:, :, None] == experts, grp_start, 0), axis=-1) + r12
    order = jnp.argsort(e12.reshape(-1), stable=True).astype(jnp.int32)
    tile_expert = jnp.minimum(
        jnp.sum(jnp.arange(n_tiles_max, dtype=jnp.int32)[:, None] >= tile_end[None, :], axis=1), N_EXPERTS - 1
    ).astype(jnp.int32)
    n_tiles = tile_end[-1:].astype(jnp.int32)
    rank = jnp.arange(p, dtype=jnp.int32) - jnp.repeat(grp_start[tile_expert], tm)
    in_use = rank < jnp.repeat(counts[tile_expert], tm)
    sorted_idx = jnp.clip(jnp.repeat(cnt_start[tile_expert], tm) + rank, 0, 2 * m - 1)
    src_token = jnp.where(in_use, jnp.take(order, sorted_idx, mode="clip") // 2, jnp.arange(p, dtype=jnp.int32) % m)
    h_sorted = jnp.take(h, src_token, axis=0, mode="clip")
    y_sorted = moe_grouped_ffn(h_sorted, tile_expert, n_tiles, w1, w3, w2, layer)
    ya = jnp.take(y_sorted, pos12[:, 0], axis=0, mode="clip")
    yb = jnp.take(y_sorted, pos12[:, 1], axis=0, mode="clip")
    return moe_combine(x, ya, yb, route, rowmod, latent_only=latent_only)


def _dir_split(a, n):
    m = a.shape[0]
    a3 = a.reshape(m, 2, n)
    return jnp.transpose(a3, (1, 0, 2)), jnp.transpose(a3, (1, 2, 0))


def ssd_layer(x, g, rowmod, w_in, conv_w, conv_b, dt_bias, a_log, d_skip, norm_g, w_out, *, bsz, ctx_len):
    m = x.shape[0]
    n_main = SSD_INNER + SSD_INNER + 2 * SSD_GROUPS * SSD_N
    zx, dt_raw = in_projection(x, g, rowmod, [w_in[:, :n_main].astype(BF16), w_in[:, n_main:].astype(BF16)],
                               [BF16, F32], shift_idx=0, scale_idx=1)
    u = dwconv_silu(zx, SSD_INNER, n_main - SSD_INNER, conv_w, conv_b, bsz=bsz, ctx_len=ctx_len)
    dt_dir, dtt_dir = _dir_split(dt_raw, SSD_HEADS)
    yf = ssd_scan(u, dt_dir, dtt_dir, dt_bias, a_log, bsz=bsz, ctx_len=ctx_len, bwd=False)
    yb = ssd_scan(u, dt_dir, dtt_dir, dt_bias, a_log, bsz=bsz, ctx_len=ctx_len, bwd=True)
    tm = PROJ_TM
    d_x = jnp.repeat(d_skip.astype(F32), SSD_P).reshape(1, SSD_INNER)
    specs = [pl.BlockSpec((tm, SSD_INNER), lambda i: (i, 0)),
             pl.BlockSpec((tm, SSD_INNER), lambda i: (i, 0)),
             pl.BlockSpec((tm, SSD_INNER), lambda i: (i, 0)),
             pl.BlockSpec((tm, SSD_INNER), lambda i: (i, 0)),
             pl.BlockSpec((1, SSD_INNER), lambda i: (0, 0)),
             pl.BlockSpec((1, SSD_INNER), lambda i: (0, 0))]
    return out_projection(_ssd_finish, [yf, yb, u, zx, d_x, norm_g.reshape(1, SSD_INNER)], specs,
                          w_out.astype(BF16), x, rowmod, gate_idx=2, tm=tm)


def hgrn_layer(x, g, rowmod, w_in, lb, norm_g, w_out, *, bsz, ctx_len):
    proj, = in_projection(x, g, rowmod, [w_in.astype(BF16)], [BF16], shift_idx=0, scale_idx=1)
    lb = lb.astype(F32).reshape(2, 1, D_MODEL)
    lbs = (jnp.log(lb), jnp.log1p(-lb), 1.0 - lb)
    of, ob = gla_scan(proj, *lbs, bsz=bsz, ctx_len=ctx_len)
    tm = PROJ_TM
    specs = [pl.BlockSpec((tm, D_MODEL), lambda i: (i, 0)),
             pl.BlockSpec((tm, D_MODEL), lambda i: (i, 0)),
             pl.BlockSpec((tm, D_MODEL), lambda i: (i, 4)),
             pl.BlockSpec((1, D_MODEL), lambda i: (0, 0))]
    return out_projection(_hgrn_finish, [of, ob, proj, norm_g.reshape(1, D_MODEL)], specs,
                          w_out.astype(BF16), x, rowmod, gate_idx=2, tm=tm)


def _attn_head_perm():
    r_per = ATT_HEADS // ATT_KV
    heads = [(2 * p + j) * r_per + r for p in range(ATT_KV // 2) for r in range(r_per) for j in range(2)]
    return np.concatenate([np.arange(h * ATT_HD, (h + 1) * ATT_HD) for h in heads])


def _rope_tables(seq_len, ctx_len, grid_w):
    rows = seq_len // grid_w
    row = jnp.repeat(jnp.arange(rows, dtype=F32), grid_w)
    col = jnp.tile(jnp.arange(grid_w, dtype=F32), rows)
    inv = ROPE_THETA ** (-jnp.arange(ROPE_FREQS, dtype=F32) / ROPE_FREQS)
    ang_r = row[:, None] * inv
    ang_c = col[:, None] * inv
    ang = jnp.concatenate([ang_r, ang_r, ang_c, ang_c], axis=-1)
    cos = jnp.concatenate([jnp.ones((ctx_len, ATT_HD), F32), jnp.cos(ang)], axis=0)
    sin = jnp.concatenate([jnp.zeros((ctx_len, ATT_HD), F32), jnp.sin(ang)], axis=0)
    return jnp.tile(cos, (1, 2)), jnp.tile(sin, (1, 2))


def _rope_matrices():
    r64 = np.zeros((ATT_HD, ATT_HD), np.float32)
    fq = ROPE_FREQS
    for ax in range(2):
        o = ax * 2 * fq
        for i in range(fq):
            r64[o + fq + i, o + i] = -1.0
            r64[o + i, o + fq + i] = 1.0
    n = ATT_HEADS
    bd = np.kron(np.eye(n, dtype=np.float32), np.ones((ATT_HD, ATT_HD), np.float32))
    rot = np.kron(np.eye(n, dtype=np.float32), r64)
    return jnp.asarray(bd, BF16), jnp.asarray(rot, BF16)


def attn_layer(x, g, rowmod, w_qkv, q_g, k_g, w_o, *, bsz, ctx_len, grid_w):
    m = x.shape[0]
    seq_len = m // bsz - ctx_len
    perm = _attn_head_perm()
    nq = ATT_HEADS * ATT_HD
    w = jnp.concatenate([w_qkv[:, :nq][:, perm], w_qkv[:, nq:]], axis=1).astype(BF16)
    cos_t, sin_t = _rope_tables(seq_len, ctx_len, grid_w)
    bd, rot = _rope_matrices()
    qg = jnp.tile(q_g.astype(F32), ATT_HEADS).reshape(1, nq)
    kg = jnp.tile(k_g.astype(F32), ATT_KV).reshape(1, ATT_KV * ATT_HD)
    q, k, v = attn_qkv(x, g, rowmod, w, cos_t, sin_t, qg, kg, bd, rot, bsz=bsz, shift_idx=0, scale_idx=1)
    o = attention(q, k, v, bsz=bsz, ctx_len=ctx_len)
    tm = PROJ_TM
    specs = [pl.BlockSpec((tm, nq), lambda i: (i, 0))]
    return out_projection(_identity_pro, [o], specs, w_o[perm, :].astype(BF16), x, rowmod, gate_idx=2, tm=tm)


def mlstm_layer(x, g, rowmod, w_up, conv_w, conv_b, w_q, w_k, w_v, w_gate, b_gate, skip, norm_g, w_down,
                *, bsz, ctx_len):
    up, = in_projection(x, g, rowmod, [w_up.astype(BF16)], [BF16], shift_idx=0, scale_idx=1)
    xc = dwconv_silu(up, 0, ML_INNER, conv_w, conv_b, bsz=bsz, ctx_len=ctx_len)
    q, k, v, gates = mlstm_qkv(xc, up, w_q.astype(BF16), w_k.astype(BF16), w_v.astype(BF16),
                               w_gate.astype(BF16), b_gate)
    gt_dir, gtt_dir = _dir_split(gates, 2 * ML_HEADS)
    hf = mlstm_scan(q, k, v, gt_dir, gtt_dir, bsz=bsz, ctx_len=ctx_len, bwd=False)
    hb = mlstm_scan(q, k, v, gt_dir, gtt_dir, bsz=bsz, ctx_len=ctx_len, bwd=True)
    tm = PROJ_TM
    specs = [pl.BlockSpec((tm, ML_INNER), lambda i: (i, 0)),
             pl.BlockSpec((tm, ML_INNER), lambda i: (i, 0)),
             pl.BlockSpec((tm, ML_INNER), lambda i: (i, 0)),
             pl.BlockSpec((tm, ML_INNER), lambda i: (i, 1)),
             pl.BlockSpec((1, ML_INNER), lambda i: (0, 0)),
             pl.BlockSpec((1, ML_INNER), lambda i: (0, 0))]
    return out_projection(_mlstm_finish, [hf, hb, xc, up, skip.reshape(1, ML_INNER), norm_g.reshape(1, ML_INNER)],
                          specs, w_down.astype(BF16), x, rowmod, gate_idx=2, tm=tm)


def kernel(x, c, ctx, c_ctx, ada_w, ada_b, norm_g, ssd_w_in, ssd_conv_w, ssd_conv_b, ssd_dt_bias, ssd_a_log, ssd_d, ssd_norm_g, ssd_w_out, hgrn_w_in, hgrn_lb, hgrn_norm_g, hgrn_w_out, attn_w_qkv, attn_q_g, attn_k_g, attn_w_o, mlstm_w_up, mlstm_conv_w, mlstm_conv_b, mlstm_w_q, mlstm_w_k, mlstm_w_v, mlstm_w_gate, mlstm_b_gate, mlstm_skip, mlstm_norm_g, mlstm_w_down, ffn_w1, ffn_w3, ffn_w2, moe_router, moe_w1, moe_w3, moe_w2):
    bsz, seq_len, _ = x.shape
    ctx_len = ctx.shape[1]
    depth = ada_w.shape[0]
    grid_w = 64
    t_all = ctx_len + seq_len
    m = bsz * t_all
    xa = jnp.concatenate([ctx, x], axis=1).reshape(m, D_MODEL)
    c_pad = jnp.zeros((8, D_MODEL), F32).at[:bsz].set(c).at[bsz].set(c_ctx)
    groups_per_batch = t_all // ROW_GROUP
    ctx_groups = ctx_len // ROW_GROUP
    gidx = np.array([bsz if (gi % groups_per_batch) < ctx_groups else gi // groups_per_batch
                     for gi in range(m // ROW_GROUP)], np.int32)
    lb_all = jnp.cumsum(jax.nn.softmax(hgrn_lb.astype(F32), axis=1), axis=1)
    lb_all = lb_all - lb_all[:, :1]
    kw = dict(bsz=bsz, ctx_len=ctx_len)
    ffn_w = [w.astype(BF16) for w in (ffn_w1, ffn_w3, ffn_w2)]
    moe_w = (moe_w1, moe_w3, moe_w2)
    for i in range(depth):
        mod = ada_modulation(c_pad, ada_w, ada_b, i).reshape(8, 6, D_MODEL)
        rowmod = mod[gidx]
        kind, j = i % 4, i // 4
        if kind == 0:
            xa = ssd_layer(xa, norm_g[i, 0], rowmod, ssd_w_in[j], ssd_conv_w[j], ssd_conv_b[j], ssd_dt_bias[j],
                           ssd_a_log[j], ssd_d[j], ssd_norm_g[j], ssd_w_out[j], **kw)
        elif kind == 1:
            xa = hgrn_layer(xa, norm_g[i, 0], rowmod, hgrn_w_in[j], lb_all[:, i], hgrn_norm_g[j], hgrn_w_out[j], **kw)
        elif kind == 2:
            xa = attn_layer(xa, norm_g[i, 0], rowmod, attn_w_qkv[j], attn_q_g[j], attn_k_g[j], attn_w_o[j],
                            grid_w=grid_w, **kw)
        else:
            xa = mlstm_layer(xa, norm_g[i, 0], rowmod, mlstm_w_up[j], mlstm_conv_w[j], mlstm_conv_b[j], mlstm_w_q[j],
                             mlstm_w_k[j], mlstm_w_v[j], mlstm_w_gate[j], mlstm_b_gate[j], mlstm_skip[j],
                             mlstm_norm_g[j], mlstm_w_down[j], **kw)
        if i % 2 == 0:
            xa = dense_ffn(xa, norm_g[i, 1], rowmod, *ffn_w, i // 2)
        else:
            last = i == depth - 1
            xa = moe_ffn(xa, norm_g[i, 1], rowmod, moe_router[i // 2], *moe_w, i // 2,
                         latent_only=(bsz, ctx_len) if last else None)
            if last:
                return xa.reshape(bsz, seq_len, D_MODEL)
    return xa.reshape(bsz, t_all, D_MODEL)[:, ctx_len:]
```

```python
import functools
import math

import jax
import jax.numpy as jnp
import numpy as np
from jax import lax
from jax.experimental import pallas as pl
from jax.experimental.pallas import tpu as pltpu

F32 = jnp.float32
BF16 = jnp.bfloat16
HI = lax.Precision.HIGHEST

D_MODEL = 1024
EPS = 1e-6
ROW_GROUP = 256
PROJ_TM = 512
CONV_W = 5
NEG_BIG = -1e30
LOG2E = math.log2(math.e)
VMEM_LIMIT = 56 << 20

SSD_INNER = 2 * D_MODEL
SSD_P = 64
SSD_HEADS = SSD_INNER // SSD_P
SSD_N = 128
SSD_GROUPS = 8
SSD_GW = SSD_INNER // SSD_GROUPS
SCAN_L = 128

HG_HEADS = 8
HG_DK = 128
HG_SUB = 8

ATT_HEADS = 16
ATT_KV = 4
ATT_HD = 64
ROPE_THETA = 10000.0
ROPE_FREQS = ATT_HD // 4
ATT_TQ = 128
ATT_ONES = 16
ATT_SHIFT_MAX = 60.0

ML_INNER = 2 * D_MODEL
ML_HEADS = 4
ML_DH = ML_INNER // ML_HEADS
ML_L = 256

D_FF = 7 * D_MODEL // 2
N_EXPERTS = 8
MOE_TM = 1024


def _cparams(sem):
    return pltpu.CompilerParams(dimension_semantics=sem, vmem_limit_bytes=VMEM_LIMIT)


def _dot(a, b):
    return jnp.dot(a, b, preferred_element_type=F32)


def _dot_hi(a, b):
    return jnp.dot(a, b, preferred_element_type=F32, precision=HI)


def _dot_nt(a, b):
    return lax.dot_general(a, b, (((1,), (1,)), ((), ())), preferred_element_type=F32)


def _dot_tn(a, b):
    return lax.dot_general(a, b, (((0,), (0,)), ((), ())), preferred_element_type=F32)


def _silu(x):
    return x * jax.nn.sigmoid(x)


def _log_sigmoid(x):
    return jnp.minimum(x, 0.0) - jnp.log1p(jnp.exp(-jnp.abs(x)))


def _softplus(x):
    return jnp.maximum(x, 0.0) + jnp.log1p(jnp.exp(-jnp.abs(x)))


def _pick_tile(m, pref):
    t = pref
    while m % t:
        t //= 2
    return t


def _rows_scale_shift(y, mod_ref, scale_idx, shift_idx):
    parts = []
    for gi in range(y.shape[0] // ROW_GROUP):
        sl = y[gi * ROW_GROUP:(gi + 1) * ROW_GROUP]
        parts.append(sl * (1.0 + mod_ref[gi, scale_idx:scale_idx + 1, :]) + mod_ref[gi, shift_idx:shift_idx + 1, :])
    return parts[0] if len(parts) == 1 else jnp.concatenate(parts, axis=0)


def _rows_gate_residual(x, acc, mod_ref, gate_idx):
    parts = []
    for gi in range(x.shape[0] // ROW_GROUP):
        sl = slice(gi * ROW_GROUP, (gi + 1) * ROW_GROUP)
        parts.append(x[sl] + mod_ref[gi, gate_idx:gate_idx + 1, :] * acc[sl])
    return parts[0] if len(parts) == 1 else jnp.concatenate(parts, axis=0)


def _norm_mod(x, g, mod_ref, shift_idx, scale_idx):
    y = x * lax.rsqrt(jnp.mean(x * x, axis=-1, keepdims=True) + EPS) * g
    return _rows_scale_shift(y, mod_ref, scale_idx, shift_idx)


def _split2(a):
    hi = a.astype(BF16)
    return hi, (a - hi.astype(F32)).astype(BF16)


def _dot_sel_l(sel, a):
    hi, lo = _split2(a)
    return _dot(sel, hi) + _dot(sel, lo)


def _dot_sel_r(a, sel):
    hi, lo = _split2(a)
    return _dot(hi, sel) + _dot(lo, sel)


def _scan_masks(n, bwd):
    ii = lax.broadcasted_iota(jnp.int32, (n, n), 0)
    jj = lax.broadcasted_iota(jnp.int32, (n, n), 1)
    visible = (ii <= jj) if bwd else (ii >= jj)
    visible_t = (ii >= jj) if bwd else (ii <= jj)
    tri = jnp.where(visible, 1.0, 0.0).astype(BF16)
    trit = jnp.where(visible_t, 1.0, 0.0).astype(BF16)
    return ii, jj, visible, tri, trit


def _ada_kernel(c_ref, w_ref, b_ref, o_ref):
    c = c_ref[...]
    o_ref[...] = _dot_hi(_silu(c), w_ref[...]) + b_ref[...]


def ada_modulation(c_pad, w, b, layer):
    n = w.shape[2]
    tn = 1024
    return pl.pallas_call(
        _ada_kernel,
        grid=(n // tn,),
        in_specs=[pl.BlockSpec(c_pad.shape, lambda j: (0, 0)),
                  pl.BlockSpec((None, D_MODEL, tn), lambda j: (layer, 0, j)),
                  pl.BlockSpec((None, 1, tn), lambda j: (layer, 0, j))],
        out_specs=pl.BlockSpec((c_pad.shape[0], tn), lambda j: (0, j)),
        out_shape=jax.ShapeDtypeStruct((c_pad.shape[0], n), F32),
        compiler_params=_cparams(("arbitrary",)),
        name="ada_modulation",
    )(c_pad, w, b.reshape(b.shape[0], 1, n))


def _inproj_kernel(x_ref, g_ref, mod_ref, *refs, n_w, shift_idx, scale_idx, tn):
    h = _norm_mod(x_ref[...], g_ref[...], mod_ref, shift_idx, scale_idx).astype(BF16)
    for w_ref, o_ref in zip(refs[:n_w], refs[n_w:]):
        n = o_ref.shape[1]
        step = min(tn, n)
        for j in range(n // step):
            o_ref[:, j * step:(j + 1) * step] = _dot(h, w_ref[:, j * step:(j + 1) * step]).astype(o_ref.dtype)


def in_projection(x, g, rowmod, ws, out_dtypes, *, shift_idx, scale_idx, tm=PROJ_TM, tn=512):
    m = x.shape[0]
    tm = _pick_tile(m, tm)
    gm = tm // ROW_GROUP
    in_specs = [pl.BlockSpec((tm, D_MODEL), lambda i: (i, 0)),
                pl.BlockSpec((1, D_MODEL), lambda i: (0, 0)),
                pl.BlockSpec((gm, 6, D_MODEL), lambda i: (i, 0, 0))]
    in_specs += [pl.BlockSpec(w.shape, lambda i: (0, 0)) for w in ws]
    out_specs = [pl.BlockSpec((tm, w.shape[1]), lambda i: (i, 0)) for w in ws]
    out_shape = [jax.ShapeDtypeStruct((m, w.shape[1]), dt) for w, dt in zip(ws, out_dtypes)]
    return pl.pallas_call(
        functools.partial(_inproj_kernel, n_w=len(ws), shift_idx=shift_idx, scale_idx=scale_idx, tn=tn),
        grid=(m // tm,),
        in_specs=in_specs, out_specs=out_specs, out_shape=out_shape,
        compiler_params=_cparams(("parallel",)),
        name="in_projection",
    )(x, g.reshape(1, D_MODEL), rowmod, *ws)


def _outproj_kernel(*refs, n_pro, pro_fn, gate_idx):
    pro_refs = refs[:n_pro]
    w_ref, x_ref, mod_ref, o_ref = refs[n_pro:]
    a = pro_fn(*pro_refs).astype(BF16)
    acc = _dot(a, w_ref[...])
    o_ref[...] = _rows_gate_residual(x_ref[...], acc, mod_ref, gate_idx)


def out_projection(pro_fn, pro_args, pro_specs, w, x, rowmod, *, gate_idx, tm):
    m = x.shape[0]
    gm = tm // ROW_GROUP
    in_specs = list(pro_specs) + [pl.BlockSpec(w.shape, lambda i: (0, 0)),
                                  pl.BlockSpec((tm, D_MODEL), lambda i: (i, 0)),
                                  pl.BlockSpec((gm, 6, D_MODEL), lambda i: (i, 0, 0))]
    return pl.pallas_call(
        functools.partial(_outproj_kernel, n_pro=len(pro_args), pro_fn=pro_fn, gate_idx=gate_idx),
        grid=(m // tm,),
        in_specs=in_specs,
        out_specs=pl.BlockSpec((tm, D_MODEL), lambda i: (i, 0)),
        out_shape=jax.ShapeDtypeStruct((m, D_MODEL), F32),
        compiler_params=_cparams(("parallel",)),
        name="out_projection",
    )(*pro_args, w, x, rowmod)


def _conv_kernel(u_ref, w_ref, b_ref, o_ref, *, ctx_len):
    x = u_ref[...].astype(F32)
    t_all = x.shape[0]
    t = lax.broadcasted_iota(jnp.int32, (t_all, 1), 0)
    in_ctx = t < ctx_len
    pad = CONV_W // 2
    acc = b_ref[...] + w_ref[pad:pad + 1, :] * x
    for off in range(-pad, pad + 1):
        if off == 0:
            continue
        xs = pltpu.roll(x, (-off) % t_all, 0)
        tt = t + off
        valid = (tt >= 0) & (tt < t_all) & ((tt < ctx_len) == in_ctx)
        acc = acc + w_ref[pad + off:pad + off + 1, :] * jnp.where(valid, xs, 0.0)
    o_ref[...] = _silu(acc).astype(o_ref.dtype)


def dwconv_silu(u, col_off, width, w, b, *, bsz, ctx_len, tc=256):
    m = u.shape[0]
    t_all = m // bsz
    cb = col_off // tc
    return pl.pallas_call(
        functools.partial(_conv_kernel, ctx_len=ctx_len),
        grid=(bsz, width // tc),
        in_specs=[pl.BlockSpec((t_all, tc), lambda bi, j: (bi, j + cb)),
                  pl.BlockSpec((CONV_W, tc), lambda bi, j: (0, j)),
                  pl.BlockSpec((1, tc), lambda bi, j: (0, j))],
        out_specs=pl.BlockSpec((t_all, tc), lambda bi, j: (bi, j)),
        out_shape=jax.ShapeDtypeStruct((m, width), BF16),
        compiler_params=_cparams(("parallel", "parallel")),
        name="dwconv_silu",
    )(u, w, b.reshape(1, width))


def _scan_chunk_index(bwd, c, n_ctx_chunks, n_chunks):
    if not bwd:
        return c
    return jnp.where(c < n_ctx_chunks, n_ctx_chunks - 1 - c, n_chunks - 1 + n_ctx_chunks - c)


def _ssd_scan_kernel(x_ref, b_ref, c_ref, dt_ref, dtt_ref, bias_ref, biast_ref, alog_ref, alogt_ref, exp_ref,
                     y_ref, state, *, bwd):
    ci = pl.program_id(1)
    L = SCAN_L
    H = SSD_HEADS

    @pl.when(ci == 0)
    def _():
        state[...] = jnp.zeros_like(state)

    _, _, visible, tri, trit = _scan_masks(L, bwd)
    dt = _softplus(dt_ref[...] + bias_ref[...])
    dtt = _softplus(dtt_ref[...] + biast_ref[...])
    a_neg = -jnp.exp(alog_ref[...])
    a_negt = -jnp.exp(alogt_ref[...])
    dta = dt * a_neg
    a_col = _dot_sel_l(tri, dta)
    a_row = _dot_sel_r(dtt * a_negt, trit)
    a_end = jnp.sum(dta, axis=0, keepdims=True)
    per_head = jnp.concatenate([jnp.exp(a_col), jnp.exp(a_end - a_col) * dt,
                                jnp.broadcast_to(jnp.exp(a_end), (16, H))], axis=0)
    per_chan = _dot_sel_r(per_head, exp_ref[...])
    ea_x, w_x, eend_x = per_chan[:L], per_chan[L:2 * L], per_chan[2 * L:2 * L + 1]
    lane = lax.broadcasted_iota(jnp.int32, (L, 2 * SSD_P), 1)
    for g in range(SSD_GROUPS):
        gs = slice(g * SSD_GW, (g + 1) * SSD_GW)
        ns = slice(g * SSD_N, (g + 1) * SSD_N)
        bg = b_ref[:, ns]
        cg = c_ref[:, ns]
        xgb = x_ref[:, gs]
        cb = _dot_nt(cg, bg)
        sg = state[:, gs]
        y_inter = _dot(cg, sg.astype(BF16)) * ea_x[:, gs]
        pieces = []
        for pr in range(SSD_GW // (2 * SSD_P)):
            ys = []
            for q in range(2):
                h = g * (SSD_GW // SSD_P) + 2 * pr + q
                rel = a_col[:, h:h + 1] - a_row[h:h + 1, :]
                gm = cb * jnp.exp(jnp.where(visible, rel, NEG_BIG)) * dtt[h:h + 1, :]
                ys.append(_dot(gm.astype(BF16), xgb[:, pr * 2 * SSD_P:(pr + 1) * 2 * SSD_P]))
            pieces.append(jnp.where(lane < SSD_P, ys[0], ys[1]))
        y = jnp.concatenate(pieces, axis=1) + y_inter
        y_ref[:, gs] = y.astype(y_ref.dtype)
        xw = (xgb.astype(F32) * w_x[:, gs]).astype(BF16)
        state[:, gs] = sg * eend_x[:, gs] + _dot_tn(bg, xw)


def ssd_scan(u, dt_dir, dtt_dir, dt_bias, a_log, *, bsz, ctx_len, bwd):
    m = u.shape[0]
    t_all = m // bsz
    nc = t_all // SCAN_L
    ncc = ctx_len // SCAN_L
    L, H = SCAN_L, SSD_HEADS
    d = int(bwd)

    def row(bi, c):
        return bi * nc + _scan_chunk_index(bwd, c, ncc, nc)

    bn = SSD_INNER // (SSD_GROUPS * SSD_N)
    expand = jnp.asarray(np.kron(np.eye(H, dtype=np.float32), np.ones((1, SSD_P), np.float32)), BF16)
    return pl.pallas_call(
        functools.partial(_ssd_scan_kernel, bwd=bwd),
        grid=(bsz, nc),
        in_specs=[pl.BlockSpec((L, SSD_INNER), lambda bi, c: (row(bi, c), 0)),
                  pl.BlockSpec((L, SSD_GROUPS * SSD_N), lambda bi, c: (row(bi, c), bn)),
                  pl.BlockSpec((L, SSD_GROUPS * SSD_N), lambda bi, c: (row(bi, c), bn + 1)),
                  pl.BlockSpec((None, L, H), lambda bi, c: (d, row(bi, c), 0)),
                  pl.BlockSpec((None, H, L), lambda bi, c: (d, 0, row(bi, c))),
                  pl.BlockSpec((None, 1, H), lambda bi, c: (d, 0, 0)),
                  pl.BlockSpec((None, H, 1), lambda bi, c: (d, 0, 0)),
                  pl.BlockSpec((None, 1, H), lambda bi, c: (d, 0, 0)),
                  pl.BlockSpec((None, H, 1), lambda bi, c: (d, 0, 0)),
                  pl.BlockSpec((H, SSD_INNER), lambda bi, c: (0, 0))],
        out_specs=pl.BlockSpec((L, SSD_INNER), lambda bi, c: (row(bi, c), 0)),
        out_shape=jax.ShapeDtypeStruct((m, SSD_INNER), BF16),
        scratch_shapes=[pltpu.VMEM((SSD_N, SSD_INNER), F32)],
        compiler_params=_cparams(("arbitrary", "arbitrary")),
        name="ssd_scan_bwd" if bwd else "ssd_scan_fwd",
    )(u, u, u, dt_dir, dtt_dir, dt_bias.reshape(2, 1, H), dt_bias.reshape(2, H, 1),
      a_log.reshape(2, 1, H), a_log.reshape(2, H, 1), expand)


def _ssd_finish(yf_ref, yb_ref, xs_ref, z_ref, d_ref, g_ref):
    y = yf_ref[...].astype(F32) + yb_ref[...].astype(F32) + d_ref[...] * xs_ref[...].astype(F32)
    y = y * _silu(z_ref[...].astype(F32))
    parts = []
    for g in range(SSD_GROUPS):
        sl = y[:, g * SSD_GW:(g + 1) * SSD_GW]
        parts.append(sl * lax.rsqrt(jnp.mean(sl * sl, axis=-1, keepdims=True) + EPS))
    return jnp.concatenate(parts, axis=1) * g_ref[...]


def _gla_direction(q_ref, f_ref, v_ref, llb_ref, l1m_ref, omlb_ref, place_ref, o_ref, state, *, bwd):
    L, C = SCAN_L, HG_SUB
    nb = L // C
    dk = HG_DK
    ii, jj, visible, tri, _ = _scan_masks(L, bwd)
    C2 = 2 * C
    nb2 = L // C2
    band_mask = ((ii // C) == (jj // C)) & visible
    mid_mask = ((ii // C2) == (jj // C2)) & visible
    off_mask = ((ii // C2) < (jj // C2)) if bwd else ((ii // C2) > (jj // C2))
    last = 0 if bwd else L - 1
    blocks2 = range(1, nb2) if bwd else range(nb2 - 1)

    def head(h, carry):
        ls = slice(h * dk, (h + 1) * dk)
        q = _silu(q_ref[:, ls].astype(F32)) * (dk ** -0.5)
        f = f_ref[:, ls].astype(F32)
        v = v_ref[:, ls]
        a = llb_ref[:, ls]
        b = l1m_ref[:, ls] + _log_sigmoid(f)
        lf = (jnp.maximum(a, b) + jnp.log1p(jnp.exp(-jnp.abs(a - b)))) * LOG2E
        k = omlb_ref[:, ls] * jax.nn.sigmoid(-f)
        fd = jnp.exp2(lf)
        cum = _dot_sel_l(tri, lf)
        c_end = cum[last:last + 1, :]
        kd = k
        es = [(q * kd).astype(BF16)]
        for dl in range(1, C):
            kd = fd * pltpu.roll(kd, (L - 1) if bwd else 1, 0)
            es.append((q * kd).astype(BF16))
        band = _dot(jnp.concatenate(es, axis=1), place_ref[...])
        band = pltpu.roll(band, 0, 1, stride=1, stride_axis=0)
        def end_rows(c):
            rows = [(j * c if bwd else (j + 1) * c - 1) for j in range(L // c)]
            return [cum[r:r + 1, :] for r in rows]

        e1, e2 = end_rows(C), end_rows(C2)
        kt1 = [k[j * C:(j + 1) * C] * jnp.exp2(e1[j] - cum[j * C:(j + 1) * C]) for j in range(nb)]
        zero1 = jnp.zeros((C, dk), F32)
        qmid, kt2 = [], []
        for m2 in range(nb2):
            early, late = (2 * m2 + 1, 2 * m2) if bwd else (2 * m2, 2 * m2 + 1)
            rs = slice(late * C, (late + 1) * C)
            piece = q[rs] * jnp.exp2(cum[rs] - e1[early])
            k_early = kt1[early] * jnp.exp2(e2[m2] - e1[early])
            qmid += [piece, zero1] if bwd else [zero1, piece]
            kt2 += [kt1[late], k_early] if bwd else [k_early, kt1[late]]
        mid = _dot_nt(jnp.concatenate(qmid, axis=0).astype(BF16), jnp.concatenate(kt1, axis=0).astype(BF16))
        kt2b = jnp.concatenate(kt2, axis=0).astype(BF16)
        zero2 = jnp.zeros((C2, dk), F32)
        kcols, qcols = [], []
        pieces = [None] * nb2
        for j in (blocks2 if bwd else reversed(blocks2)):
            parts = [jnp.zeros((j * C2, dk), BF16), kt2b[j * C2:(j + 1) * C2], jnp.zeros((L - (j + 1) * C2, dk), BF16)]
            kcols.append(jnp.concatenate([x for x in parts if x.shape[0]], axis=0))
            near = j - 1 if bwd else j + 1
            if near in blocks2:
                step = jnp.exp2(e2[near] - e2[j])
                for i in (range(0, near) if bwd else range(near + 1, nb2)):
                    pieces[i] = pieces[i] * step
            rs = slice(near * C2, (near + 1) * C2)
            pieces[near] = q[rs] * jnp.exp2(cum[rs] - e2[j])
            qcols.append(jnp.concatenate([zero2 if p is None else p for p in pieces], axis=0).astype(BF16))
        off = _dot_nt(jnp.concatenate(qcols, axis=1), jnp.concatenate(kcols, axis=1))
        att = jnp.where(band_mask, band, jnp.where(mid_mask, mid, jnp.where(off_mask, off, 0.0)))
        st = state[h]
        o = _dot(att.astype(BF16), v) + _dot_nt((q * jnp.exp2(cum)).astype(BF16), st.astype(BF16))
        o_ref[:, ls] = o.astype(o_ref.dtype)
        kw = (k * jnp.exp2(c_end - cum)).astype(BF16)
        state[h] = st * jnp.exp2(c_end) + _dot_tn(v, kw)
        return carry

    return head


def _gla_scan_kernel(qf_ref, ff_ref, vf_ref, qb_ref, fb_ref, vb_ref, llb_ref, l1m_ref, omlb_ref, pf_ref, pb_ref,
                     of_ref, ob_ref, state):
    @pl.when(pl.program_id(1) == 0)
    def _():
        state[...] = jnp.zeros_like(state)

    fwd = _gla_direction(qf_ref, ff_ref, vf_ref, llb_ref.at[0], l1m_ref.at[0], omlb_ref.at[0], pf_ref, of_ref,
                         state.at[0], bwd=False)
    bwd = _gla_direction(qb_ref, fb_ref, vb_ref, llb_ref.at[1], l1m_ref.at[1], omlb_ref.at[1], pb_ref, ob_ref,
                         state.at[1], bwd=True)
    for h in range(HG_HEADS):
        fwd(h, 0)
        bwd(h, 0)


def _gla_place_matrix(bwd):
    pm = np.zeros((HG_SUB * HG_DK, SCAN_L), np.float32)
    for dl in range(HG_SUB):
        pm[dl * HG_DK:(dl + 1) * HG_DK, dl if bwd else (SCAN_L - dl) % SCAN_L] = 1.0
    return jnp.asarray(pm, BF16)


def gla_scan(proj, llb, l1m, omlb, *, bsz, ctx_len):
    m = proj.shape[0]
    t_all = m // bsz
    nc = t_all // SCAN_L
    ncc = ctx_len // SCAN_L
    L = SCAN_L

    def blk(bwd, col):
        return pl.BlockSpec((L, D_MODEL), lambda bi, c: (bi * nc + _scan_chunk_index(bwd, c, ncc, nc), col))

    vec = pl.BlockSpec((2, 1, D_MODEL), lambda bi, c: (0, 0, 0))
    pf, pb = _gla_place_matrix(False), _gla_place_matrix(True)
    const = pl.BlockSpec(pf.shape, lambda bi, c: (0, 0))
    return pl.pallas_call(
        _gla_scan_kernel,
        grid=(bsz, nc),
        in_specs=[blk(False, 0), blk(False, 1), blk(False, 3), blk(True, 0), blk(True, 2), blk(True, 3),
                  vec, vec, vec, const, const],
        out_specs=[blk(False, 0), blk(True, 0)],
        out_shape=[jax.ShapeDtypeStruct((m, D_MODEL), BF16)] * 2,
        scratch_shapes=[pltpu.VMEM((2, HG_HEADS, HG_DK, HG_DK), F32)],
        compiler_params=_cparams(("arbitrary", "arbitrary")),
        name="gla_scan",
    )(proj, proj, proj, proj, proj, proj, llb, l1m, omlb, pf, pb)


def _hgrn_finish(of_ref, ob_ref, gate_ref, g_ref):
    o = of_ref[...].astype(F32) + ob_ref[...].astype(F32)
    parts = []
    for h in range(HG_HEADS):
        sl = o[:, h * HG_DK:(h + 1) * HG_DK]
        parts.append(sl * lax.rsqrt(jnp.mean(sl * sl, axis=-1, keepdims=True) + EPS))
    return jnp.concatenate(parts, axis=1) * g_ref[...] * _silu(gate_ref[...].astype(F32))


def _attn_qkv_kernel(x_ref, g_ref, mod_ref, w_ref, cos_ref, sin_ref, qg_ref, kg_ref, bd_ref, rot_ref,
                     q_ref, k_ref, v_ref, *, shift_idx, scale_idx):
    h = _norm_mod(x_ref[...], g_ref[...], mod_ref, shift_idx, scale_idx).astype(BF16)
    nq = ATT_HEADS * ATT_HD
    nk = ATT_KV * ATT_HD

    def norm_rope(a, gain, width, out_scale):
        ss = _dot((a * a).astype(BF16), bd_ref[:width, :width])
        an = a * lax.rsqrt(ss * (1.0 / ATT_HD) + EPS) * gain
        rot = _dot(an.astype(BF16), rot_ref[:width, :width])
        cos = jnp.concatenate([cos_ref[...]] * (width // 128), axis=1)
        sin = jnp.concatenate([sin_ref[...]] * (width // 128), axis=1)
        return (an * cos + rot * sin) * out_scale

    q = _dot(h, w_ref[:, :nq])
    q_ref[...] = norm_rope(q, qg_ref[...], nq, ATT_HD ** -0.5 * math.log2(math.e)).astype(q_ref.dtype)
    k = _dot(h, w_ref[:, nq:nq + nk])
    k_ref[...] = norm_rope(k, kg_ref[...], nk, 1.0).astype(k_ref.dtype)
    v_ref[...] = _dot(h, w_ref[:, nq + nk:]).astype(v_ref.dtype)


def attn_qkv(x, g, rowmod, w, cos_t, sin_t, q_g, k_g, bd, rot, *, bsz, shift_idx, scale_idx):
    m = x.shape[0]
    tm = ROW_GROUP
    tpb = (m // bsz) // tm
    nq = ATT_HEADS * ATT_HD
    nk = ATT_KV * ATT_HD
    const = lambda shape: pl.BlockSpec(shape, lambda i: (0, 0))
    return pl.pallas_call(
        functools.partial(_attn_qkv_kernel, shift_idx=shift_idx, scale_idx=scale_idx),
        grid=(m // tm,),
        in_specs=[pl.BlockSpec((tm, D_MODEL), lambda i: (i, 0)),
                  const((1, D_MODEL)),
                  pl.BlockSpec((1, 6, D_MODEL), lambda i: (i, 0, 0)),
                  const(w.shape),
                  pl.BlockSpec((tm, 128), lambda i: (i % tpb, 0)),
                  pl.BlockSpec((tm, 128), lambda i: (i % tpb, 0)),
                  const((1, nq)), const((1, nk)), const(bd.shape), const(rot.shape)],
        out_specs=[pl.BlockSpec((tm, nq), lambda i: (i, 0)),
                   pl.BlockSpec((tm, nk), lambda i: (i, 0)),
                   pl.BlockSpec((tm, nk), lambda i: (i, 0))],
        out_shape=[jax.ShapeDtypeStruct((m, nq), BF16),
                   jax.ShapeDtypeStruct((m, nk), BF16),
                   jax.ShapeDtypeStruct((m, nk), BF16)],
        compiler_params=_cparams(("parallel",)),
        name="attn_qkv",
    )(x, g.reshape(1, D_MODEL), rowmod, w, cos_t, sin_t, q_g, k_g, bd, rot)


def _attn_kernel(q_ref, k_ref, v_ref, o_ref, kaug, vtaug, kmax2, *, ctx_len):
    qt = pl.program_id(2)
    tq = q_ref.shape[0]
    hd = ATT_HD
    lane = lax.broadcasted_iota(jnp.int32, (1, 2 * hd), 1)
    n_r = q_ref.shape[1] // (2 * hd)
    aug_lane = (hd, 0)
    ii = lax.broadcasted_iota(jnp.int32, (2 * hd, 2 * hd), 0)
    jj = lax.broadcasted_iota(jnp.int32, (2 * hd, 2 * hd), 1)
    half_sum = jnp.where((ii // hd) == (jj // hd), 1.0, 0.0).astype(BF16)

    @pl.when(qt == 0)
    def _():
        k = k_ref[...]
        one = jnp.ones((), BF16)
        vt = v_ref[...].astype(F32).T.astype(BF16)
        ones_rows = jnp.ones((ATT_ONES, vt.shape[1]), BF16)
        for j in range(2):
            vtaug[j, 0:hd, :] = vt[j * hd:(j + 1) * hd, :]
            vtaug[j, hd:hd + ATT_ONES, :] = ones_rows
            kaug[j] = jnp.where(lane == aug_lane[j], one, k)
        kf = k.astype(F32)
        kn2 = _dot((kf * kf).astype(BF16), half_sum)
        kmax2[...] = jnp.broadcast_to(jnp.max(kn2, axis=0, keepdims=True), kmax2.shape)

    def queries(j):
        sel = (lane // hd) == j
        return jnp.concatenate(
            [jnp.where(sel, q_ref[:, r * 128:(r + 1) * 128], jnp.zeros((), BF16)) for r in range(n_r)], axis=0)

    def values(j, pt, n_keys):
        ot = _dot(vtaug[j, :, :n_keys], pt)
        return ot[0:hd] / ot[hd:hd + 1]

    def finish(tops):
        o = jnp.concatenate(tops, axis=0).T
        for r in range(n_r):
            o_ref[:, r * 128:(r + 1) * 128] = o[r * tq:(r + 1) * tq].astype(o_ref.dtype)

    def attend_exact(n_keys):
        tops = []
        for j in range(2):
            st = _dot_nt(k_ref[:n_keys, :], queries(j))
            mx = jnp.max(st, axis=0, keepdims=True)
            tops.append(values(j, jnp.exp2(st - mx).astype(BF16), n_keys))
        finish(tops)

    def attend_shifted(n_keys, shifts):
        tops = []
        for j in range(2):
            qa = jnp.where(lane == aug_lane[j], (-shifts[j]).astype(BF16), queries(j))
            pt = jnp.exp2(_dot_nt(kaug[j, :n_keys, :], qa)).astype(BF16)
            tops.append(values(j, pt, n_keys))
        finish(tops)

    def attend(n_keys):
        shifts = []
        ones = jnp.ones((2 * hd, 2 * hd), BF16)
        for j in range(2):
            qf = queries(j).astype(F32)
            qn2 = _dot((qf * qf).astype(BF16), ones)
            shifts.append(jnp.sqrt(qn2 * kmax2[0:1, j * hd:j * hd + 1]) * 1.02)
        worst = jnp.max(jnp.maximum(shifts[0], shifts[1]))
        small = worst < ATT_SHIFT_MAX

        @pl.when(small)
        def _():
            attend_shifted(n_keys, shifts)

        @pl.when(jnp.logical_not(small))
        def _():
            attend_exact(n_keys)

    n_ctx_tiles = ctx_len // tq

    @pl.when(qt < n_ctx_tiles)
    def _():
        attend(ctx_len)

    @pl.when(qt >= n_ctx_tiles)
    def _():
        attend(k_ref.shape[0])


def attention(q, k, v, *, bsz, ctx_len):
    m = q.shape[0]
    t_all = m // bsz
    tq = ATT_TQ
    nqt = t_all // tq
    n_pairs = ATT_KV // 2
    qw = q.shape[1] // n_pairs
    return pl.pallas_call(
        functools.partial(_attn_kernel, ctx_len=ctx_len),
        grid=(bsz, n_pairs, nqt),
        in_specs=[pl.BlockSpec((tq, qw), lambda bi, p, t: (bi * nqt + t, p)),
                  pl.BlockSpec((t_all, 2 * ATT_HD), lambda bi, p, t: (bi, p)),
                  pl.BlockSpec((t_all, 2 * ATT_HD), lambda bi, p, t: (bi, p))],
        out_specs=pl.BlockSpec((tq, qw), lambda bi, p, t: (bi * nqt + t, p)),
        out_shape=jax.ShapeDtypeStruct(q.shape, BF16),
        scratch_shapes=[pltpu.VMEM((2, t_all, 2 * ATT_HD), BF16),
                        pltpu.VMEM((2, ATT_HD + ATT_ONES, t_all), BF16),
                        pltpu.VMEM((8, 2 * ATT_HD), F32)],
        compiler_params=_cparams(("arbitrary", "arbitrary", "arbitrary")),
        name="attention",
    )(q, k, v)


def _identity_pro(a_ref):
    return a_ref[...]


def _mlstm_qkv_kernel(xc_ref, xm_ref, wq_ref, wk_ref, wv_ref, wg_ref, bg_ref, q_ref, k_ref, v_ref, gate_ref):
    gates = jnp.zeros(gate_ref.shape, F32) + bg_ref[...]
    for which, (src, w_ref, o_ref) in enumerate(((xc_ref, wq_ref, q_ref), (xc_ref, wk_ref, k_ref),
                                                 (xm_ref, wv_ref, v_ref))):
        for h in range(ML_HEADS):
            hs = slice(h * ML_DH, (h + 1) * ML_DH)
            r = _dot(src[:, hs], w_ref[h]).astype(BF16)
            o_ref[:, hs] = r
            gates = gates + _dot(r, wg_ref[which * ML_INNER + h * ML_DH:which * ML_INNER + (h + 1) * ML_DH, :])
    gate_ref[...] = gates


def mlstm_qkv(xc, up, wq, wk, wv, wg, bg, *, tm=PROJ_TM):
    m = xc.shape[0]
    ng = wg.shape[1]
    const = lambda shape: pl.BlockSpec(shape, lambda i: (0,) * len(shape))
    row = pl.BlockSpec((tm, ML_INNER), lambda i: (i, 0))
    return pl.pallas_call(
        _mlstm_qkv_kernel,
        grid=(m // tm,),
        in_specs=[row, row, const(wq.shape), const(wk.shape), const(wv.shape), const(wg.shape), const((1, ng))],
        out_specs=[row, row, row, pl.BlockSpec((tm, ng), lambda i: (i, 0))],
        out_shape=[jax.ShapeDtypeStruct((m, ML_INNER), BF16)] * 3 + [jax.ShapeDtypeStruct((m, ng), F32)],
        compiler_params=_cparams(("parallel",)),
        name="mlstm_qkv",
    )(xc, up, wq, wk, wv, wg, bg.reshape(1, ng))


def _mlstm_scan_kernel(q_ref, k_ref, v_ref, gt_ref, gtt_ref, h_ref, c_st, n_st, m_st, *, bwd):
    ci = pl.program_id(1)
    L = ML_L
    nh = ML_HEADS
    scale = ML_DH ** -0.5

    @pl.when(ci == 0)
    def _():
        c_st[...] = jnp.zeros_like(c_st)
        n_st[...] = jnp.zeros_like(n_st)
        m_st[...] = jnp.full(m_st.shape, NEG_BIG, F32)

    _, _, visible, tri, trit = _scan_masks(L, bwd)
    gt = gt_ref[...]
    gtt = gtt_ref[...]
    li_c, li_r = gt[:, :nh], gtt[:nh, :]
    lf_c, lf_r = _log_sigmoid(gt[:, nh:]), _log_sigmoid(gtt[nh:, :])
    cum_c = _dot_sel_l(tri, lf_c)
    cum_r = _dot_sel_r(lf_r, trit)
    end_c = jnp.sum(lf_c, axis=0, keepdims=True)
    for h in range(nh):
        hs = slice(h * ML_DH, (h + 1) * ML_DH)
        q = q_ref[:, hs]
        kb = k_ref[:, hs]
        k = kb.astype(F32)
        v = v_ref[:, hs]
        m_prev = m_st[h:h + 1, 0:1]
        cum_end = end_c[:, h:h + 1]
        dmat = jnp.where(visible, cum_c[:, h:h + 1] - cum_r[h:h + 1, :] + li_r[h:h + 1, :], -jnp.inf)
        inter = cum_c[:, h:h + 1] + m_prev
        m_t = jnp.maximum(inter, jnp.max(dmat, axis=1, keepdims=True))
        w = jnp.exp(dmat - m_t)
        w_c = jnp.exp(inter - m_t)
        qk = _dot_nt(q, kb) * scale * w
        cmat = c_st[h]
        num = _dot(qk.astype(BF16), v) + w_c * _dot(q, cmat.astype(BF16))
        qn = jnp.sum(q.astype(F32) * n_st[h:h + 1, :], axis=1, keepdims=True)
        den = jnp.sum(qk, axis=1, keepdims=True) + w_c * qn
        hv = num / jnp.maximum(jnp.abs(den), jnp.exp(-m_t))
        h_ref[:, hs] = hv.astype(h_ref.dtype)
        wend_c = cum_end - cum_c[:, h:h + 1] + li_c[:, h:h + 1]
        wend_r = cum_end - cum_r[h:h + 1, :] + li_r[h:h + 1, :]
        m_new = jnp.maximum(cum_end + m_prev, jnp.max(wend_r, axis=1, keepdims=True))
        a_old = jnp.exp(cum_end + m_prev - m_new)
        e_c = jnp.exp(wend_c - m_new) * scale
        e_r = jnp.exp(wend_r - m_new) * scale
        c_st[h] = a_old * cmat + _dot_tn((k * e_c).astype(BF16), v)
        e_r8 = jnp.broadcast_to(e_r, (8, L)).astype(BF16)
        n_st[h:h + 1, :] = a_old * n_st[h:h + 1, :] + _dot(e_r8, kb)[0:1, :]
        m_st[h:h + 1, :] = jnp.broadcast_to(m_new, (1, m_st.shape[1]))


def mlstm_scan(q, k, v, gt_dir, gtt_dir, *, bsz, ctx_len, bwd):
    m = q.shape[0]
    t_all = m // bsz
    nc = t_all // ML_L
    ncc = ctx_len // ML_L
    L, nh = ML_L, ML_HEADS
    d = int(bwd)

    def row(bi, c):
        return bi * nc + _scan_chunk_index(bwd, c, ncc, nc)

    blk = pl.BlockSpec((L, ML_INNER), lambda bi, c: (row(bi, c), 0))
    return pl.pallas_call(
        functools.partial(_mlstm_scan_kernel, bwd=bwd),
        grid=(bsz, nc),
        in_specs=[blk, blk, blk,
                  pl.BlockSpec((None, L, 2 * nh), lambda bi, c: (d, row(bi, c), 0)),
                  pl.BlockSpec((None, 2 * nh, L), lambda bi, c: (d, 0, row(bi, c)))],
        out_specs=pl.BlockSpec((L, ML_INNER), lambda bi, c: (row(bi, c), 0)),
        out_shape=jax.ShapeDtypeStruct((m, ML_INNER), BF16),
        scratch_shapes=[pltpu.VMEM((nh, ML_DH, ML_DH), F32),
                        pltpu.VMEM((8, ML_DH), F32),
                        pltpu.VMEM((8, 128), F32)],
        compiler_params=_cparams(("arbitrary", "arbitrary")),
        name="mlstm_scan_bwd" if bwd else "mlstm_scan_fwd",
    )(q, k, v, gt_dir, gtt_dir)


def _mlstm_finish(hf_ref, hb_ref, xc_ref, z_ref, skip_ref, g_ref):
    hsum = hf_ref[...].astype(F32) + hb_ref[...].astype(F32)
    parts = []
    for h in range(ML_HEADS):
        sl = hsum[:, h * ML_DH:(h + 1) * ML_DH]
        parts.append(sl * lax.rsqrt(jnp.mean(sl * sl, axis=-1, keepdims=True) + EPS))
    hn = jnp.concatenate(parts, axis=1) * g_ref[...]
    return (hn + skip_ref[...] * xc_ref[...].astype(F32)) * _silu(z_ref[...].astype(F32))


def _ffn_kernel(x_ref, g_ref, mod_ref, w1_ref, w3_ref, w2_ref, o_ref, h_sc, acc_sc, *, shift_idx, scale_idx, gate_idx):
    f = pl.program_id(1)

    @pl.when(f == 0)
    def _():
        h_sc[...] = _norm_mod(x_ref[...], g_ref[...], mod_ref, shift_idx, scale_idx).astype(BF16)
        acc_sc[...] = jnp.zeros_like(acc_sc)

    h = h_sc[...]
    a = _silu(_dot(h, w1_ref[...])) * _dot(h, w3_ref[...])
    acc_sc[...] += _dot(a.astype(BF16), w2_ref[...])

    @pl.when(f == pl.num_programs(1) - 1)
    def _():
        o_ref[...] = _rows_gate_residual(x_ref[...], acc_sc[...], mod_ref, gate_idx)


def dense_ffn(x, g, rowmod, w1, w3, w2, layer, *, tm=1024, tf=512):
    m = x.shape[0]
    tm = _pick_tile(m, tm)
    gm = tm // ROW_GROUP
    nf = D_FF // tf
    return pl.pallas_call(
        functools.partial(_ffn_kernel, shift_idx=3, scale_idx=4, gate_idx=5),
        grid=(m // tm, nf),
        in_specs=[pl.BlockSpec((tm, D_MODEL), lambda i, f: (i, 0)),
                  pl.BlockSpec((1, D_MODEL), lambda i, f: (0, 0)),
                  pl.BlockSpec((gm, 6, D_MODEL), lambda i, f: (i, 0, 0)),
                  pl.BlockSpec((None, D_MODEL, tf), lambda i, f: (layer, 0, f)),
                  pl.BlockSpec((None, D_MODEL, tf), lambda i, f: (layer, 0, f)),
                  pl.BlockSpec((None, tf, D_MODEL), lambda i, f: (layer, f, 0))],
        out_specs=pl.BlockSpec((tm, D_MODEL), lambda i, f: (i, 0)),
        out_shape=jax.ShapeDtypeStruct((m, D_MODEL), F32),
        scratch_shapes=[pltpu.VMEM((tm, D_MODEL), BF16), pltpu.VMEM((tm, D_MODEL), F32)],
        compiler_params=_cparams(("parallel", "arbitrary")),
        name="dense_ffn",
    )(x, g.reshape(1, D_MODEL), rowmod, w1, w3, w2)


def _router_kernel(x_ref, g_ref, mod_ref, wr_ref, h_ref, route_ref, cnt_ref, cnt_sc, *, shift_idx, scale_idx, n_real):
    i = pl.program_id(0)

    @pl.when(i == 0)
    def _():
        cnt_sc[...] = jnp.zeros_like(cnt_sc)

    @pl.when(i < n_real)
    def _():
        h = _norm_mod(x_ref[...], g_ref[...], mod_ref, shift_idx, scale_idx)
        h_ref[...] = h.astype(h_ref.dtype)
        logits = _dot_hi(h, wr_ref[...])
        tm = logits.shape[0]
        lane = lax.broadcasted_iota(jnp.int32, logits.shape, 1)
        logits = jnp.where(lane < N_EXPERTS, logits, -jnp.inf)
        m1 = jnp.max(logits, axis=-1, keepdims=True)
        i1 = jnp.min(jnp.where(logits == m1, lane, 128), axis=-1, keepdims=True)
        rest = jnp.where(lane == i1, -jnp.inf, logits)
        m2 = jnp.max(rest, axis=-1, keepdims=True)
        i2 = jnp.min(jnp.where(rest == m2, lane, 128), axis=-1, keepdims=True)
        e2 = jnp.exp(m2 - m1)
        w1 = 1.0 / (1.0 + e2)
        w2 = e2 / (1.0 + e2)
        chosen = jnp.where(lane == i1, 1.0, jnp.where(lane == i2, 1.0, 0.0))
        ii = lax.broadcasted_iota(jnp.int32, (tm, tm), 0)
        jj = lax.broadcasted_iota(jnp.int32, (tm, tm), 1)
        strict = jnp.where(ii > jj, 1.0, 0.0).astype(BF16)
        prefix = _dot(strict, chosen.astype(BF16)) + cnt_sc[0:1, :]
        r1 = jnp.sum(jnp.where(lane == i1, prefix, 0.0), axis=-1, keepdims=True)
        r2 = jnp.sum(jnp.where(lane == i2, prefix, 0.0), axis=-1, keepdims=True)
        cnt_sc[...] = cnt_sc[...] + jnp.sum(chosen, axis=0, keepdims=True)
        route = jnp.zeros(logits.shape, F32)
        for col, val in enumerate((i1.astype(F32), i2.astype(F32), w1, w2, r1, r2)):
            route = jnp.where(lane == col, val, route)
        route_ref[...] = route

    @pl.when(i >= n_real)
    def _():
        h_ref[...] = jnp.zeros_like(h_ref)
        route_ref[...] = jnp.zeros_like(route_ref)

    cnt_ref[...] = cnt_sc[...]


def moe_router(x, g, rowmod, wr_pad, p_rows, *, tm=PROJ_TM):
    m = x.shape[0]
    n_real = m // tm
    gm = tm // ROW_GROUP
    clamp = lambda i: jnp.minimum(i, n_real - 1)
    return pl.pallas_call(
        functools.partial(_router_kernel, shift_idx=3, scale_idx=4, n_real=n_real),
        grid=(p_rows // tm,),
        in_specs=[pl.BlockSpec((tm, D_MODEL), lambda i: (clamp(i), 0)),
                  pl.BlockSpec((1, D_MODEL), lambda i: (0, 0)),
                  pl.BlockSpec((gm, 6, D_MODEL), lambda i: (clamp(i), 0, 0)),
                  pl.BlockSpec(wr_pad.shape, lambda i: (0, 0))],
        out_specs=[pl.BlockSpec((tm, D_MODEL), lambda i: (i, 0)), pl.BlockSpec((tm, 128), lambda i: (i, 0)),
                   pl.BlockSpec((8, 128), lambda i: (0, 0))],
        out_shape=[jax.ShapeDtypeStruct((p_rows, D_MODEL), BF16), jax.ShapeDtypeStruct((p_rows, 128), F32),
                   jax.ShapeDtypeStruct((8, 128), F32)],
        scratch_shapes=[pltpu.VMEM((8, 128), F32)],
        compiler_params=_cparams(("arbitrary",)),
        name="moe_router",
    )(x, g.reshape(1, D_MODEL), rowmod, wr_pad)


def _moe_ffn_kernel(te_ref, nt_ref, h_ref, w1_ref, w3_ref, w2_ref, o_ref, acc_sc):
    i = pl.program_id(0)
    f = pl.program_id(1)
    live = i < nt_ref[0]

    @pl.when(f == 0)
    def _():
        acc_sc[...] = jnp.zeros_like(acc_sc)

    @pl.when(live)
    def _():
        h = h_ref[...]
        a = _silu(_dot(h, w1_ref[...].astype(BF16))) * _dot(h, w3_ref[...].astype(BF16))
        acc_sc[...] += _dot(a.astype(BF16), w2_ref[...].astype(BF16))

    @pl.when(f == pl.num_programs(1) - 1)
    def _():
        o_ref[...] = acc_sc[...].astype(o_ref.dtype)


def moe_grouped_ffn(h_sorted, tile_expert, n_tiles, w1, w3, w2, layer, *, tm=MOE_TM, tf=512):
    p = h_sorted.shape[0]
    nf = D_FF // tf
    def fblk(i, f, nt):
        return jnp.where(i < nt[0], f, nf - 1)

    grid_spec = pltpu.PrefetchScalarGridSpec(
        num_scalar_prefetch=2,
        grid=(p // tm, nf),
        in_specs=[pl.BlockSpec((tm, D_MODEL), lambda i, f, te, nt: (jnp.minimum(i, nt[0] - 1), 0)),
                  pl.BlockSpec((None, None, D_MODEL, tf), lambda i, f, te, nt: (layer, te[i], 0, fblk(i, f, nt))),
                  pl.BlockSpec((None, None, D_MODEL, tf), lambda i, f, te, nt: (layer, te[i], 0, fblk(i, f, nt))),
                  pl.BlockSpec((None, None, tf, D_MODEL), lambda i, f, te, nt: (layer, te[i], fblk(i, f, nt), 0))],
        out_specs=pl.BlockSpec((tm, D_MODEL), lambda i, f, te, nt: (i, 0)),
        scratch_shapes=[pltpu.VMEM((tm, D_MODEL), F32)])
    return pl.pallas_call(
        _moe_ffn_kernel,
        grid_spec=grid_spec,
        out_shape=jax.ShapeDtypeStruct((p, D_MODEL), BF16),
        compiler_params=_cparams(("arbitrary", "arbitrary")),
        name="moe_grouped_ffn",
    )(tile_expert, n_tiles, h_sorted, w1, w3, w2)


def _moe_combine_kernel(x_ref, ya_ref, yb_ref, route_ref, mod_ref, o_ref, *, gate_idx):
    r = route_ref[...]
    y = r[:, 2:3] * ya_ref[...].astype(F32) + r[:, 3:4] * yb_ref[...].astype(F32)
    o_ref[...] = _rows_gate_residual(x_ref[...], y, mod_ref, gate_idx)


def moe_combine(x, ya, yb, route, rowmod, *, latent_only=None):
    m = x.shape[0]
    tm = ROW_GROUP
    if latent_only is None:
        n_out, src = m // tm, (lambda i: i)
    else:
        bsz, ctx_len = latent_only
        ctx_tiles = ctx_len // tm
        lat_tiles = m // bsz // tm - ctx_tiles
        n_out, src = bsz * lat_tiles, (lambda i: i + (i // lat_tiles + 1) * ctx_tiles)
    row = pl.BlockSpec((tm, D_MODEL), lambda i: (src(i), 0))
    return pl.pallas_call(
        functools.partial(_moe_combine_kernel, gate_idx=5),
        grid=(n_out,),
        in_specs=[row, row, row, pl.BlockSpec((tm, 128), lambda i: (src(i), 0)),
                  pl.BlockSpec((1, 6, D_MODEL), lambda i: (src(i), 0, 0))],
        out_specs=pl.BlockSpec((tm, D_MODEL), lambda i: (i, 0)),
        out_shape=jax.ShapeDtypeStruct((n_out * tm, D_MODEL), F32),
        compiler_params=_cparams(("parallel",)),
        name="moe_combine",
    )(x, ya, yb, route, rowmod)


def moe_ffn(x, g, rowmod, w_router, w1, w3, w2, layer, latent_only=None):
    m = x.shape[0]
    tm = MOE_TM
    n_tiles_max = (2 * m) // tm + N_EXPERTS
    p = n_tiles_max * tm
    wr_pad = jnp.zeros((D_MODEL, 128), F32).at[:, :N_EXPERTS].set(w_router)
    h, route, cnt = moe_router(x, g, rowmod, wr_pad, p)
    experts = jnp.arange(N_EXPERTS, dtype=jnp.int32)
    counts = cnt[0, :N_EXPERTS].astype(jnp.int32)
    tiles_per = (counts + tm - 1) // tm
    tile_end = jnp.cumsum(tiles_per)
    grp_start = (tile_end - tiles_per) * tm
    cnt_start = jnp.cumsum(counts) - counts
    e12 = route[:m, 0:2].astype(jnp.int32)
    r12 = route[:m, 4:6].astype(jnp.int32)
    pos12 = jnp.sum(jnp.where(e12[:, :, None] == experts, grp_start, 0), axis=-1) + r12
    order = jnp.argsort(e12.reshape(-1), stable=True).astype(jnp.int32)
    tile_expert = jnp.minimum(
        jnp.sum(jnp.arange(n_tiles_max, dtype=jnp.int32)[:, None] >= tile_end[None, :], axis=1), N_EXPERTS - 1
    ).astype(jnp.int32)
    n_tiles = tile_end[-1:].astype(jnp.int32)
    rank = jnp.arange(p, dtype=jnp.int32) - jnp.repeat(grp_start[tile_expert], tm)
    in_use = rank < jnp.repeat(counts[tile_expert], tm)
    sorted_idx = jnp.clip(jnp.repeat(cnt_start[tile_expert], tm) + rank, 0, 2 * m - 1)
    src_token = jnp.where(in_use, jnp.take(order, sorted_idx, mode="clip") // 2, jnp.arange(p, dtype=jnp.int32) % m)
    h_sorted = jnp.take(h, src_token, axis=0, mode="clip")
    y_sorted = moe_grouped_ffn(h_sorted, tile_expert, n_tiles, w1, w3, w2, layer)
    ya = jnp.take(y_sorted, pos12[:, 0], axis=0, mode="clip")
    yb = jnp.take(y_sorted, pos12[:, 1], axis=0, mode="clip")
    return moe_combine(x, ya, yb, route, rowmod, latent_only=latent_only)


def _dir_split(a, n):
    m = a.shape[0]
    a3 = a.reshape(m, 2, n)
    return jnp.transpose(a3, (1, 0, 2)), jnp.transpose(a3, (1, 2, 0))


def ssd_layer(x, g, rowmod, w_in, conv_w, conv_b, dt_bias, a_log, d_skip, norm_g, w_out, *, bsz, ctx_len):
    m = x.shape[0]
    n_main = SSD_INNER + SSD_INNER + 2 * SSD_GROUPS * SSD_N
    zx, dt_raw = in_projection(x, g, rowmod, [w_in[:, :n_main].astype(BF16), w_in[:, n_main:].astype(BF16)],
                               [BF16, F32], shift_idx=0, scale_idx=1)
    u = dwconv_silu(zx, SSD_INNER, n_main - SSD_INNER, conv_w, conv_b, bsz=bsz, ctx_len=ctx_len)
    dt_dir, dtt_dir = _dir_split(dt_raw, SSD_HEADS)
    yf = ssd_scan(u, dt_dir, dtt_dir, dt_bias, a_log, bsz=bsz, ctx_len=ctx_len, bwd=False)
    yb = ssd_scan(u, dt_dir, dtt_dir, dt_bias, a_log, bsz=bsz, ctx_len=ctx_len, bwd=True)
    tm = PROJ_TM
    d_x = jnp.repeat(d_skip.astype(F32), SSD_P).reshape(1, SSD_INNER)
    specs = [pl.BlockSpec((tm, SSD_INNER), lambda i: (i, 0)),
             pl.BlockSpec((tm, SSD_INNER), lambda i: (i, 0)),
             pl.BlockSpec((tm, SSD_INNER), lambda i: (i, 0)),
             pl.BlockSpec((tm, SSD_INNER), lambda i: (i, 0)),
             pl.BlockSpec((1, SSD_INNER), lambda i: (0, 0)),
             pl.BlockSpec((1, SSD_INNER), lambda i: (0, 0))]
    return out_projection(_ssd_finish, [yf, yb, u, zx, d_x, norm_g.reshape(1, SSD_INNER)], specs,
                          w_out.astype(BF16), x, rowmod, gate_idx=2, tm=tm)


def hgrn_layer(x, g, rowmod, w_in, lb, norm_g, w_out, *, bsz, ctx_len):
    proj, = in_projection(x, g, rowmod, [w_in.astype(BF16)], [BF16], shift_idx=0, scale_idx=1)
    lb = lb.astype(F32).reshape(2, 1, D_MODEL)
    lbs = (jnp.log(lb), jnp.log1p(-lb), 1.0 - lb)
    of, ob = gla_scan(proj, *lbs, bsz=bsz, ctx_len=ctx_len)
    tm = PROJ_TM
    specs = [pl.BlockSpec((tm, D_MODEL), lambda i: (i, 0)),
             pl.BlockSpec((tm, D_MODEL), lambda i: (i, 0)),
             pl.BlockSpec((tm, D_MODEL), lambda i: (i, 4)),
             pl.BlockSpec((1, D_MODEL), lambda i: (0, 0))]
    return out_projection(_hgrn_finish, [of, ob, proj, norm_g.reshape(1, D_MODEL)], specs,
                          w_out.astype(BF16), x, rowmod, gate_idx=2, tm=tm)


def _attn_head_perm():
    r_per = ATT_HEADS // ATT_KV
    heads = [(2 * p + j) * r_per + r for p in range(ATT_KV // 2) for r in range(r_per) for j in range(2)]
    return np.concatenate([np.arange(h * ATT_HD, (h + 1) * ATT_HD) for h in heads])


def _rope_tables(seq_len, ctx_len, grid_w):
    rows = seq_len // grid_w
    row = jnp.repeat(jnp.arange(rows, dtype=F32), grid_w)
    col = jnp.tile(jnp.arange(grid_w, dtype=F32), rows)
    inv = ROPE_THETA ** (-jnp.arange(ROPE_FREQS, dtype=F32) / ROPE_FREQS)
    ang_r = row[:, None] * inv
    ang_c = col[:, None] * inv
    ang = jnp.concatenate([ang_r, ang_r, ang_c, ang_c], axis=-1)
    cos = jnp.concatenate([jnp.ones((ctx_len, ATT_HD), F32), jnp.cos(ang)], axis=0)
    sin = jnp.concatenate([jnp.zeros((ctx_len, ATT_HD), F32), jnp.sin(ang)], axis=0)
    return jnp.tile(cos, (1, 2)), jnp.tile(sin, (1, 2))


def _rope_matrices():
    r64 = np.zeros((ATT_HD, ATT_HD), np.float32)
    fq = ROPE_FREQS
    for ax in range(2):
        o = ax * 2 * fq
        for i in range(fq):
            r64[o + fq + i, o + i] = -1.0
            r64[o + i, o + fq + i] = 1.0
    n = ATT_HEADS
    bd = np.kron(np.eye(n, dtype=np.float32), np.ones((ATT_HD, ATT_HD), np.float32))
    rot = np.kron(np.eye(n, dtype=np.float32), r64)
    return jnp.asarray(bd, BF16), jnp.asarray(rot, BF16)


def attn_layer(x, g, rowmod, w_qkv, q_g, k_g, w_o, *, bsz, ctx_len, grid_w):
    m = x.shape[0]
    seq_len = m // bsz - ctx_len
    perm = _attn_head_perm()
    nq = ATT_HEADS * ATT_HD
    w = jnp.concatenate([w_qkv[:, :nq][:, perm], w_qkv[:, nq:]], axis=1).astype(BF16)
    cos_t, sin_t = _rope_tables(seq_len, ctx_len, grid_w)
    bd, rot = _rope_matrices()
    qg = jnp.tile(q_g.astype(F32), ATT_HEADS).reshape(1, nq)
    kg = jnp.tile(k_g.astype(F32), ATT_KV).reshape(1, ATT_KV * ATT_HD)
    q, k, v = attn_qkv(x, g, rowmod, w, cos_t, sin_t, qg, kg, bd, rot, bsz=bsz, shift_idx=0, scale_idx=1)
    o = attention(q, k, v, bsz=bsz, ctx_len=ctx_len)
    tm = PROJ_TM
    specs = [pl.BlockSpec((tm, nq), lambda i: (i, 0))]
    return out_projection(_identity_pro, [o], specs, w_o[perm, :].astype(BF16), x, rowmod, gate_idx=2, tm=tm)


def mlstm_layer(x, g, rowmod, w_up, conv_w, conv_b, w_q, w_k, w_v, w_gate, b_gate, skip, norm_g, w_down,
                *, bsz, ctx_len):
    up, = in_projection(x, g, rowmod, [w_up.astype(BF16)], [BF16], shift_idx=0, scale_idx=1)
    xc = dwconv_silu(up, 0, ML_INNER, conv_w, conv_b, bsz=bsz, ctx_len=ctx_len)
    q, k, v, gates = mlstm_qkv(xc, up, w_q.astype(BF16), w_k.astype(BF16), w_v.astype(BF16),
                               w_gate.astype(BF16), b_gate)
    gt_dir, gtt_dir = _dir_split(gates, 2 * ML_HEADS)
    hf = mlstm_scan(q, k, v, gt_dir, gtt_dir, bsz=bsz, ctx_len=ctx_len, bwd=False)
    hb = mlstm_scan(q, k, v, gt_dir, gtt_dir, bsz=bsz, ctx_len=ctx_len, bwd=True)
    tm = PROJ_TM
    specs = [pl.BlockSpec((tm, ML_INNER), lambda i: (i, 0)),
             pl.BlockSpec((tm, ML_INNER), lambda i: (i, 0)),
             pl.BlockSpec((tm, ML_INNER), lambda i: (i, 0)),
             pl.BlockSpec((tm, ML_INNER), lambda i: (i, 1)),
             pl.BlockSpec((1, ML_INNER), lambda i: (0, 0)),
             pl.BlockSpec((1, ML_INNER), lambda i: (0, 0))]
    return out_projection(_mlstm_finish, [hf, hb, xc, up, skip.reshape(1, ML_INNER), norm_g.reshape(1, ML_INNER)],
                          specs, w_down.astype(BF16), x, rowmod, gate_idx=2, tm=tm)


def kernel(x, c, ctx, c_ctx, ada_w, ada_b, norm_g, ssd_w_in, ssd_conv_w, ssd_conv_b, ssd_dt_bias, ssd_a_log, ssd_d, ssd_norm_g, ssd_w_out, hgrn_w_in, hgrn_lb, hgrn_norm_g, hgrn_w_out, attn_w_qkv, attn_q_g, attn_k_g, attn_w_o, mlstm_w_up, mlstm_conv_w, mlstm_conv_b, mlstm_w_q, mlstm_w_k, mlstm_w_v, mlstm_w_gate, mlstm_b_gate, mlstm_skip, mlstm_norm_g, mlstm_w_down, ffn_w1, ffn_w3, ffn_w2, moe_router, moe_w1, moe_w3, moe_w2):
    bsz, seq_len, _ = x.shape
    ctx_len = ctx.shape[1]
    depth = ada_w.shape[0]
    grid_w = 64
    t_all = ctx_len + seq_len
    m = bsz * t_all
    xa = jnp.concatenate([ctx, x], axis=1).reshape(m, D_MODEL)
    c_pad = jnp.zeros((8, D_MODEL), F32).at[:bsz].set(c).at[bsz].set(c_ctx)
    groups_per_batch = t_all // ROW_GROUP
    ctx_groups = ctx_len // ROW_GROUP
    gidx = np.array([bsz if (gi % groups_per_batch) < ctx_groups else gi // groups_per_batch
                     for gi in range(m // ROW_GROUP)], np.int32)
    lb_all = jnp.cumsum(jax.nn.softmax(hgrn_lb.astype(F32), axis=1), axis=1)
    lb_all = lb_all - lb_all[:, :1]
    kw = dict(bsz=bsz, ctx_len=ctx_len)
    ffn_w = [w.astype(BF16) for w in (ffn_w1, ffn_w3, ffn_w2)]
    moe_w = (moe_w1, moe_w3, moe_w2)
    for i in range(depth):
        mod = ada_modulation(c_pad, ada_w, ada_b, i).reshape(8, 6, D_MODEL)
        rowmod = mod[gidx]
        kind, j = i % 4, i // 4
        if kind == 0:
            xa = ssd_layer(xa, norm_g[i, 0], rowmod, ssd_w_in[j], ssd_conv_w[j], ssd_conv_b[j], ssd_dt_bias[j],
                           ssd_a_log[j], ssd_d[j], ssd_norm_g[j], ssd_w_out[j], **kw)
        elif kind == 1:
            xa = hgrn_layer(xa, norm_g[i, 0], rowmod, hgrn_w_in[j], lb_all[:, i], hgrn_norm_g[j], hgrn_w_out[j], **kw)
        elif kind == 2:
            xa = attn_layer(xa, norm_g[i, 0], rowmod, attn_w_qkv[j], attn_q_g[j], attn_k_g[j], attn_w_o[j],
                            grid_w=grid_w, **kw)
        else:
            xa = mlstm_layer(xa, norm_g[i, 0], rowmod, mlstm_w_up[j], mlstm_conv_w[j], mlstm_conv_b[j], mlstm_w_q[j],
                             mlstm_w_k[j], mlstm_w_v[j], mlstm_w_gate[j], mlstm_b_gate[j], mlstm_skip[j],
                             mlstm_norm_g[j], mlstm_w_down[j], **kw)
        if i % 2 == 0:
            xa = dense_ffn(xa, norm_g[i, 1], rowmod, *ffn_w, i // 2)
        else:
            last = i == depth - 1
            xa = moe_ffn(xa, norm_g[i, 1], rowmod, moe_router[i // 2], *moe_w, i // 2,
                         latent_only=(bsz, ctx_len) if last else None)
            if last:
                return xa.reshape(bsz, seq_len, D_MODEL)
    return xa.reshape(bsz, t_all, D_MODEL)[:, ctx_len:]
```

```python
import functools
import math

import jax
import jax.numpy as jnp
import numpy as np
from jax import lax
from jax.experimental import pallas as pl
from jax.experimental.pallas import tpu as pltpu

F32 = jnp.float32
BF16 = jnp.bfloat16
HI = lax.Precision.HIGHEST

D_MODEL = 1024
EPS = 1e-6
ROW_GROUP = 256
PROJ_TM = 512
CONV_W = 5
NEG_BIG = -1e30
LOG2E = math.log2(math.e)
VMEM_LIMIT = 56 << 20

SSD_INNER = 2 * D_MODEL
SSD_P = 64
SSD_HEADS = SSD_INNER // SSD_P
SSD_N = 128
SSD_GROUPS = 8
SSD_GW = SSD_INNER // SSD_GROUPS
SCAN_L = 128

HG_HEADS = 8
HG_DK = 128
HG_SUB = 8

ATT_HEADS = 16
ATT_KV = 4
ATT_HD = 64
ROPE_THETA = 10000.0
ROPE_FREQS = ATT_HD // 4
ATT_TQ = 128
ATT_ONES = 16
ATT_SHIFT_MAX = 60.0

ML_INNER = 2 * D_MODEL
ML_HEADS = 4
ML_DH = ML_INNER // ML_HEADS
ML_L = 256

D_FF = 7 * D_MODEL // 2
N_EXPERTS = 8
MOE_TM = 1024


def _cparams(sem):
    return pltpu.CompilerParams(dimension_semantics=sem, vmem_limit_bytes=VMEM_LIMIT)


def _dot(a, b):
    return jnp.dot(a, b, preferred_element_type=F32)


def _dot_hi(a, b):
    return jnp.dot(a, b, preferred_element_type=F32, precision=HI)


def _dot_nt(a, b):
    return lax.dot_general(a, b, (((1,), (1,)), ((), ())), preferred_element_type=F32)


def _dot_tn(a, b):
    return lax.dot_general(a, b, (((0,), (0,)), ((), ())), preferred_element_type=F32)


def _silu(x):
    return x * jax.nn.sigmoid(x)


def _log_sigmoid(x):
    return jnp.minimum(x, 0.0) - jnp.log1p(jnp.exp(-jnp.abs(x)))


def _softplus(x):
    return jnp.maximum(x, 0.0) + jnp.log1p(jnp.exp(-jnp.abs(x)))


def _pick_tile(m, pref):
    t = pref
    while m % t:
        t //= 2
    return t


def _rows_scale_shift(y, mod_ref, scale_idx, shift_idx):
    parts = []
    for gi in range(y.shape[0] // ROW_GROUP):
        sl = y[gi * ROW_GROUP:(gi + 1) * ROW_GROUP]
        parts.append(sl * (1.0 + mod_ref[gi, scale_idx:scale_idx + 1, :]) + mod_ref[gi, shift_idx:shift_idx + 1, :])
    return parts[0] if len(parts) == 1 else jnp.concatenate(parts, axis=0)


def _rows_gate_residual(x, acc, mod_ref, gate_idx):
    parts = []
    for gi in range(x.shape[0] // ROW_GROUP):
        sl = slice(gi * ROW_GROUP, (gi + 1) * ROW_GROUP)
        parts.append(x[sl] + mod_ref[gi, gate_idx:gate_idx + 1, :] * acc[sl])
    return parts[0] if len(parts) == 1 else jnp.concatenate(parts, axis=0)


def _norm_mod(x, g, mod_ref, shift_idx, scale_idx):
    y = x * lax.rsqrt(jnp.mean(x * x, axis=-1, keepdims=True) + EPS) * g
    return _rows_scale_shift(y, mod_ref, scale_idx, shift_idx)


def _split2(a):
    hi = a.astype(BF16)
    return hi, (a - hi.astype(F32)).astype(BF16)


def _dot_sel_l(sel, a):
    hi, lo = _split2(a)
    return _dot(sel, hi) + _dot(sel, lo)


def _dot_sel_r(a, sel):
    hi, lo = _split2(a)
    return _dot(hi, sel) + _dot(lo, sel)


def _scan_masks(n, bwd):
    ii = lax.broadcasted_iota(jnp.int32, (n, n), 0)
    jj = lax.broadcasted_iota(jnp.int32, (n, n), 1)
    visible = (ii <= jj) if bwd else (ii >= jj)
    visible_t = (ii >= jj) if bwd else (ii <= jj)
    tri = jnp.where(visible, 1.0, 0.0).astype(BF16)
    trit = jnp.where(visible_t, 1.0, 0.0).astype(BF16)
    return ii, jj, visible, tri, trit


def _ada_kernel(c_ref, w_ref, b_ref, o_ref):
    c = c_ref[...]
    o_ref[...] = _dot_hi(_silu(c), w_ref[...]) + b_ref[...]


def ada_modulation(c_pad, w, b, layer):
    n = w.shape[2]
    tn = 1024
    return pl.pallas_call(
        _ada_kernel,
        grid=(n // tn,),
        in_specs=[pl.BlockSpec(c_pad.shape, lambda j: (0, 0)),
                  pl.BlockSpec((None, D_MODEL, tn), lambda j: (layer, 0, j)),
                  pl.BlockSpec((None, 1, tn), lambda j: (layer, 0, j))],
        out_specs=pl.BlockSpec((c_pad.shape[0], tn), lambda j: (0, j)),
        out_shape=jax.ShapeDtypeStruct((c_pad.shape[0], n), F32),
        compiler_params=_cparams(("arbitrary",)),
        name="ada_modulation",
    )(c_pad, w, b.reshape(b.shape[0], 1, n))


def _inproj_kernel(x_ref, g_ref, mod_ref, *refs, n_w, shift_idx, scale_idx, tn):
    h = _norm_mod(x_ref[...], g_ref[...], mod_ref, shift_idx, scale_idx).astype(BF16)
    for w_ref, o_ref in zip(refs[:n_w], refs[n_w:]):
        n = o_ref.shape[1]
        step = min(tn, n)
        for j in range(n // step):
            o_ref[:, j * step:(j + 1) * step] = _dot(h, w_ref[:, j * step:(j + 1) * step]).astype(o_ref.dtype)


def in_projection(x, g, rowmod, ws, out_dtypes, *, shift_idx, scale_idx, tm=PROJ_TM, tn=512):
    m = x.shape[0]
    tm = _pick_tile(m, tm)
    gm = tm // ROW_GROUP
    in_specs = [pl.BlockSpec((tm, D_MODEL), lambda i: (i, 0)),
                pl.BlockSpec((1, D_MODEL), lambda i: (0, 0)),
                pl.BlockSpec((gm, 6, D_MODEL), lambda i: (i, 0, 0))]
    in_specs += [pl.BlockSpec(w.shape, lambda i: (0, 0)) for w in ws]
    out_specs = [pl.BlockSpec((tm, w.shape[1]), lambda i: (i, 0)) for w in ws]
    out_shape = [jax.ShapeDtypeStruct((m, w.shape[1]), dt) for w, dt in zip(ws, out_dtypes)]
    return pl.pallas_call(
        functools.partial(_inproj_kernel, n_w=len(ws), shift_idx=shift_idx, scale_idx=scale_idx, tn=tn),
        grid=(m // tm,),
        in_specs=in_specs, out_specs=out_specs, out_shape=out_shape,
        compiler_params=_cparams(("parallel",)),
        name="in_projection",
    )(x, g.reshape(1, D_MODEL), rowmod, *ws)


def _outproj_kernel(*refs, n_pro, pro_fn, gate_idx):
    pro_refs = refs[:n_pro]
    w_ref, x_ref, mod_ref, o_ref = refs[n_pro:]
    a = pro_fn(*pro_refs).astype(BF16)
    acc = _dot(a, w_ref[...])
    o_ref[...] = _rows_gate_residual(x_ref[...], acc, mod_ref, gate_idx)


def out_projection(pro_fn, pro_args, pro_specs, w, x, rowmod, *, gate_idx, tm):
    m = x.shape[0]
    gm = tm // ROW_GROUP
    in_specs = list(pro_specs) + [pl.BlockSpec(w.shape, lambda i: (0, 0)),
                                  pl.BlockSpec((tm, D_MODEL), lambda i: (i, 0)),
                                  pl.BlockSpec((gm, 6, D_MODEL), lambda i: (i, 0, 0))]
    return pl.pallas_call(
        functools.partial(_outproj_kernel, n_pro=len(pro_args), pro_fn=pro_fn, gate_idx=gate_idx),
        grid=(m // tm,),
        in_specs=in_specs,
        out_specs=pl.BlockSpec((tm, D_MODEL), lambda i: (i, 0)),
        out_shape=jax.ShapeDtypeStruct((m, D_MODEL), F32),
        compiler_params=_cparams(("parallel",)),
        name="out_projection",
    )(*pro_args, w, x, rowmod)


def _conv_kernel(u_ref, w_ref, b_ref, o_ref, *, ctx_len):
    x = u_ref[...].astype(F32)
    t_all = x.shape[0]
    pad = CONV_W // 2
    gap = jnp.zeros((8, x.shape[1]), F32)
    xp = jnp.concatenate([gap, x[:ctx_len], gap, x[ctx_len:], gap], axis=0)
    n = xp.shape[0]
    acc = b_ref[...] + w_ref[pad:pad + 1, :] * xp
    for off in range(-pad, pad + 1):
        if off:
            acc = acc + w_ref[pad + off:pad + off + 1, :] * pltpu.roll(xp, (-off) % n, 0)
    y = _silu(acc)
    o_ref[...] = jnp.concatenate([y[8:8 + ctx_len], y[16 + ctx_len:16 + t_all]], axis=0).astype(o_ref.dtype)


def dwconv_silu(u, col_off, width, w, b, *, bsz, ctx_len, tc=256):
    m = u.shape[0]
    t_all = m // bsz
    cb = col_off // tc
    return pl.pallas_call(
        functools.partial(_conv_kernel, ctx_len=ctx_len),
        grid=(bsz, width // tc),
        in_specs=[pl.BlockSpec((t_all, tc), lambda bi, j: (bi, j + cb)),
                  pl.BlockSpec((CONV_W, tc), lambda bi, j: (0, j)),
                  pl.BlockSpec((1, tc), lambda bi, j: (0, j))],
        out_specs=pl.BlockSpec((t_all, tc), lambda bi, j: (bi, j)),
        out_shape=jax.ShapeDtypeStruct((m, width), BF16),
        compiler_params=_cparams(("parallel", "parallel")),
        name="dwconv_silu",
    )(u, w, b.reshape(1, width))


def _scan_chunk_index(bwd, c, n_ctx_chunks, n_chunks):
    if not bwd:
        return c
    return jnp.where(c < n_ctx_chunks, n_ctx_chunks - 1 - c, n_chunks - 1 + n_ctx_chunks - c)


def _ssd_scan_kernel(x_ref, b_ref, c_ref, dt_ref, dtt_ref, bias_ref, biast_ref, alog_ref, alogt_ref, exp_ref,
                     y_ref, state, *, bwd):
    ci = pl.program_id(1)
    L = SCAN_L
    H = SSD_HEADS

    @pl.when(ci == 0)
    def _():
        state[...] = jnp.zeros_like(state)

    _, _, visible, tri, trit = _scan_masks(L, bwd)
    dt = _softplus(dt_ref[...] + bias_ref[...])
    dtt = _softplus(dtt_ref[...] + biast_ref[...])
    a_neg = -jnp.exp(alog_ref[...])
    a_negt = -jnp.exp(alogt_ref[...])
    dta = dt * a_neg
    a_col = _dot_sel_l(tri, dta)
    a_row = _dot_sel_r(dtt * a_negt, trit)
    a_end = jnp.sum(dta, axis=0, keepdims=True)
    per_head = jnp.concatenate([jnp.exp(a_col), jnp.exp(a_end - a_col) * dt,
                                jnp.broadcast_to(jnp.exp(a_end), (16, H))], axis=0)
    per_chan = _dot_sel_r(per_head, exp_ref[...])
    ea_x, w_x, eend_x = per_chan[:L], per_chan[L:2 * L], per_chan[2 * L:2 * L + 1]
    lane = lax.broadcasted_iota(jnp.int32, (L, 2 * SSD_P), 1)
    for g in range(SSD_GROUPS):
        gs = slice(g * SSD_GW, (g + 1) * SSD_GW)
        ns = slice(g * SSD_N, (g + 1) * SSD_N)
        bg = b_ref[:, ns]
        cg = c_ref[:, ns]
        xgb = x_ref[:, gs]
        cb = _dot_nt(cg, bg)
        sg = state[:, gs]
        y_inter = _dot(cg, sg.astype(BF16)) * ea_x[:, gs]
        pieces = []
        for pr in range(SSD_GW // (2 * SSD_P)):
            ys = []
            for q in range(2):
                h = g * (SSD_GW // SSD_P) + 2 * pr + q
                rel = a_col[:, h:h + 1] - a_row[h:h + 1, :]
                gm = cb * jnp.exp(jnp.where(visible, rel, NEG_BIG)) * dtt[h:h + 1, :]
                ys.append(_dot(gm.astype(BF16), xgb[:, pr * 2 * SSD_P:(pr + 1) * 2 * SSD_P]))
            pieces.append(jnp.where(lane < SSD_P, ys[0], ys[1]))
        y = jnp.concatenate(pieces, axis=1) + y_inter
        y_ref[:, gs] = y.astype(y_ref.dtype)
        xw = (xgb.astype(F32) * w_x[:, gs]).astype(BF16)
        state[:, gs] = sg * eend_x[:, gs] + _dot_tn(bg, xw)


def ssd_scan(u, dt_dir, dtt_dir, dt_bias, a_log, *, bsz, ctx_len, bwd):
    m = u.shape[0]
    t_all = m // bsz
    nc = t_all // SCAN_L
    ncc = ctx_len // SCAN_L
    L, H = SCAN_L, SSD_HEADS
    d = int(bwd)

    def row(bi, c):
        return bi * nc + _scan_chunk_index(bwd, c, ncc, nc)

    bn = SSD_INNER // (SSD_GROUPS * SSD_N)
    expand = jnp.asarray(np.kron(np.eye(H, dtype=np.float32), np.ones((1, SSD_P), np.float32)), BF16)
    return pl.pallas_call(
        functools.partial(_ssd_scan_kernel, bwd=bwd),
        grid=(bsz, nc),
        in_specs=[pl.BlockSpec((L, SSD_INNER), lambda bi, c: (row(bi, c), 0)),
                  pl.BlockSpec((L, SSD_GROUPS * SSD_N), lambda bi, c: (row(bi, c), bn)),
                  pl.BlockSpec((L, SSD_GROUPS * SSD_N), lambda bi, c: (row(bi, c), bn + 1)),
                  pl.BlockSpec((None, L, H), lambda bi, c: (d, row(bi, c), 0)),
                  pl.BlockSpec((None, H, L), lambda bi, c: (d, 0, row(bi, c))),
                  pl.BlockSpec((None, 1, H), lambda bi, c: (d, 0, 0)),
                  pl.BlockSpec((None, H, 1), lambda bi, c: (d, 0, 0)),
                  pl.BlockSpec((None, 1, H), lambda bi, c: (d, 0, 0)),
                  pl.BlockSpec((None, H, 1), lambda bi, c: (d, 0, 0)),
                  pl.BlockSpec((H, SSD_INNER), lambda bi, c: (0, 0))],
        out_specs=pl.BlockSpec((L, SSD_INNER), lambda bi, c: (row(bi, c), 0)),
        out_shape=jax.ShapeDtypeStruct((m, SSD_INNER), BF16),
        scratch_shapes=[pltpu.VMEM((SSD_N, SSD_INNER), F32)],
        compiler_params=_cparams(("arbitrary", "arbitrary")),
        name="ssd_scan_bwd" if bwd else "ssd_scan_fwd",
    )(u, u, u, dt_dir, dtt_dir, dt_bias.reshape(2, 1, H), dt_bias.reshape(2, H, 1),
      a_log.reshape(2, 1, H), a_log.reshape(2, H, 1), expand)


def _ssd_finish(yf_ref, yb_ref, xs_ref, z_ref, d_ref, g_ref):
    y = yf_ref[...].astype(F32) + yb_ref[...].astype(F32) + d_ref[...] * xs_ref[...].astype(F32)
    y = y * _silu(z_ref[...].astype(F32))
    parts = []
    for g in range(SSD_GROUPS):
        sl = y[:, g * SSD_GW:(g + 1) * SSD_GW]
        parts.append(sl * lax.rsqrt(jnp.mean(sl * sl, axis=-1, keepdims=True) + EPS))
    return jnp.concatenate(parts, axis=1) * g_ref[...]


def _gla_direction(q_ref, f_ref, v_ref, llb_ref, l1m_ref, omlb_ref, place_ref, o_ref, state, *, bwd):
    L, C = SCAN_L, HG_SUB
    nb = L // C
    dk = HG_DK
    ii, jj, visible, tri, _ = _scan_masks(L, bwd)
    C2 = 2 * C
    nb2 = L // C2
    band_mask = ((ii // C) == (jj // C)) & visible
    mid_mask = ((ii // C2) == (jj // C2)) & visible
    off_mask = ((ii // C2) < (jj // C2)) if bwd else ((ii // C2) > (jj // C2))
    last = 0 if bwd else L - 1
    blocks2 = range(1, nb2) if bwd else range(nb2 - 1)

    def head(h, carry):
        ls = slice(h * dk, (h + 1) * dk)
        q = _silu(q_ref[:, ls].astype(F32)) * (dk ** -0.5)
        f = f_ref[:, ls].astype(F32)
        v = v_ref[:, ls]
        a = llb_ref[:, ls]
        b = l1m_ref[:, ls] + _log_sigmoid(f)
        lf = (jnp.maximum(a, b) + jnp.log1p(jnp.exp(-jnp.abs(a - b)))) * LOG2E
        k = omlb_ref[:, ls] * jax.nn.sigmoid(-f)
        fd = jnp.exp2(lf)
        cum = _dot_sel_l(tri, lf)
        c_end = cum[last:last + 1, :]
        kd = k
        es = [(q * kd).astype(BF16)]
        for dl in range(1, C):
            kd = fd * pltpu.roll(kd, (L - 1) if bwd else 1, 0)
            es.append((q * kd).astype(BF16))
        band = _dot(jnp.concatenate(es, axis=1), place_ref[...])
        band = pltpu.roll(band, 0, 1, stride=1, stride_axis=0)
        def end_rows(c):
            rows = [(j * c if bwd else (j + 1) * c - 1) for j in range(L // c)]
            return [cum[r:r + 1, :] for r in rows]

        e1, e2 = end_rows(C), end_rows(C2)
        kt1 = [k[j * C:(j + 1) * C] * jnp.exp2(e1[j] - cum[j * C:(j + 1) * C]) for j in range(nb)]
        zero1 = jnp.zeros((C, dk), F32)
        qmid, kt2 = [], []
        for m2 in range(nb2):
            early, late = (2 * m2 + 1, 2 * m2) if bwd else (2 * m2, 2 * m2 + 1)
            rs = slice(late * C, (late + 1) * C)
            piece = q[rs] * jnp.exp2(cum[rs] - e1[early])
            k_early = kt1[early] * jnp.exp2(e2[m2] - e1[early])
            qmid += [piece, zero1] if bwd else [zero1, piece]
            kt2 += [kt1[late], k_early] if bwd else [k_early, kt1[late]]
        mid = _dot_nt(jnp.concatenate(qmid, axis=0).astype(BF16), jnp.concatenate(kt1, axis=0).astype(BF16))
        kt2b = jnp.concatenate(kt2, axis=0).astype(BF16)
        zero2 = jnp.zeros((C2, dk), F32)
        kcols, qcols = [], []
        pieces = [None] * nb2
        for j in (blocks2 if bwd else reversed(blocks2)):
            parts = [jnp.zeros((j * C2, dk), BF16), kt2b[j * C2:(j + 1) * C2], jnp.zeros((L - (j + 1) * C2, dk), BF16)]
            kcols.append(jnp.concatenate([x for x in parts if x.shape[0]], axis=0))
            near = j - 1 if bwd else j + 1
            if near in blocks2:
                step = jnp.exp2(e2[near] - e2[j])
                for i in (range(0, near) if bwd else range(near + 1, nb2)):
                    pieces[i] = pieces[i] * step
            rs = slice(near * C2, (near + 1) * C2)
            pieces[near] = q[rs] * jnp.exp2(cum[rs] - e2[j])
            qcols.append(jnp.concatenate([zero2 if p is None else p for p in pieces], axis=0).astype(BF16))
        off = _dot_nt(jnp.concatenate(qcols, axis=1), jnp.concatenate(kcols, axis=1))
        att = jnp.where(band_mask, band, jnp.where(mid_mask, mid, jnp.where(off_mask, off, 0.0)))
        st = state[h]
        o = _dot(att.astype(BF16), v) + _dot_nt((q * jnp.exp2(cum)).astype(BF16), st.astype(BF16))
        o_ref[:, ls] = o.astype(o_ref.dtype)
        kw = (k * jnp.exp2(c_end - cum)).astype(BF16)
        state[h] = st * jnp.exp2(c_end) + _dot_tn(v, kw)
        return carry

    return head


def _gla_scan_kernel(qf_ref, ff_ref, vf_ref, qb_ref, fb_ref, vb_ref, llb_ref, l1m_ref, omlb_ref, pf_ref, pb_ref,
                     of_ref, ob_ref, state):
    @pl.when(pl.program_id(1) == 0)
    def _():
        state[...] = jnp.zeros_like(state)

    fwd = _gla_direction(qf_ref, ff_ref, vf_ref, llb_ref.at[0], l1m_ref.at[0], omlb_ref.at[0], pf_ref, of_ref,
                         state.at[0], bwd=False)
    bwd = _gla_direction(qb_ref, fb_ref, vb_ref, llb_ref.at[1], l1m_ref.at[1], omlb_ref.at[1], pb_ref, ob_ref,
                         state.at[1], bwd=True)
    for h in range(HG_HEADS):
        fwd(h, 0)
        bwd(h, 0)


def _gla_place_matrix(bwd):
    pm = np.zeros((HG_SUB * HG_DK, SCAN_L), np.float32)
    for dl in range(HG_SUB):
        pm[dl * HG_DK:(dl + 1) * HG_DK, dl if bwd else (SCAN_L - dl) % SCAN_L] = 1.0
    return jnp.asarray(pm, BF16)


def gla_scan(proj, llb, l1m, omlb, *, bsz, ctx_len):
    m = proj.shape[0]
    t_all = m // bsz
    nc = t_all // SCAN_L
    ncc = ctx_len // SCAN_L
    L = SCAN_L

    def blk(bwd, col):
        return pl.BlockSpec((L, D_MODEL), lambda bi, c: (bi * nc + _scan_chunk_index(bwd, c, ncc, nc), col))

    vec = pl.BlockSpec((2, 1, D_MODEL), lambda bi, c: (0, 0, 0))
    pf, pb = _gla_place_matrix(False), _gla_place_matrix(True)
    const = pl.BlockSpec(pf.shape, lambda bi, c: (0, 0))
    return pl.pallas_call(
        _gla_scan_kernel,
        grid=(bsz, nc),
        in_specs=[blk(False, 0), blk(False, 1), blk(False, 3), blk(True, 0), blk(True, 2), blk(True, 3),
                  vec, vec, vec, const, const],
        out_specs=[blk(False, 0), blk(True, 0)],
        out_shape=[jax.ShapeDtypeStruct((m, D_MODEL), BF16)] * 2,
        scratch_shapes=[pltpu.VMEM((2, HG_HEADS, HG_DK, HG_DK), F32)],
        compiler_params=_cparams(("arbitrary", "arbitrary")),
        name="gla_scan",
    )(proj, proj, proj, proj, proj, proj, llb, l1m, omlb, pf, pb)


def _hgrn_finish(of_ref, ob_ref, gate_ref, g_ref):
    o = of_ref[...].astype(F32) + ob_ref[...].astype(F32)
    parts = []
    for h in range(HG_HEADS):
        sl = o[:, h * HG_DK:(h + 1) * HG_DK]
        parts.append(sl * lax.rsqrt(jnp.mean(sl * sl, axis=-1, keepdims=True) + EPS))
    return jnp.concatenate(parts, axis=1) * g_ref[...] * _silu(gate_ref[...].astype(F32))


def _attn_qkv_kernel(x_ref, g_ref, mod_ref, w_ref, cos_ref, sin_ref, qg_ref, kg_ref, bd_ref, rot_ref,
                     q_ref, k_ref, v_ref, *, shift_idx, scale_idx):
    h = _norm_mod(x_ref[...], g_ref[...], mod_ref, shift_idx, scale_idx).astype(BF16)
    nq = ATT_HEADS * ATT_HD
    nk = ATT_KV * ATT_HD

    def norm_rope(a, gain, width, out_scale):
        ss = _dot((a * a).astype(BF16), bd_ref[:width, :width])
        an = a * lax.rsqrt(ss * (1.0 / ATT_HD) + EPS) * gain
        rot = _dot(an.astype(BF16), rot_ref[:width, :width])
        cos = jnp.concatenate([cos_ref[...]] * (width // 128), axis=1)
        sin = jnp.concatenate([sin_ref[...]] * (width // 128), axis=1)
        return (an * cos + rot * sin) * out_scale

    q = _dot(h, w_ref[:, :nq])
    q_ref[...] = norm_rope(q, qg_ref[...], nq, ATT_HD ** -0.5 * math.log2(math.e)).astype(q_ref.dtype)
    k = _dot(h, w_ref[:, nq:nq + nk])
    k_ref[...] = norm_rope(k, kg_ref[...], nk, 1.0).astype(k_ref.dtype)
    v_ref[...] = _dot(h, w_ref[:, nq + nk:]).astype(v_ref.dtype)


def attn_qkv(x, g, rowmod, w, cos_t, sin_t, q_g, k_g, bd, rot, *, bsz, shift_idx, scale_idx):
    m = x.shape[0]
    tm = ROW_GROUP
    tpb = (m // bsz) // tm
    nq = ATT_HEADS * ATT_HD
    nk = ATT_KV * ATT_HD
    const = lambda shape: pl.BlockSpec(shape, lambda i: (0, 0))
    return pl.pallas_call(
        functools.partial(_attn_qkv_kernel, shift_idx=shift_idx, scale_idx=scale_idx),
        grid=(m // tm,),
        in_specs=[pl.BlockSpec((tm, D_MODEL), lambda i: (i, 0)),
                  const((1, D_MODEL)),
                  pl.BlockSpec((1, 6, D_MODEL), lambda i: (i, 0, 0)),
                  const(w.shape),
                  pl.BlockSpec((tm, 128), lambda i: (i % tpb, 0)),
                  pl.BlockSpec((tm, 128), lambda i: (i % tpb, 0)),
                  const((1, nq)), const((1, nk)), const(bd.shape), const(rot.shape)],
        out_specs=[pl.BlockSpec((tm, nq), lambda i: (i, 0)),
                   pl.BlockSpec((tm, nk), lambda i: (i, 0)),
                   pl.BlockSpec((tm, nk), lambda i: (i, 0))],
        out_shape=[jax.ShapeDtypeStruct((m, nq), BF16),
                   jax.ShapeDtypeStruct((m, nk), BF16),
                   jax.ShapeDtypeStruct((m, nk), BF16)],
        compiler_params=_cparams(("parallel",)),
        name="attn_qkv",
    )(x, g.reshape(1, D_MODEL), rowmod, w, cos_t, sin_t, q_g, k_g, bd, rot)


def _attn_kernel(q_ref, k_ref, v_ref, o_ref, kaug, vtaug, kmax2, *, ctx_len):
    qt = pl.program_id(2)
    tq = q_ref.shape[0]
    hd = ATT_HD
    lane = lax.broadcasted_iota(jnp.int32, (1, 2 * hd), 1)
    n_r = q_ref.shape[1] // (2 * hd)
    aug_lane = (hd, 0)
    ii = lax.broadcasted_iota(jnp.int32, (2 * hd, 2 * hd), 0)
    jj = lax.broadcasted_iota(jnp.int32, (2 * hd, 2 * hd), 1)
    half_sum = jnp.where((ii // hd) == (jj // hd), 1.0, 0.0).astype(BF16)

    @pl.when(qt == 0)
    def _():
        k = k_ref[...]
        one = jnp.ones((), BF16)
        vt = v_ref[...].astype(F32).T.astype(BF16)
        ones_rows = jnp.ones((ATT_ONES, vt.shape[1]), BF16)
        for j in range(2):
            vtaug[j, 0:hd, :] = vt[j * hd:(j + 1) * hd, :]
            vtaug[j, hd:hd + ATT_ONES, :] = ones_rows
            kaug[j] = jnp.where(lane == aug_lane[j], one, k)
        kf = k.astype(F32)
        kn2 = _dot((kf * kf).astype(BF16), half_sum)
        kmax2[...] = jnp.broadcast_to(jnp.max(kn2, axis=0, keepdims=True), kmax2.shape)

    def queries(j):
        sel = (lane // hd) == j
        return jnp.concatenate(
            [jnp.where(sel, q_ref[:, r * 128:(r + 1) * 128], jnp.zeros((), BF16)) for r in range(n_r)], axis=0)

    def values(j, pt, n_keys):
        ot = _dot(vtaug[j, :, :n_keys], pt)
        return ot[0:hd] / ot[hd:hd + 1]

    def finish(tops):
        o = jnp.concatenate(tops, axis=0).T
        for r in range(n_r):
            o_ref[:, r * 128:(r + 1) * 128] = o[r * tq:(r + 1) * tq].astype(o_ref.dtype)

    def attend_exact(n_keys):
        tops = []
        for j in range(2):
            st = _dot_nt(k_ref[:n_keys, :], queries(j))
            mx = jnp.max(st, axis=0, keepdims=True)
            tops.append(values(j, jnp.exp2(st - mx).astype(BF16), n_keys))
        finish(tops)

    def attend_shifted(n_keys, shifts):
        tops = []
        for j in range(2):
            qa = jnp.where(lane == aug_lane[j], (-shifts[j]).astype(BF16), queries(j))
            pt = jnp.exp2(_dot_nt(kaug[j, :n_keys, :], qa)).astype(BF16)
            tops.append(values(j, pt, n_keys))
        finish(tops)

    def attend(n_keys):
        shifts = []
        ones = jnp.ones((2 * hd, 2 * hd), BF16)
        for j in range(2):
            qf = queries(j).astype(F32)
            qn2 = _dot((qf * qf).astype(BF16), ones)
            shifts.append(jnp.sqrt(qn2 * kmax2[0:1, j * hd:j * hd + 1]) * 1.02)
        worst = jnp.max(jnp.maximum(shifts[0], shifts[1]))
        small = worst < ATT_SHIFT_MAX

        @pl.when(small)
        def _():
            attend_shifted(n_keys, shifts)

        @pl.when(jnp.logical_not(small))
        def _():
            attend_exact(n_keys)

    n_ctx_tiles = ctx_len // tq

    @pl.when(qt < n_ctx_tiles)
    def _():
        attend(ctx_len)

    @pl.when(qt >= n_ctx_tiles)
    def _():
        attend(k_ref.shape[0])


def attention(q, k, v, *, bsz, ctx_len):
    m = q.shape[0]
    t_all = m // bsz
    tq = ATT_TQ
    nqt = t_all // tq
    n_pairs = ATT_KV // 2
    qw = q.shape[1] // n_pairs
    return pl.pallas_call(
        functools.partial(_attn_kernel, ctx_len=ctx_len),
        grid=(bsz, n_pairs, nqt),
        in_specs=[pl.BlockSpec((tq, qw), lambda bi, p, t: (bi * nqt + t, p)),
                  pl.BlockSpec((t_all, 2 * ATT_HD), lambda bi, p, t: (bi, p)),
                  pl.BlockSpec((t_all, 2 * ATT_HD), lambda bi, p, t: (bi, p))],
        out_specs=pl.BlockSpec((tq, qw), lambda bi, p, t: (bi * nqt + t, p)),
        out_shape=jax.ShapeDtypeStruct(q.shape, BF16),
        scratch_shapes=[pltpu.VMEM((2, t_all, 2 * ATT_HD), BF16),
                        pltpu.VMEM((2, ATT_HD + ATT_ONES, t_all), BF16),
                        pltpu.VMEM((8, 2 * ATT_HD), F32)],
        compiler_params=_cparams(("arbitrary", "arbitrary", "arbitrary")),
        name="attention",
    )(q, k, v)


def _identity_pro(a_ref):
    return a_ref[...]


def _mlstm_qkv_kernel(xc_ref, xm_ref, wq_ref, wk_ref, wv_ref, wg_ref, bg_ref, q_ref, k_ref, v_ref, gate_ref):
    gates = jnp.zeros(gate_ref.shape, F32) + bg_ref[...]
    for which, (src, w_ref, o_ref) in enumerate(((xc_ref, wq_ref, q_ref), (xc_ref, wk_ref, k_ref),
                                                 (xm_ref, wv_ref, v_ref))):
        for h in range(ML_HEADS):
            hs = slice(h * ML_DH, (h + 1) * ML_DH)
            r = _dot(src[:, hs], w_ref[h]).astype(BF16)
            o_ref[:, hs] = r
            gates = gates + _dot(r, wg_ref[which * ML_INNER + h * ML_DH:which * ML_INNER + (h + 1) * ML_DH, :])
    gate_ref[...] = gates


def mlstm_qkv(xc, up, wq, wk, wv, wg, bg, *, tm=PROJ_TM):
    m = xc.shape[0]
    ng = wg.shape[1]
    const = lambda shape: pl.BlockSpec(shape, lambda i: (0,) * len(shape))
    row = pl.BlockSpec((tm, ML_INNER), lambda i: (i, 0))
    return pl.pallas_call(
        _mlstm_qkv_kernel,
        grid=(m // tm,),
        in_specs=[row, row, const(wq.shape), const(wk.shape), const(wv.shape), const(wg.shape), const((1, ng))],
        out_specs=[row, row, row, pl.BlockSpec((tm, ng), lambda i: (i, 0))],
        out_shape=[jax.ShapeDtypeStruct((m, ML_INNER), BF16)] * 3 + [jax.ShapeDtypeStruct((m, ng), F32)],
        compiler_params=_cparams(("parallel",)),
        name="mlstm_qkv",
    )(xc, up, wq, wk, wv, wg, bg.reshape(1, ng))


def _mlstm_scan_kernel(q_ref, k_ref, v_ref, gt_ref, gtt_ref, h_ref, c_st, n_st, m_st, *, bwd):
    ci = pl.program_id(1)
    L = ML_L
    nh = ML_HEADS
    scale = ML_DH ** -0.5

    @pl.when(ci == 0)
    def _():
        c_st[...] = jnp.zeros_like(c_st)
        n_st[...] = jnp.zeros_like(n_st)
        m_st[...] = jnp.full(m_st.shape, NEG_BIG, F32)

    _, _, visible, tri, trit = _scan_masks(L, bwd)
    gt = gt_ref[...]
    gtt = gtt_ref[...]
    li_c, li_r = gt[:, :nh], gtt[:nh, :]
    lf_c, lf_r = _log_sigmoid(gt[:, nh:]), _log_sigmoid(gtt[nh:, :])
    cum_c = _dot_sel_l(tri, lf_c)
    cum_r = _dot_sel_r(lf_r, trit)
    end_c = jnp.sum(lf_c, axis=0, keepdims=True)
    for h in range(nh):
        hs = slice(h * ML_DH, (h + 1) * ML_DH)
        q = q_ref[:, hs]
        kb = k_ref[:, hs]
        k = kb.astype(F32)
        v = v_ref[:, hs]
        m_prev = m_st[h:h + 1, 0:1]
        cum_end = end_c[:, h:h + 1]
        dmat = jnp.where(visible, cum_c[:, h:h + 1] - cum_r[h:h + 1, :] + li_r[h:h + 1, :], -jnp.inf)
        inter = cum_c[:, h:h + 1] + m_prev
        m_t = jnp.maximum(inter, jnp.max(dmat, axis=1, keepdims=True))
        w = jnp.exp(dmat - m_t)
        w_c = jnp.exp(inter - m_t)
        qk = _dot_nt(q, kb) * scale * w
        cmat = c_st[h]
        num = _dot(qk.astype(BF16), v) + w_c * _dot(q, cmat.astype(BF16))
        qn = jnp.sum(q.astype(F32) * n_st[h:h + 1, :], axis=1, keepdims=True)
        den = jnp.sum(qk, axis=1, keepdims=True) + w_c * qn
        hv = num / jnp.maximum(jnp.abs(den), jnp.exp(-m_t))
        h_ref[:, hs] = hv.astype(h_ref.dtype)
        wend_c = cum_end - cum_c[:, h:h + 1] + li_c[:, h:h + 1]
        wend_r = cum_end - cum_r[h:h + 1, :] + li_r[h:h + 1, :]
        m_new = jnp.maximum(cum_end + m_prev, jnp.max(wend_r, axis=1, keepdims=True))
        a_old = jnp.exp(cum_end + m_prev - m_new)
        e_c = jnp.exp(wend_c - m_new) * scale
        e_r = jnp.exp(wend_r - m_new) * scale
        c_st[h] = a_old * cmat + _dot_tn((k * e_c).astype(BF16), v)
        e_r8 = jnp.broadcast_to(e_r, (8, L)).astype(BF16)
        n_st[h:h + 1, :] = a_old * n_st[h:h + 1, :] + _dot(e_r8, kb)[0:1, :]
        m_st[h:h + 1, :] = jnp.broadcast_to(m_new, (1, m_st.shape[1]))


def mlstm_scan(q, k, v, gt_dir, gtt_dir, *, bsz, ctx_len, bwd):
    m = q.shape[0]
    t_all = m // bsz
    nc = t_all // ML_L
    ncc = ctx_len // ML_L
    L, nh = ML_L, ML_HEADS
    d = int(bwd)

    def row(bi, c):
        return bi * nc + _scan_chunk_index(bwd, c, ncc, nc)

    blk = pl.BlockSpec((L, ML_INNER), lambda bi, c: (row(bi, c), 0))
    return pl.pallas_call(
        functools.partial(_mlstm_scan_kernel, bwd=bwd),
        grid=(bsz, nc),
        in_specs=[blk, blk, blk,
                  pl.BlockSpec((None, L, 2 * nh), lambda bi, c: (d, row(bi, c), 0)),
                  pl.BlockSpec((None, 2 * nh, L), lambda bi, c: (d, 0, row(bi, c)))],
        out_specs=pl.BlockSpec((L, ML_INNER), lambda bi, c: (row(bi, c), 0)),
        out_shape=jax.ShapeDtypeStruct((m, ML_INNER), BF16),
        scratch_shapes=[pltpu.VMEM((nh, ML_DH, ML_DH), F32),
                        pltpu.VMEM((8, ML_DH), F32),
                        pltpu.VMEM((8, 128), F32)],
        compiler_params=_cparams(("arbitrary", "arbitrary")),
        name="mlstm_scan_bwd" if bwd else "mlstm_scan_fwd",
    )(q, k, v, gt_dir, gtt_dir)


def _mlstm_finish(hf_ref, hb_ref, xc_ref, z_ref, skip_ref, g_ref):
    hsum = hf_ref[...].astype(F32) + hb_ref[...].astype(F32)
    parts = []
    for h in range(ML_HEADS):
        sl = hsum[:, h * ML_DH:(h + 1) * ML_DH]
        parts.append(sl * lax.rsqrt(jnp.mean(sl * sl, axis=-1, keepdims=True) + EPS))
    hn = jnp.concatenate(parts, axis=1) * g_ref[...]
    return (hn + skip_ref[...] * xc_ref[...].astype(F32)) * _silu(z_ref[...].astype(F32))


def _ffn_kernel(x_ref, g_ref, mod_ref, w1_ref, w3_ref, w2_ref, o_ref, h_sc, acc_sc, *, shift_idx, scale_idx, gate_idx):
    f = pl.program_id(1)

    @pl.when(f == 0)
    def _():
        h_sc[...] = _norm_mod(x_ref[...], g_ref[...], mod_ref, shift_idx, scale_idx).astype(BF16)
        acc_sc[...] = jnp.zeros_like(acc_sc)

    h = h_sc[...]
    a = _silu(_dot(h, w1_ref[...])) * _dot(h, w3_ref[...])
    acc_sc[...] += _dot(a.astype(BF16), w2_ref[...])

    @pl.when(f == pl.num_programs(1) - 1)
    def _():
        o_ref[...] = _rows_gate_residual(x_ref[...], acc_sc[...], mod_ref, gate_idx)


def dense_ffn(x, g, rowmod, w1, w3, w2, layer, *, tm=1024, tf=512):
    m = x.shape[0]
    tm = _pick_tile(m, tm)
    gm = tm // ROW_GROUP
    nf = D_FF // tf
    return pl.pallas_call(
        functools.partial(_ffn_kernel, shift_idx=3, scale_idx=4, gate_idx=5),
        grid=(m // tm, nf),
        in_specs=[pl.BlockSpec((tm, D_MODEL), lambda i, f: (i, 0)),
                  pl.BlockSpec((1, D_MODEL), lambda i, f: (0, 0)),
                  pl.BlockSpec((gm, 6, D_MODEL), lambda i, f: (i, 0, 0)),
                  pl.BlockSpec((None, D_MODEL, tf), lambda i, f: (layer, 0, f)),
                  pl.BlockSpec((None, D_MODEL, tf), lambda i, f: (layer, 0, f)),
                  pl.BlockSpec((None, tf, D_MODEL), lambda i, f: (layer, f, 0))],
        out_specs=pl.BlockSpec((tm, D_MODEL), lambda i, f: (i, 0)),
        out_shape=jax.ShapeDtypeStruct((m, D_MODEL), F32),
        scratch_shapes=[pltpu.VMEM((tm, D_MODEL), BF16), pltpu.VMEM((tm, D_MODEL), F32)],
        compiler_params=_cparams(("parallel", "arbitrary")),
        name="dense_ffn",
    )(x, g.reshape(1, D_MODEL), rowmod, w1, w3, w2)


def _router_kernel(x_ref, g_ref, mod_ref, wr_ref, h_ref, route_ref, cnt_ref, cnt_sc, *, shift_idx, scale_idx, n_real):
    i = pl.program_id(0)

    @pl.when(i == 0)
    def _():
        cnt_sc[...] = jnp.zeros_like(cnt_sc)

    @pl.when(i < n_real)
    def _():
        h = _norm_mod(x_ref[...], g_ref[...], mod_ref, shift_idx, scale_idx)
        h_ref[...] = h.astype(h_ref.dtype)
        h_hi, h_lo = _split2(h)
        w_hi, w_lo = _split2(wr_ref[...])
        logits = _dot(h_hi, w_hi) + (_dot(h_lo, w_hi) + _dot(h_hi, w_lo))
        tm = logits.shape[0]
        lane = lax.broadcasted_iota(jnp.int32, logits.shape, 1)
        logits = jnp.where(lane < N_EXPERTS, logits, -jnp.inf)
        m1 = jnp.max(logits, axis=-1, keepdims=True)
        i1 = jnp.min(jnp.where(logits == m1, lane, 128), axis=-1, keepdims=True)
        rest = jnp.where(lane == i1, -jnp.inf, logits)
        m2 = jnp.max(rest, axis=-1, keepdims=True)
        i2 = jnp.min(jnp.where(rest == m2, lane, 128), axis=-1, keepdims=True)
        e2 = jnp.exp(m2 - m1)
        w1 = 1.0 / (1.0 + e2)
        w2 = e2 / (1.0 + e2)
        chosen = jnp.where(lane == i1, 1.0, jnp.where(lane == i2, 1.0, 0.0))
        ii = lax.broadcasted_iota(jnp.int32, (tm, tm), 0)
        jj = lax.broadcasted_iota(jnp.int32, (tm, tm), 1)
        strict = jnp.where(ii > jj, 1.0, 0.0).astype(BF16)
        prefix = _dot(strict, chosen.astype(BF16)) + cnt_sc[0:1, :]
        r1 = jnp.sum(jnp.where(lane == i1, prefix, 0.0), axis=-1, keepdims=True)
        r2 = jnp.sum(jnp.where(lane == i2, prefix, 0.0), axis=-1, keepdims=True)
        cnt_sc[...] = cnt_sc[...] + jnp.sum(chosen, axis=0, keepdims=True)
        route = jnp.zeros(logits.shape, F32)
        for col, val in enumerate((i1.astype(F32), i2.astype(F32), w1, w2, r1, r2)):
            route = jnp.where(lane == col, val, route)
        route_ref[...] = route

    @pl.when(i >= n_real)
    def _():
        h_ref[...] = jnp.zeros_like(h_ref)
        route_ref[...] = jnp.zeros_like(route_ref)

    cnt_ref[...] = cnt_sc[...]


def moe_router(x, g, rowmod, wr_pad, p_rows, *, tm=PROJ_TM):
    m = x.shape[0]
    n_real = m // tm
    gm = tm // ROW_GROUP
    clamp = lambda i: jnp.minimum(i, n_real - 1)
    return pl.pallas_call(
        functools.partial(_router_kernel, shift_idx=3, scale_idx=4, n_real=n_real),
        grid=(p_rows // tm,),
        in_specs=[pl.BlockSpec((tm, D_MODEL), lambda i: (clamp(i), 0)),
                  pl.BlockSpec((1, D_MODEL), lambda i: (0, 0)),
                  pl.BlockSpec((gm, 6, D_MODEL), lambda i: (clamp(i), 0, 0)),
                  pl.BlockSpec(wr_pad.shape, lambda i: (0, 0))],
        out_specs=[pl.BlockSpec((tm, D_MODEL), lambda i: (i, 0)), pl.BlockSpec((tm, 128), lambda i: (i, 0)),
                   pl.BlockSpec((8, 128), lambda i: (0, 0))],
        out_shape=[jax.ShapeDtypeStruct((p_rows, D_MODEL), BF16), jax.ShapeDtypeStruct((p_rows, 128), F32),
                   jax.ShapeDtypeStruct((8, 128), F32)],
        scratch_shapes=[pltpu.VMEM((8, 128), F32)],
        compiler_params=_cparams(("arbitrary",)),
        name="moe_router",
    )(x, g.reshape(1, D_MODEL), rowmod, wr_pad)


def _moe_ffn_kernel(te_ref, nt_ref, h_ref, w1_ref, w3_ref, w2_ref, o_ref, acc_sc):
    i = pl.program_id(0)
    f = pl.program_id(1)
    live = i < nt_ref[0]

    @pl.when(f == 0)
    def _():
        acc_sc[...] = jnp.zeros_like(acc_sc)

    @pl.when(live)
    def _():
        h = h_ref[...]
        a = _silu(_dot(h, w1_ref[...].astype(BF16))) * _dot(h, w3_ref[...].astype(BF16))
        acc_sc[...] += _dot(a.astype(BF16), w2_ref[...].astype(BF16))

    @pl.when(f == pl.num_programs(1) - 1)
    def _():
        o_ref[...] = acc_sc[...].astype(o_ref.dtype)


def moe_grouped_ffn(h_sorted, tile_expert, n_tiles, w1, w3, w2, layer, *, tm=MOE_TM, tf=512):
    p = h_sorted.shape[0]
    nf = D_FF // tf
    def fblk(i, f, nt):
        return jnp.where(i < nt[0], f, nf - 1)

    grid_spec = pltpu.PrefetchScalarGridSpec(
        num_scalar_prefetch=2,
        grid=(p // tm, nf),
        in_specs=[pl.BlockSpec((tm, D_MODEL), lambda i, f, te, nt: (jnp.minimum(i, nt[0] - 1), 0)),
                  pl.BlockSpec((None, None, D_MODEL, tf), lambda i, f, te, nt: (layer, te[i], 0, fblk(i, f, nt))),
                  pl.BlockSpec((None, None, D_MODEL, tf), lambda i, f, te, nt: (layer, te[i], 0, fblk(i, f, nt))),
                  pl.BlockSpec((None, None, tf, D_MODEL), lambda i, f, te, nt: (layer, te[i], fblk(i, f, nt), 0))],
        out_specs=pl.BlockSpec((tm, D_MODEL), lambda i, f, te, nt: (i, 0)),
        scratch_shapes=[pltpu.VMEM((tm, D_MODEL), F32)])
    return pl.pallas_call(
        _moe_ffn_kernel,
        grid_spec=grid_spec,
        out_shape=jax.ShapeDtypeStruct((p, D_MODEL), BF16),
        compiler_params=_cparams(("arbitrary", "arbitrary")),
        name="moe_grouped_ffn",
    )(tile_expert, n_tiles, h_sorted, w1, w3, w2)


def _moe_combine_kernel(x_ref, ya_ref, yb_ref, route_ref, mod_ref, o_ref, *, gate_idx):
    r = route_ref[...]
    y = r[:, 2:3] * ya_ref[...].astype(F32) + r[:, 3:4] * yb_ref[...].astype(F32)
    o_ref[...] = _rows_gate_residual(x_ref[...], y, mod_ref, gate_idx)


def moe_combine(x, ya, yb, route, rowmod, *, latent_only=None):
    m = x.shape[0]
    tm = ROW_GROUP
    if latent_only is None:
        n_out, src = m // tm, (lambda i: i)
    else:
        bsz, ctx_len = latent_only
        ctx_tiles = ctx_len // tm
        lat_tiles = m // bsz // tm - ctx_tiles
        n_out, src = bsz * lat_tiles, (lambda i: i + (i // lat_tiles + 1) * ctx_tiles)
    row = pl.BlockSpec((tm, D_MODEL), lambda i: (src(i), 0))
    return pl.pallas_call(
        functools.partial(_moe_combine_kernel, gate_idx=5),
        grid=(n_out,),
        in_specs=[row, row, row, pl.BlockSpec((tm, 128), lambda i: (src(i), 0)),
                  pl.BlockSpec((1, 6, D_MODEL), lambda i: (src(i), 0, 0))],
        out_specs=pl.BlockSpec((tm, D_MODEL), lambda i: (i, 0)),
        out_shape=jax.ShapeDtypeStruct((n_out * tm, D_MODEL), F32),
        compiler_params=_cparams(("parallel",)),
        name="moe_combine",
    )(x, ya, yb, route, rowmod)


def moe_ffn(x, g, rowmod, w_router, w1, w3, w2, layer, latent_only=None):
    m = x.shape[0]
    tm = MOE_TM
    n_tiles_max = (2 * m) // tm + N_EXPERTS
    p = n_tiles_max * tm
    wr_pad = jnp.zeros((D_MODEL, 128), F32).at[:, :N_EXPERTS].set(w_router)
    h, route, cnt = moe_router(x, g, rowmod, wr_pad, p)
    experts = jnp.arange(N_EXPERTS, dtype=jnp.int32)
    counts = cnt[0, :N_EXPERTS].astype(jnp.int32)
    tiles_per = (counts + tm - 1) // tm
    tile_end = jnp.cumsum(tiles_per)
    grp_start = (tile_end - tiles_per) * tm
    cnt_start = jnp.cumsum(counts) - counts
    e12 = route[:m, 0:2].astype(jnp.int32)
    r12 = route[:m, 4:6].astype(jnp.int32)
    pos12 = jnp.sum(jnp.where(e12[:, :, None] == experts, grp_start, 0), axis=-1) + r12
    order = jnp.argsort(e12.reshape(-1), stable=True).astype(jnp.int32)
    tile_expert = jnp.minimum(
        jnp.sum(jnp.arange(n_tiles_max, dtype=jnp.int32)[:, None] >= tile_end[None, :], axis=1), N_EXPERTS - 1
    ).astype(jnp.int32)
    n_tiles = tile_end[-1:].astype(jnp.int32)
    rank = jnp.arange(p, dtype=jnp.int32) - jnp.repeat(grp_start[tile_expert], tm)
    in_use = rank < jnp.repeat(counts[tile_expert], tm)
    sorted_idx = jnp.clip(jnp.repeat(cnt_start[tile_expert], tm) + rank, 0, 2 * m - 1)
    src_token = jnp.where(in_use, jnp.take(order, sorted_idx, mode="clip") // 2, jnp.arange(p, dtype=jnp.int32) % m)
    h_sorted = jnp.take(h, src_token, axis=0, mode="clip")
    y_sorted = moe_grouped_ffn(h_sorted, tile_expert, n_tiles, w1, w3, w2, layer)
    ya = jnp.take(y_sorted, pos12[:, 0], axis=0, mode="clip")
    yb = jnp.take(y_sorted, pos12[:, 1], axis=0, mode="clip")
    return moe_combine(x, ya, yb, route, rowmod, latent_only=latent_only)


def _dir_split(a, n):
    m = a.shape[0]
    a3 = a.reshape(m, 2, n)
    return jnp.transpose(a3, (1, 0, 2)), jnp.transpose(a3, (1, 2, 0))


def ssd_layer(x, g, rowmod, w_in, conv_w, conv_b, dt_bias, a_log, d_skip, norm_g, w_out, *, bsz, ctx_len):
    m = x.shape[0]
    n_main = SSD_INNER + SSD_INNER + 2 * SSD_GROUPS * SSD_N
    zx, dt_raw = in_projection(x, g, rowmod, [w_in[:, :n_main].astype(BF16), w_in[:, n_main:].astype(BF16)],
                               [BF16, F32], shift_idx=0, scale_idx=1)
    u = dwconv_silu(zx, SSD_INNER, n_main - SSD_INNER, conv_w, conv_b, bsz=bsz, ctx_len=ctx_len)
    dt_dir, dtt_dir = _dir_split(dt_raw, SSD_HEADS)
    yf = ssd_scan(u, dt_dir, dtt_dir, dt_bias, a_log, bsz=bsz, ctx_len=ctx_len, bwd=False)
    yb = ssd_scan(u, dt_dir, dtt_dir, dt_bias, a_log, bsz=bsz, ctx_len=ctx_len, bwd=True)
    tm = PROJ_TM
    d_x = jnp.repeat(d_skip.astype(F32), SSD_P).reshape(1, SSD_INNER)
    specs = [pl.BlockSpec((tm, SSD_INNER), lambda i: (i, 0)),
             pl.BlockSpec((tm, SSD_INNER), lambda i: (i, 0)),
             pl.BlockSpec((tm, SSD_INNER), lambda i: (i, 0)),
             pl.BlockSpec((tm, SSD_INNER), lambda i: (i, 0)),
             pl.BlockSpec((1, SSD_INNER), lambda i: (0, 0)),
             pl.BlockSpec((1, SSD_INNER), lambda i: (0, 0))]
    return out_projection(_ssd_finish, [yf, yb, u, zx, d_x, norm_g.reshape(1, SSD_INNER)], specs,
                          w_out.astype(BF16), x, rowmod, gate_idx=2, tm=tm)


def hgrn_layer(x, g, rowmod, w_in, lb, norm_g, w_out, *, bsz, ctx_len):
    proj, = in_projection(x, g, rowmod, [w_in.astype(BF16)], [BF16], shift_idx=0, scale_idx=1)
    lb = lb.astype(F32).reshape(2, 1, D_MODEL)
    lbs = (jnp.log(lb), jnp.log1p(-lb), 1.0 - lb)
    of, ob = gla_scan(proj, *lbs, bsz=bsz, ctx_len=ctx_len)
    tm = PROJ_TM
    specs = [pl.BlockSpec((tm, D_MODEL), lambda i: (i, 0)),
             pl.BlockSpec((tm, D_MODEL), lambda i: (i, 0)),
             pl.BlockSpec((tm, D_MODEL), lambda i: (i, 4)),
             pl.BlockSpec((1, D_MODEL), lambda i: (0, 0))]
    return out_projection(_hgrn_finish, [of, ob, proj, norm_g.reshape(1, D_MODEL)], specs,
                          w_out.astype(BF16), x, rowmod, gate_idx=2, tm=tm)


def _attn_head_perm():
    r_per = ATT_HEADS // ATT_KV
    heads = [(2 * p + j) * r_per + r for p in range(ATT_KV // 2) for r in range(r_per) for j in range(2)]
    return np.concatenate([np.arange(h * ATT_HD, (h + 1) * ATT_HD) for h in heads])


def _rope_tables(seq_len, ctx_len, grid_w):
    rows = seq_len // grid_w
    row = jnp.repeat(jnp.arange(rows, dtype=F32), grid_w)
    col = jnp.tile(jnp.arange(grid_w, dtype=F32), rows)
    inv = ROPE_THETA ** (-jnp.arange(ROPE_FREQS, dtype=F32) / ROPE_FREQS)
    ang_r = row[:, None] * inv
    ang_c = col[:, None] * inv
    ang = jnp.concatenate([ang_r, ang_r, ang_c, ang_c], axis=-1)
    cos = jnp.concatenate([jnp.ones((ctx_len, ATT_HD), F32), jnp.cos(ang)], axis=0)
    sin = jnp.concatenate([jnp.zeros((ctx_len, ATT_HD), F32), jnp.sin(ang)], axis=0)
    return jnp.tile(cos, (1, 2)), jnp.tile(sin, (1, 2))


def _rope_matrices():
    r64 = np.zeros((ATT_HD, ATT_HD), np.float32)
    fq = ROPE_FREQS
    for ax in range(2):
        o = ax * 2 * fq
        for i in range(fq):
            r64[o + fq + i, o + i] = -1.0
            r64[o + i, o + fq + i] = 1.0
    n = ATT_HEADS
    bd = np.kron(np.eye(n, dtype=np.float32), np.ones((ATT_HD, ATT_HD), np.float32))
    rot = np.kron(np.eye(n, dtype=np.float32), r64)
    return jnp.asarray(bd, BF16), jnp.asarray(rot, BF16)


def attn_layer(x, g, rowmod, w_qkv, q_g, k_g, w_o, *, bsz, ctx_len, grid_w):
    m = x.shape[0]
    seq_len = m // bsz - ctx_len
    perm = _attn_head_perm()
    nq = ATT_HEADS * ATT_HD
    w = jnp.concatenate([w_qkv[:, :nq][:, perm], w_qkv[:, nq:]], axis=1).astype(BF16)
    cos_t, sin_t = _rope_tables(seq_len, ctx_len, grid_w)
    bd, rot = _rope_matrices()
    qg = jnp.tile(q_g.astype(F32), ATT_HEADS).reshape(1, nq)
    kg = jnp.tile(k_g.astype(F32), ATT_KV).reshape(1, ATT_KV * ATT_HD)
    q, k, v = attn_qkv(x, g, rowmod, w, cos_t, sin_t, qg, kg, bd, rot, bsz=bsz, shift_idx=0, scale_idx=1)
    o = attention(q, k, v, bsz=bsz, ctx_len=ctx_len)
    tm = PROJ_TM
    specs = [pl.BlockSpec((tm, nq), lambda i: (i, 0))]
    return out_projection(_identity_pro, [o], specs, w_o[perm, :].astype(BF16), x, rowmod, gate_idx=2, tm=tm)


def mlstm_layer(x, g, rowmod, w_up, conv_w, conv_b, w_q, w_k, w_v, w_gate, b_gate, skip, norm_g, w_down,
                *, bsz, ctx_len):
    up, = in_projection(x, g, rowmod, [w_up.astype(BF16)], [BF16], shift_idx=0, scale_idx=1)
    xc = dwconv_silu(up, 0, ML_INNER, conv_w, conv_b, bsz=bsz, ctx_len=ctx_len)
    q, k, v, gates = mlstm_qkv(xc, up, w_q.astype(BF16), w_k.astype(BF16), w_v.astype(BF16),
                               w_gate.astype(BF16), b_gate)
    gt_dir, gtt_dir = _dir_split(gates, 2 * ML_HEADS)
    hf = mlstm_scan(q, k, v, gt_dir, gtt_dir, bsz=bsz, ctx_len=ctx_len, bwd=False)
    hb = mlstm_scan(q, k, v, gt_dir, gtt_dir, bsz=bsz, ctx_len=ctx_len, bwd=True)
    tm = PROJ_TM
    specs = [pl.BlockSpec((tm, ML_INNER), lambda i: (i, 0)),
             pl.BlockSpec((tm, ML_INNER), lambda i: (i, 0)),
             pl.BlockSpec((tm, ML_INNER), lambda i: (i, 0)),
             pl.BlockSpec((tm, ML_INNER), lambda i: (i, 1)),
             pl.BlockSpec((1, ML_INNER), lambda i: (0, 0)),
             pl.BlockSpec((1, ML_INNER), lambda i: (0, 0))]
    return out_projection(_mlstm_finish, [hf, hb, xc, up, skip.reshape(1, ML_INNER), norm_g.reshape(1, ML_INNER)],
                          specs, w_down.astype(BF16), x, rowmod, gate_idx=2, tm=tm)


def kernel(x, c, ctx, c_ctx, ada_w, ada_b, norm_g, ssd_w_in, ssd_conv_w, ssd_conv_b, ssd_dt_bias, ssd_a_log, ssd_d, ssd_norm_g, ssd_w_out, hgrn_w_in, hgrn_lb, hgrn_norm_g, hgrn_w_out, attn_w_qkv, attn_q_g, attn_k_g, attn_w_o, mlstm_w_up, mlstm_conv_w, mlstm_conv_b, mlstm_w_q, mlstm_w_k, mlstm_w_v, mlstm_w_gate, mlstm_b_gate, mlstm_skip, mlstm_norm_g, mlstm_w_down, ffn_w1, ffn_w3, ffn_w2, moe_router, moe_w1, moe_w3, moe_w2):
    bsz, seq_len, _ = x.shape
    ctx_len = ctx.shape[1]
    depth = ada_w.shape[0]
    grid_w = 64
    t_all = ctx_len + seq_len
    m = bsz * t_all
    xa = jnp.concatenate([ctx, x], axis=1).reshape(m, D_MODEL)
    c_pad = jnp.zeros((8, D_MODEL), F32).at[:bsz].set(c).at[bsz].set(c_ctx)
    groups_per_batch = t_all // ROW_GROUP
    ctx_groups = ctx_len // ROW_GROUP
    gidx = np.array([bsz if (gi % groups_per_batch) < ctx_groups else gi // groups_per_batch
                     for gi in range(m // ROW_GROUP)], np.int32)
    lb_all = jnp.cumsum(jax.nn.softmax(hgrn_lb.astype(F32), axis=1), axis=1)
    lb_all = lb_all - lb_all[:, :1]
    kw = dict(bsz=bsz, ctx_len=ctx_len)
    ffn_w = [w.astype(BF16) for w in (ffn_w1, ffn_w3, ffn_w2)]
    moe_w = (moe_w1, moe_w3, moe_w2)
    for i in range(depth):
        mod = ada_modulation(c_pad, ada_w, ada_b, i).reshape(8, 6, D_MODEL)
        rowmod = mod[gidx]
        kind, j = i % 4, i // 4
        if kind == 0:
            xa = ssd_layer(xa, norm_g[i, 0], rowmod, ssd_w_in[j], ssd_conv_w[j], ssd_conv_b[j], ssd_dt_bias[j],
                           ssd_a_log[j], ssd_d[j], ssd_norm_g[j], ssd_w_out[j], **kw)
        elif kind == 1:
            xa = hgrn_layer(xa, norm_g[i, 0], rowmod, hgrn_w_in[j], lb_all[:, i], hgrn_norm_g[j], hgrn_w_out[j], **kw)
        elif kind == 2:
            xa = attn_layer(xa, norm_g[i, 0], rowmod, attn_w_qkv[j], attn_q_g[j], attn_k_g[j], attn_w_o[j],
                            grid_w=grid_w, **kw)
        else:
            xa = mlstm_layer(xa, norm_g[i, 0], rowmod, mlstm_w_up[j], mlstm_conv_w[j], mlstm_conv_b[j], mlstm_w_q[j],
                             mlstm_w_k[j], mlstm_w_v[j], mlstm_w_gate[j], mlstm_b_gate[j], mlstm_skip[j],
                             mlstm_norm_g[j], mlstm_w_down[j], **kw)
        if i % 2 == 0:
            xa = dense_ffn(xa, norm_g[i, 1], rowmod, *ffn_w, i // 2)
        else:
            last = i == depth - 1
            xa = moe_ffn(xa, norm_g[i, 1], rowmod, moe_router[i // 2], *moe_w, i // 2,
                         latent_only=(bsz, ctx_len) if last else None)
            if last:
                return xa.reshape(bsz, seq_len, D_MODEL)
    return xa.reshape(bsz, t_all, D_MODEL)[:, ctx_len:]
```
